```python
import jax, jax.numpy as jnp
from jax import lax
import numpy as np

D_MODEL = 2048
BATCH = 8
SEQ = 2048
DEPTH = 1
DEC_BATCH = 16
DEC_SEQ = 32
PAST_LEN = 4096

CHUNK = 64
D_CONV = 1024
CONV_W = 3
N_HEADS = 16
HEAD_DIM = 128
D_ATTN = N_HEADS * HEAD_DIM
N_GROUPS = 4
EXPERTS_PER_GROUP = 8
N_EXPERTS = N_GROUPS * EXPERTS_PER_GROUP
TOP_K = 2
D_EXPERT = 512
Q_BLOCK = 128
N_IN = 3 * D_CONV + 3 * D_ATTN + N_HEADS
SPLIT_POINTS = (D_CONV, 2 * D_CONV, 3 * D_CONV, 3 * D_CONV + D_ATTN, 3 * D_CONV + 2 * D_ATTN, 3 * D_CONV + 3 * D_ATTN)
EPS = 1e-6
NEG_INF = -1e30

kernel_name = "hybrid_conv_fox_hmoe_stream_step"


def rmsnorm(x, g):
    x32 = x.astype(jnp.float32)
    y = x32 * lax.rsqrt(jnp.mean(x32 * x32, axis=-1, keepdims=True) + EPS)
    return (y * g.astype(jnp.float32)).astype(x.dtype)


def modulate(x, g, shift, scale):
    return rmsnorm(x, g) * (1 + scale[:, None, :]) + shift[:, None, :]


def forgetting_attention(q, k, v, f_q, f_k, q_pos, k_pos):
    b, s, h, d = q.shape
    qb = min(Q_BLOCK, s)
    nb = s // qb
    scale = HEAD_DIM ** -0.5
    q_blocks = q.reshape(b, nb, qb, h, d).transpose(1, 0, 2, 3, 4)
    fq_blocks = f_q.reshape(b, nb, qb, h).transpose(1, 0, 2, 3)
    pos_blocks = q_pos.reshape(nb, qb)
    fk_t = f_k.transpose(0, 2, 1)

    def one_block(args):
        qi, fqi, pi = args
        logits = jnp.einsum('bqhd,bkhd->bhqk', qi, k).astype(jnp.float32) * scale
        logits = logits + fqi.transpose(0, 2, 1)[:, :, :, None] - fk_t[:, :, None, :]
        visible = k_pos[None, :] <= pi[:, None]
        logits = jnp.where(visible[None, None], logits, NEG_INF)
        p = jax.nn.softmax(logits, axis=-1).astype(v.dtype)
        return jnp.einsum('bhqk,bkhd->bqhd', p, v)

    out = lax.map(one_block, (q_blocks, fq_blocks, pos_blocks))
    return out.transpose(1, 0, 2, 3, 4).reshape(b, s, h * d)


def hierarchical_moe(h, w_rg, b_rg, w_re, b_re, w1, w3, w2):
    b, s, d = h.shape
    t = h.reshape(b * s, d)
    group_prob = jax.nn.softmax((t @ w_rg).astype(jnp.float32) + b_rg.astype(jnp.float32), axis=-1)
    g_val, g_idx = lax.top_k(group_prob, 1)
    expert_logits = ((t @ w_re).astype(jnp.float32) + b_re.astype(jnp.float32)).reshape(-1, N_GROUPS, EXPERTS_PER_GROUP)
    in_group = jnp.take_along_axis(expert_logits, g_idx[:, :, None], axis=1)[:, 0]
    e_val, e_idx = lax.top_k(jax.nn.softmax(in_group, axis=-1), TOP_K)
    e_val = e_val / jnp.sum(e_val, axis=-1, keepdims=True)
    weights = g_val * e_val
    expert_id = g_idx * EXPERTS_PER_GROUP + e_idx
    combine = jnp.sum(jax.nn.one_hot(expert_id, N_EXPERTS, dtype=jnp.float32) * weights[..., None], axis=1)

    def expert_step(acc, args):
        w1e, w3e, w2e, ce = args
        y = (jax.nn.silu(t @ w1e) * (t @ w3e)) @ w2e
        return acc + ce[:, None] * y.astype(jnp.float32), None

    acc, _ = lax.scan(expert_step, jnp.zeros((b * s, d), jnp.float32), (w1, w3, w2, combine.T))
    return acc.astype(h.dtype).reshape(b, s, d)


def trunk_layer(x, c, conv_prev, k_past, v_past, lf_past, w_ada, b_ada, norm1_g, norm2_g, w_in,
                conv_w, conv_b, b_forget, w_br_conv, w_br_attn, w_gate, b_gate, w_o,
                w_rg, b_rg, w_re, b_re, w1, w3, w2):
    b, s, _ = x.shape
    p = k_past.shape[1]
    mod = c @ w_ada + b_ada
    shift1, scale1, gate1, shift2, scale2, gate2 = jnp.split(mod, 6, axis=-1)
    h = modulate(x, norm1_g, shift1, scale1)
    proj = h @ w_in
    bg, cg, vc, q, k, v, fl = jnp.split(proj, SPLIT_POINTS, axis=-1)
    u = cg * vc
    padded = jnp.concatenate([conv_prev.astype(u.dtype), u], axis=1)
    conv = conv_b + sum(conv_w[j] * padded[:, j:j + s] for j in range(CONV_W))
    y_conv = bg * conv
    new_conv = padded[:, s:]
    q = q.reshape(b, s, N_HEADS, HEAD_DIM)
    k = k.reshape(b, s, N_HEADS, HEAD_DIM)
    v = v.reshape(b, s, N_HEADS, HEAD_DIM)
    lf = jax.nn.log_sigmoid(fl.astype(jnp.float32) + b_forget.astype(jnp.float32))
    k_all = jnp.concatenate([k_past.astype(k.dtype), k], axis=1)
    v_all = jnp.concatenate([v_past.astype(v.dtype), v], axis=1)
    f_all = jnp.cumsum(jnp.concatenate([lf_past.astype(jnp.float32), lf], axis=1), axis=1)
    k_pos = jnp.arange(p + s)
    q_pos = p + jnp.arange(s)
    y_attn = forgetting_attention(q, k_all, v_all, f_all[:, p:], f_all, q_pos, k_pos)
    gates = jax.nn.sigmoid(h @ w_gate + b_gate)
    g_conv, g_attn = jnp.split(gates, 2, axis=-1)
    merged = g_conv * (y_conv @ w_br_conv) + g_attn * (y_attn @ w_br_attn)
    x = x + gate1[:, None, :] * (merged @ w_o)
    h2 = modulate(x, norm2_g, shift2, scale2)
    x = x + gate2[:, None, :] * hierarchical_moe(h2, w_rg, b_rg, w_re, b_re, w1, w3, w2)
    return x, k, v, lf, new_conv


def setup_inputs(seed: int = 0) -> dict:
    key = jax.random.key(seed)
    ks = jax.random.split(key, 40)
    f32 = jnp.float32

    def nrm(k, shape, std):
        return jax.random.normal(k, shape, f32) * std

    dm = D_MODEL ** -0.5
    return {
        "x_prompt": nrm(ks[0], (BATCH, SEQ, D_MODEL), 1.0),
        "x_sample": nrm(ks[1], (DEC_BATCH, DEC_SEQ, D_MODEL), 1.0),
        "cache_k": nrm(ks[2], (DEPTH, DEC_BATCH, PAST_LEN, N_HEADS, HEAD_DIM), 1.0),
        "cache_v": nrm(ks[3], (DEPTH, DEC_BATCH, PAST_LEN, N_HEADS, HEAD_DIM), 1.0),
        "cache_logf": jax.nn.log_sigmoid(2.0 + nrm(ks[4], (DEPTH, DEC_BATCH, PAST_LEN, N_HEADS), 0.5)),
        "state_conv": nrm(ks[5], (DEPTH, DEC_BATCH, CONV_W - 1, D_CONV), 1.0),
        "c_prompt": nrm(ks[6], (BATCH, D_MODEL), 1.0),
        "c_sample": nrm(ks[7], (DEC_BATCH, D_MODEL), 1.0),
        "w_ada": nrm(ks[8], (DEPTH, D_MODEL, 6 * D_MODEL), 0.5 * dm),
        "b_ada": nrm(ks[9], (DEPTH, 6 * D_MODEL), 0.02),
        "norm1_g": 1.0 + nrm(ks[10], (DEPTH, D_MODEL), 0.02),
        "norm2_g": 1.0 + nrm(ks[11], (DEPTH, D_MODEL), 0.02),
        "normf_g": 1.0 + nrm(ks[12], (D_MODEL,), 0.02),
        "w_in": nrm(ks[13], (DEPTH, D_MODEL, N_IN), dm),
        "conv_w": nrm(ks[14], (DEPTH, CONV_W, D_CONV), 0.5),
        "conv_b": nrm(ks[15], (DEPTH, D_CONV), 0.02),
        "b_forget": 2.0 + nrm(ks[16], (DEPTH, N_HEADS), 0.1),
        "w_br_conv": nrm(ks[17], (DEPTH, D_CONV, D_MODEL), D_CONV ** -0.5),
        "w_br_attn": nrm(ks[18], (DEPTH, D_ATTN, D_MODEL), D_ATTN ** -0.5),
        "w_gate": nrm(ks[19], (DEPTH, D_MODEL, 2 * D_MODEL), dm),
        "b_gate": nrm(ks[20], (DEPTH, 2 * D_MODEL), 0.02),
        "w_o": nrm(ks[21], (DEPTH, D_MODEL, D_MODEL), dm),
        "w_rg": nrm(ks[22], (DEPTH, D_MODEL, N_GROUPS), dm),
        "b_rg": nrm(ks[23], (DEPTH, N_GROUPS), 0.01),
        "w_re": nrm(ks[24], (DEPTH, D_MODEL, N_EXPERTS), dm),
        "b_re": nrm(ks[25], (DEPTH, N_EXPERTS), 0.01),
        "w1": nrm(ks[26], (DEPTH, N_EXPERTS, D_MODEL, D_EXPERT), dm),
        "w3": nrm(ks[27], (DEPTH, N_EXPERTS, D_MODEL, D_EXPERT), dm),
        "w2": nrm(ks[28], (DEPTH, N_EXPERTS, D_EXPERT, D_MODEL), D_EXPERT ** -0.5),
    }


def reference(x_prompt, x_sample, cache_k, cache_v, cache_logf, state_conv, c_prompt, c_sample,
              w_ada, b_ada, norm1_g, norm2_g, normf_g, w_in, conv_w, conv_b, b_forget,
              w_br_conv, w_br_attn, w_gate, b_gate, w_o, w_rg, b_rg, w_re, b_re, w1, w3, w2):
    xp, xs = x_prompt, x_sample
    kp_l, vp_l, lfp_l, cvp_l = [], [], [], []
    ks_l, vs_l, lfs_l, cvs_l = [], [], [], []
    for l in range(DEPTH):
        weights = (w_ada[l], b_ada[l], norm1_g[l], norm2_g[l], w_in[l], conv_w[l], conv_b[l], b_forget[l],
                   w_br_conv[l], w_br_attn[l], w_gate[l], b_gate[l], w_o[l], w_rg[l], b_rg[l], w_re[l], b_re[l],
                   w1[l], w3[l], w2[l])
        b = xp.shape[0]
        xp, kp, vp, lfp, cvp = trunk_layer(
            xp, c_prompt,
            jnp.zeros((b, CONV_W - 1, D_CONV), xp.dtype),
            jnp.zeros((b, 0, N_HEADS, HEAD_DIM), xp.dtype),
            jnp.zeros((b, 0, N_HEADS, HEAD_DIM), xp.dtype),
            jnp.zeros((b, 0, N_HEADS), jnp.float32),
            *weights)
        xs, ksn, vsn, lfsn, cvsn = trunk_layer(
            xs, c_sample, state_conv[l], cache_k[l], cache_v[l], cache_logf[l], *weights)
        kp_l.append(kp); vp_l.append(vp); lfp_l.append(lfp); cvp_l.append(cvp)
        ks_l.append(ksn); vs_l.append(vsn); lfs_l.append(lfsn); cvs_l.append(cvsn)
    y_prompt = rmsnorm(xp, normf_g)
    y_sample = rmsnorm(xs, normf_g)
    return (y_prompt, y_sample,
            jnp.stack(kp_l), jnp.stack(vp_l), jnp.stack(lfp_l), jnp.stack(cvp_l),
            jnp.stack(ks_l), jnp.stack(vs_l), jnp.stack(lfs_l), jnp.stack(cvs_l))
```

```python
import functools

import jax
import jax.numpy as jnp
from jax import lax
from jax.experimental import pallas as pl
from jax.experimental.pallas import tpu as pltpu

F32 = jnp.float32
BF16 = jnp.bfloat16
I32 = jnp.int32

EPS = 1e-6
NEG_INF = -1e30
TOP_K = 2
LANE = 128
BF16_ROWS = 16
VMEM_LIMIT_BYTES = 56 * 1024 * 1024
NT_DIMS = (((1,), (1,)), ((), ()))


def _tile(n, pref, align):
    if n <= pref:
        return n
    t = (pref // align) * align
    while t >= align:
        if n % t == 0:
            return t
        t -= align
    return n


def _round_up(n, m):
    return (n + m - 1) // m * m


def _params(*sem):
    return pltpu.CompilerParams(dimension_semantics=sem, vmem_limit_bytes=VMEM_LIMIT_BYTES)


def _rowvec(vec, seq_len, tm):
    n_seq, d = vec.shape
    if tm <= seq_len:
        per = seq_len // tm
        return vec.reshape(n_seq, 1, d), pl.BlockSpec((1, 1, d), lambda i, *_: (i // per, 0, 0))
    return jnp.repeat(vec, seq_len, axis=0), pl.BlockSpec((tm, d), lambda i, *_: (i, 0))


def _bc(ref):
    v = ref[...]
    return v.reshape(v.shape[-2], v.shape[-1])


def _rms(x, g):
    return x * lax.rsqrt(jnp.mean(x * x, axis=-1, keepdims=True) + EPS) * g


def _ada_kernel(c_ref, w_ref, b_ref, o_ref):
    o_ref[...] = jnp.dot(c_ref[...].astype(BF16), w_ref[...].astype(BF16), preferred_element_type=F32) + b_ref[...]


def _ada(c_all, w, b):
    r, d = c_all.shape
    n = w.shape[1]
    tn = _tile(n, 1024, LANE)
    return pl.pallas_call(
        _ada_kernel,
        grid=(n // tn,),
        in_specs=[
            pl.BlockSpec((r, d), lambda j: (0, 0)),
            pl.BlockSpec((d, tn), lambda j: (0, j)),
            pl.BlockSpec((1, tn), lambda j: (0, j)),
        ],
        out_specs=pl.BlockSpec((r, tn), lambda j: (0, j)),
        out_shape=jax.ShapeDtypeStruct((r, n), F32),
        compiler_params=_params("parallel"),
        name="ada",
    )(c_all, w, b.reshape(1, n))


def _norm_kernel(x_ref, g_ref, sh_ref, sc_ref, h_ref):
    h = _rms(x_ref[...], g_ref[...]) * (1.0 + _bc(sc_ref)) + _bc(sh_ref)
    h_ref[...] = h.astype(h_ref.dtype)


def _norm(x, g, shift, scale, seq_len):
    m, d = x.shape
    tm = _tile(seq_len, 512, 8)
    sh, sh_spec = _rowvec(shift, seq_len, tm)
    sc, sc_spec = _rowvec(scale, seq_len, tm)
    return pl.pallas_call(
        _norm_kernel,
        grid=(m // tm,),
        in_specs=[pl.BlockSpec((tm, d), lambda i: (i, 0)), pl.BlockSpec((1, d), lambda i: (0, 0)), sh_spec, sc_spec],
        out_specs=pl.BlockSpec((tm, d), lambda i: (i, 0)),
        out_shape=jax.ShapeDtypeStruct((m, d), BF16),
        compiler_params=_params("parallel"),
        name="norm",
    )(x, g.reshape(1, d), sh, sc)


def _proj_kernel(a_ref, w_ref, o_ref, *, scale):
    acc = jnp.dot(a_ref[...], w_ref[...], preferred_element_type=F32)
    if scale != 1.0:
        acc = acc * scale
    o_ref[...] = acc.astype(o_ref.dtype)


def _proj(a, w, col_off, ncols, out_dtype, scale=1.0):
    m, k = a.shape
    tm = _tile(m, 1024, BF16_ROWS)
    tn = _tile(ncols, 512, LANE)
    while tn > LANE and (col_off % tn or ncols % tn):
        tn -= LANE
    assert col_off % tn == 0 and ncols % tn == 0
    off = col_off // tn
    return pl.pallas_call(
        functools.partial(_proj_kernel, scale=scale),
        grid=(m // tm, ncols // tn),
        in_specs=[pl.BlockSpec((tm, k), lambda i, j: (i, 0)), pl.BlockSpec((k, tn), lambda i, j: (0, off + j))],
        out_specs=pl.BlockSpec((tm, tn), lambda i, j: (i, j)),
        out_shape=jax.ShapeDtypeStruct((m, ncols), out_dtype),
        compiler_params=_params("parallel", "parallel"),
        name="proj",
    )(a, w)


CONV_TAIL_ROWS = 8


def _conv_kernel(*refs, seq_len, has_prev):
    if has_prev:
        h_ref, wb_ref, wc_ref, wv_ref, cw_ref, cb_ref, p1_ref, p2_ref, y_ref, tail_ref = refs
    else:
        h_ref, wb_ref, wc_ref, wv_ref, cw_ref, cb_ref, y_ref, tail_ref = refs
    h = h_ref[...]
    bg = jnp.dot(h, wb_ref[...], preferred_element_type=F32)
    cg = jnp.dot(h, wc_ref[...], preferred_element_type=F32)
    vc = jnp.dot(h, wv_ref[...], preferred_element_type=F32)
    u = cg * vc
    tm = u.shape[0]
    t = lax.broadcasted_iota(I32, (tm, 1), 0)
    if tm > seq_len:
        t = jnp.bitwise_and(t, seq_len - 1) if seq_len & (seq_len - 1) == 0 else lax.rem(t, seq_len)
    u1 = jnp.where(t >= 1, pltpu.roll(u, 1, 0), p1_ref[...] if has_prev else 0.0)
    u2 = jnp.where(t >= 2, pltpu.roll(u, 2, 0), p2_ref[...] if has_prev else 0.0)
    cw = cw_ref[...]
    conv = cb_ref[...] + ((cw[0:1] * u2 + cw[1:2] * u1) + cw[2:3] * u)
    y_ref[...] = (bg * conv).astype(y_ref.dtype)
    for s in range(tm // seq_len):
        tail_ref[s] = u[(s + 1) * seq_len - CONV_TAIL_ROWS:(s + 1) * seq_len, :]


def _conv_branch(h, w_in_bf, conv_w, conv_b, prev, seq_len, d_conv):
    m, d = h.shape
    assert conv_w.shape[0] == 3 and seq_len >= CONV_TAIL_ROWS
    tm = seq_len if seq_len >= 512 else m
    assert m % tm == 0 and tm % seq_len == 0
    tc = _tile(d_conv, 256, LANE)
    nc = d_conv // tc
    n_seq = m // seq_len
    seq_per_tile = tm // seq_len
    in_specs = [
        pl.BlockSpec((tm, d), lambda i, j: (i, 0)),
        pl.BlockSpec((d, tc), lambda i, j: (0, j)),
        pl.BlockSpec((d, tc), lambda i, j: (0, nc + j)),
        pl.BlockSpec((d, tc), lambda i, j: (0, 2 * nc + j)),
        pl.BlockSpec((3, tc), lambda i, j: (0, j)),
        pl.BlockSpec((1, tc), lambda i, j: (0, j)),
    ]
    args = [h, w_in_bf, w_in_bf, w_in_bf, conv_w, conv_b.reshape(1, d_conv)]
    if prev is not None:
        zeros = jnp.zeros((n_seq, seq_len, d_conv), F32)
        p1 = zeros.at[:, 0].set(prev[:, 1]).reshape(m, d_conv)
        p2 = zeros.at[:, 0].set(prev[:, 0]).at[:, 1].set(prev[:, 1]).reshape(m, d_conv)
        in_specs += [pl.BlockSpec((tm, tc), lambda i, j: (i, j))] * 2
        args += [p1, p2]
    return pl.pallas_call(
        functools.partial(_conv_kernel, seq_len=seq_len, has_prev=prev is not None),
        grid=(m // tm, nc),
        in_specs=in_specs,
        out_specs=[
            pl.BlockSpec((tm, tc), lambda i, j: (i, j)),
            pl.BlockSpec((seq_per_tile, CONV_TAIL_ROWS, tc), lambda i, j: (i, 0, j)),
        ],
        out_shape=[
            jax.ShapeDtypeStruct((m, d_conv), BF16),
            jax.ShapeDtypeStruct((n_seq, CONV_TAIL_ROWS, d_conv), F32),
        ],
        compiler_params=_params("parallel", "parallel"),
        name="conv_branch",
    )(*args)


def _forget_kernel(wf_ref, h_ref, b_ref, lf_ref):
    fl = lax.dot_general(wf_ref[...], h_ref[...], NT_DIMS, preferred_element_type=F32)
    lf_ref[...] = jax.nn.log_sigmoid(fl + b_ref[...])


def _forget(h, wf_t, b_f):
    m, d = h.shape
    hp = wf_t.shape[0]
    tm = _tile(m, 1024, LANE)
    return pl.pallas_call(
        _forget_kernel,
        grid=(m // tm,),
        in_specs=[
            pl.BlockSpec((hp, d), lambda i: (0, 0)),
            pl.BlockSpec((tm, d), lambda i: (i, 0)),
            pl.BlockSpec((hp, 1), lambda i: (0, 0)),
        ],
        out_specs=pl.BlockSpec((hp, tm), lambda i: (0, i)),
        out_shape=jax.ShapeDtypeStruct((hp, m), F32),
        compiler_params=_params("parallel"),
        name="forget",
    )(wf_t, h, b_f)


CUMSUM_CHUNK = LANE


def _cumsum_kernel(x_ref, o_ref):
    r, n = x_ref.shape
    c = CUMSUM_CHUNK
    tri = (lax.broadcasted_iota(I32, (c, c), 0) <= lax.broadcasted_iota(I32, (c, c), 1)).astype(F32)
    carry = jnp.zeros((r, 1), F32)
    for j in range(n // c):
        cs = jnp.dot(x_ref[:, j * c:(j + 1) * c], tri, preferred_element_type=F32, precision=lax.Precision.HIGHEST)
        cs = cs + carry
        o_ref[:, j * c:(j + 1) * c] = cs
        carry = cs[:, c - 1:c]


def _cumsum_lanes(x):
    r, n = x.shape
    tr = _tile(r, 128, 8)
    return pl.pallas_call(
        _cumsum_kernel,
        grid=(r // tr,),
        in_specs=[pl.BlockSpec((tr, n), lambda i: (i, 0))],
        out_specs=pl.BlockSpec((tr, n), lambda i: (i, 0)),
        out_shape=jax.ShapeDtypeStruct((r, n), F32),
        compiler_params=_params("parallel"),
        name="cumsum",
    )(x)


def _attn_prompt_kernel(q_ref, k_ref, v_ref, f_ref, o_ref, *, tq):
    seq_len = q_ref.shape[0]
    kb = k_ref[...].astype(BF16)
    vb = v_ref[...].astype(BF16)
    f = f_ref[...].reshape(1, seq_len)
    causal = lax.broadcasted_iota(I32, (tq, tq), 1) <= lax.broadcasted_iota(I32, (tq, tq), 0)
    for qi in range(seq_len // tq):
        lo = qi * tq
        q = q_ref[lo:lo + tq, :]
        sd = lax.dot_general(q, kb[lo:lo + tq], NT_DIMS, preferred_element_type=F32) - f[:, lo:lo + tq]
        sd = jnp.where(causal, sd, NEG_INF)
        m = jnp.max(sd, axis=-1, keepdims=True)
        if qi > 0:
            sp = lax.dot_general(q, kb[:lo], NT_DIMS, preferred_element_type=F32) - f[:, :lo]
            m = jnp.maximum(m, jnp.max(sp, axis=-1, keepdims=True))
            pp = jnp.exp(sp - m)
            l = jnp.sum(pp, axis=-1, keepdims=True)
            o = jnp.dot(pp.astype(BF16), vb[:lo], preferred_element_type=F32)
        pd = jnp.exp(sd - m)
        ld = jnp.sum(pd, axis=-1, keepdims=True)
        od = jnp.dot(pd.astype(BF16), vb[lo:lo + tq], preferred_element_type=F32)
        if qi > 0:
            l, o = l + ld, o + od
        else:
            l, o = ld, od
        o_ref[lo:lo + tq, :] = (o / l).astype(o_ref.dtype)


def _attn_prompt(q, k, v, f, n_seq, seq_len, n_heads, dh):
    m = n_seq * seq_len
    tq = _tile(seq_len, 256, LANE)
    blk = pl.BlockSpec((seq_len, dh), lambda b, h: (b, h))
    return pl.pallas_call(
        functools.partial(_attn_prompt_kernel, tq=tq),
        grid=(n_seq, n_heads),
        in_specs=[blk, blk, blk, pl.BlockSpec((1, 1, seq_len), lambda b, h: (b * n_heads + h, 0, 0))],
        out_specs=blk,
        out_shape=jax.ShapeDtypeStruct((m, n_heads * dh), BF16),
        compiler_params=_params("parallel", "parallel"),
        name="attn_prompt",
    )(q, k, v, f)


def _attn_sample_kernel(q_ref, kc_ref, vc_ref, fp_ref, kn_ref, vn_ref, fn_ref, o_ref, m_sc, l_sc, acc_sc, *, n_heads, dh):
    kt = pl.program_id(1)
    lq = q_ref.shape[0]

    @pl.when(kt == 0)
    def _():
        m_sc[...] = jnp.full(m_sc.shape, NEG_INF, F32)
        l_sc[...] = jnp.zeros(l_sc.shape, F32)
        acc_sc[...] = jnp.zeros(acc_sc.shape, F32)

    q = q_ref[...]

    def update(kb, vb, fk, mask):
        s_heads = []
        for h in range(n_heads):
            s = lax.dot_general(q[:, h * dh:(h + 1) * dh], kb[:, h * dh:(h + 1) * dh], NT_DIMS, preferred_element_type=F32)
            s = s - fk[h:h + 1, :]
            if mask is not None:
                s = jnp.where(mask, s, NEG_INF)
            s_heads.append(s)
        s = jnp.concatenate(s_heads, axis=0)
        m_prev = m_sc[...]
        m_new = jnp.maximum(m_prev, jnp.max(s, axis=-1, keepdims=True))
        alpha = jnp.exp(m_prev - m_new)
        p = jnp.exp(s - m_new)
        l_sc[...] = alpha * l_sc[...] + jnp.sum(p, axis=-1, keepdims=True)
        pb = p.astype(BF16)
        o_heads = [
            jnp.dot(pb[h * lq:(h + 1) * lq], vb[:, h * dh:(h + 1) * dh], preferred_element_type=F32) for h in range(n_heads)
        ]
        acc_sc[...] = alpha * acc_sc[...] + jnp.concatenate(o_heads, axis=0)
        m_sc[...] = m_new

    update(kc_ref[0].astype(BF16), vc_ref[0].astype(BF16), fp_ref[0], None)

    @pl.when(kt == pl.num_programs(1) - 1)
    def _():
        pad = jnp.zeros((LANE - lq, n_heads * dh), BF16)
        kn = jnp.concatenate([kn_ref[...].astype(BF16), pad], axis=0)
        vn = jnp.concatenate([vn_ref[...].astype(BF16), pad], axis=0)
        mask = lax.broadcasted_iota(I32, (lq, LANE), 1) <= lax.broadcasted_iota(I32, (lq, LANE), 0)
        update(kn, vn, fn_ref[0], mask)
        out = acc_sc[...] / l_sc[...]
        o_ref[...] = jnp.concatenate([out[h * lq:(h + 1) * lq] for h in range(n_heads)], axis=1).astype(o_ref.dtype)


def _attn_sample(q, cache_k, cache_v, k_new, v_new, f_all, n_seq, lq, past, n_heads, dh):
    da = n_heads * dh
    tk = _tile(past, 512, LANE)
    assert lq <= LANE and lq % BF16_ROWS == 0 and past % LANE == 0
    qblk = pl.BlockSpec((lq, da), lambda b, t: (b, 0))
    cblk = pl.BlockSpec((1, tk, da), lambda b, t: (b, t, 0))
    return pl.pallas_call(
        functools.partial(_attn_sample_kernel, n_heads=n_heads, dh=dh),
        grid=(n_seq, past // tk),
        in_specs=[
            qblk,
            cblk,
            cblk,
            pl.BlockSpec((1, n_heads, tk), lambda b, t: (b, 0, t)),
            qblk,
            qblk,
            pl.BlockSpec((1, n_heads, LANE), lambda b, t: (b, 0, past // LANE)),
        ],
        out_specs=qblk,
        out_shape=jax.ShapeDtypeStruct((n_seq * lq, da), BF16),
        scratch_shapes=[
            pltpu.VMEM((n_heads * lq, 1), F32),
            pltpu.VMEM((n_heads * lq, 1), F32),
            pltpu.VMEM((n_heads * lq, dh), F32),
        ],
        compiler_params=_params("parallel", "arbitrary"),
        name="attn_sample",
    )(q, cache_k, cache_v, f_all, k_new, v_new, f_all)


def _merge_kernel(yc_ref, ya_ref, h_ref, wc_ref, wa_ref, wgc_ref, wga_ref, bgc_ref, bga_ref, o_ref):
    h = h_ref[...]
    a = jnp.dot(yc_ref[...], wc_ref[...], preferred_element_type=F32)
    b = jnp.dot(ya_ref[...], wa_ref[...], preferred_element_type=F32)
    gc = jax.nn.sigmoid(jnp.dot(h, wgc_ref[...], preferred_element_type=F32) + bgc_ref[...])
    ga = jax.nn.sigmoid(jnp.dot(h, wga_ref[...], preferred_element_type=F32) + bga_ref[...])
    o_ref[...] = (gc * a + ga * b).astype(o_ref.dtype)


def _merge(yc, ya, h, w_brc, w_bra, w_gate, b_gate):
    m, d = h.shape
    dc, da = yc.shape[1], ya.shape[1]
    tm = _tile(m, 1024, BF16_ROWS)
    tn = _tile(d, 512, LANE)
    nj = d // tn
    return pl.pallas_call(
        _merge_kernel,
        grid=(m // tm, nj),
        in_specs=[
            pl.BlockSpec((tm, dc), lambda i, j: (i, 0)),
            pl.BlockSpec((tm, da), lambda i, j: (i, 0)),
            pl.BlockSpec((tm, d), lambda i, j: (i, 0)),
            pl.BlockSpec((dc, tn), lambda i, j: (0, j)),
            pl.BlockSpec((da, tn), lambda i, j: (0, j)),
            pl.BlockSpec((d, tn), lambda i, j: (0, j)),
            pl.BlockSpec((d, tn), lambda i, j: (0, nj + j)),
            pl.BlockSpec((1, tn), lambda i, j: (0, j)),
            pl.BlockSpec((1, tn), lambda i, j: (0, nj + j)),
        ],
        out_specs=pl.BlockSpec((tm, tn), lambda i, j: (i, j)),
        out_shape=jax.ShapeDtypeStruct((m, d), BF16),
        compiler_params=_params("parallel", "parallel"),
        name="merge",
    )(yc, ya, h, w_brc, w_bra, w_gate, w_gate, b_gate, b_gate)


ROUTE_ID0, ROUTE_ID1, ROUTE_W0, ROUTE_W1 = 0, 1, 2, 3


def _wo_kernel(mg_ref, x_ref, wo_ref, g1_ref, ng_ref, sh_ref, sc_ref, wr_ref, br_ref, x1_ref, h2_ref, rt_ref, *, n_groups, per_group):
    o = jnp.dot(mg_ref[...], wo_ref[...], preferred_element_type=F32)
    x1 = x_ref[...] + _bc(g1_ref) * o
    x1_ref[...] = x1
    h2 = _rms(x1, ng_ref[...]) * (1.0 + _bc(sc_ref)) + _bc(sh_ref)
    h2_ref[...] = h2
    lg = jnp.dot(h2, wr_ref[...], preferred_element_type=F32, precision=lax.Precision.HIGHEST) + br_ref[...]
    col = lax.broadcasted_iota(I32, lg.shape, 1).astype(F32)
    big = float(LANE)
    gl = jnp.where(col < n_groups, lg, NEG_INF)
    gmax = jnp.max(gl, axis=-1, keepdims=True)
    g_val = 1.0 / jnp.sum(jnp.exp(gl - gmax), axis=-1, keepdims=True)
    g_idx = jnp.min(jnp.where(gl == gmax, col, big), axis=-1, keepdims=True)
    lo = n_groups + g_idx * per_group
    el = jnp.where((col >= lo) & (col < lo + per_group), lg, NEG_INF)
    m1 = jnp.max(el, axis=-1, keepdims=True)
    i1 = jnp.min(jnp.where(el == m1, col, big), axis=-1, keepdims=True)
    el2 = jnp.where(col == i1, NEG_INF, el)
    m2 = jnp.max(el2, axis=-1, keepdims=True)
    i2 = jnp.min(jnp.where(el2 == m2, col, big), axis=-1, keepdims=True)
    r = jnp.exp(m2 - m1)
    w0 = g_val / (1.0 + r)
    w1 = g_val * r / (1.0 + r)
    rt = jnp.where(col == ROUTE_ID0, i1 - n_groups, 0.0)
    rt = jnp.where(col == ROUTE_ID1, i2 - n_groups, rt)
    rt = jnp.where(col == ROUTE_W0, w0, rt)
    rt = jnp.where(col == ROUTE_W1, w1, rt)
    rt_ref[...] = rt


def _wo(merged, x, w_o, gate1, norm_g, shift2, scale2, w_r, b_r, seq_len, n_groups, per_group):
    m, d = x.shape
    tm = _tile(m, 256, BF16_ROWS) if seq_len >= 256 else m
    g1, g1_spec = _rowvec(gate1, seq_len, tm)
    sh, sh_spec = _rowvec(shift2, seq_len, tm)
    sc, sc_spec = _rowvec(scale2, seq_len, tm)
    row = pl.BlockSpec((tm, d), lambda i: (i, 0))
    const = lambda shape: pl.BlockSpec(shape, lambda i: (0, 0))
    return pl.pallas_call(
        functools.partial(_wo_kernel, n_groups=n_groups, per_group=per_group),
        grid=(m // tm,),
        in_specs=[row, row, const((d, d)), g1_spec, const((1, d)), sh_spec, sc_spec, const((d, LANE)), const((1, LANE))],
        out_specs=[row, row, pl.BlockSpec((tm, LANE), lambda i: (i, 0))],
        out_shape=[
            jax.ShapeDtypeStruct((m, d), F32),
            jax.ShapeDtypeStruct((m, d), F32),
            jax.ShapeDtypeStruct((m, LANE), F32),
        ],
        compiler_params=_params("parallel"),
        name="wo_norm_router",
    )(merged, x, w_o, g1, norm_g.reshape(1, d), sh, sc, w_r, b_r)


def _row_copy(src_hbm, row, dst, slot, r, sem):
    return pltpu.make_async_copy(src_hbm.at[pl.ds(row, 1)], dst.at[slot, pl.ds(r, 1)], sem.at[slot])


def _gather_start(idx_ref, src_hbm, dst, slot, sem, n_rows):
    def body(r, c):
        _row_copy(src_hbm, idx_ref[0, 0, r], dst, slot, r, sem).start()
        return c

    lax.fori_loop(0, n_rows, body, 0)


def _gather_wait(src_hbm, dst, slot, sem, n_rows):
    def body(r, c):
        _row_copy(src_hbm, 0, dst, slot, r, sem).wait()
        return c

    lax.fori_loop(0, n_rows, body, 0)


def _expert_kernel(te_ref, tv_ref, tok_ref, tokn_ref, h2_hbm, w1_ref, w3_ref, w2_ref, o_ref, xbuf, w1b, w3b, w2b, sem):
    i = pl.program_id(0)
    n = pl.num_programs(0)
    tm = xbuf.shape[1]
    slot = lax.rem(i, 2)
    nxt = jnp.minimum(i + 1, n - 1)
    valid = tv_ref[i] == 1

    @pl.when(jnp.logical_and(i == 0, valid))
    def _():
        _gather_start(tok_ref, h2_hbm, xbuf, 0, sem, tm)

    @pl.when(jnp.logical_and(i + 1 < n, tv_ref[nxt] == 1))
    def _():
        _gather_start(tokn_ref, h2_hbm, xbuf, 1 - slot, sem, tm)

    @pl.when(valid)
    def _():
        _gather_wait(h2_hbm, xbuf, slot, sem, tm)

        @pl.when(jnp.logical_or(i == 0, te_ref[i] != te_ref[jnp.maximum(i - 1, 0)]))
        def _():
            w1b[...] = w1_ref[0].astype(BF16)
            w3b[...] = w3_ref[0].astype(BF16)
            w2b[...] = w2_ref[0].astype(BF16)

        x = xbuf[slot].astype(BF16)
        a = jnp.dot(x, w1b[...], preferred_element_type=F32)
        b = jnp.dot(x, w3b[...], preferred_element_type=F32)
        hm = (jax.nn.silu(a) * b).astype(BF16)
        o_ref[...] = jnp.dot(hm, w2b[...], preferred_element_type=F32)

    @pl.when(jnp.logical_not(valid))
    def _():
        o_ref[...] = jnp.zeros(o_ref.shape, F32)


def _experts(h2_all, tile_expert, tile_valid, tok_sorted, w1, w3, w2, tm):
    n_tiles = tile_expert.shape[0]
    _, d, de = w1.shape
    tok3 = tok_sorted.reshape(n_tiles, 1, tm)
    grid_spec = pltpu.PrefetchScalarGridSpec(
        num_scalar_prefetch=2,
        grid=(n_tiles,),
        in_specs=[
            pl.BlockSpec((1, 1, tm), lambda i, te, tv: (i, 0, 0), memory_space=pltpu.SMEM),
            pl.BlockSpec((1, 1, tm), lambda i, te, tv: (jnp.minimum(i + 1, n_tiles - 1), 0, 0), memory_space=pltpu.SMEM),
            pl.BlockSpec(memory_space=pl.ANY),
            pl.BlockSpec((1, d, de), lambda i, te, tv: (te[i], 0, 0)),
            pl.BlockSpec((1, d, de), lambda i, te, tv: (te[i], 0, 0)),
            pl.BlockSpec((1, de, d), lambda i, te, tv: (te[i], 0, 0)),
        ],
        out_specs=pl.BlockSpec((tm, d), lambda i, te, tv: (i, 0)),
        scratch_shapes=[
            pltpu.VMEM((2, tm, d), F32),
            pltpu.VMEM((d, de), BF16),
            pltpu.VMEM((d, de), BF16),
            pltpu.VMEM((de, d), BF16),
            pltpu.SemaphoreType.DMA((2,)),
        ],
    )
    return pl.pallas_call(
        _expert_kernel,
        grid_spec=grid_spec,
        out_shape=jax.ShapeDtypeStruct((n_tiles * tm, d), F32),
        compiler_params=_params("arbitrary"),
        name="experts",
    )(tile_expert, tile_valid, tok3, tok3, h2_all, w1, w3, w2)


def _combine_kernel(d_ref, dn_ref, ys_hbm, x1_ref, rt_ref, g2_ref, ng_ref, y_ref, ybuf, sem):
    i = pl.program_id(0)
    n = pl.num_programs(0)
    tm = x1_ref.shape[0]
    slot = lax.rem(i, 2)

    @pl.when(i == 0)
    def _():
        _gather_start(d_ref, ys_hbm, ybuf, 0, sem, 2 * tm)

    @pl.when(i + 1 < n)
    def _():
        _gather_start(dn_ref, ys_hbm, ybuf, 1 - slot, sem, 2 * tm)

    _gather_wait(ys_hbm, ybuf, slot, sem, 2 * tm)
    rt = rt_ref[...]
    moe = rt[:, ROUTE_W0:ROUTE_W0 + 1] * ybuf[slot, 0:tm] + rt[:, ROUTE_W1:ROUTE_W1 + 1] * ybuf[slot, tm:2 * tm]
    x2 = x1_ref[...] + _bc(g2_ref) * moe
    y_ref[...] = _rms(x2, ng_ref[...])


def _combine(ys, dest, x1, route, gate2, normf_g, seq_len):
    m, d = x1.shape
    tm = _tile(seq_len, 256, 8) if seq_len >= 256 else _tile(m, 256, seq_len)
    nb = m // tm
    d3 = dest.reshape(nb, tm, 2).transpose(0, 2, 1).reshape(nb, 1, 2 * tm)
    g2, g2_spec = _rowvec(gate2, seq_len, tm)
    row = pl.BlockSpec((tm, d), lambda i: (i, 0))
    return pl.pallas_call(
        _combine_kernel,
        grid=(nb,),
        in_specs=[
            pl.BlockSpec((1, 1, 2 * tm), lambda i: (i, 0, 0), memory_space=pltpu.SMEM),
            pl.BlockSpec((1, 1, 2 * tm), lambda i: (jnp.minimum(i + 1, nb - 1), 0, 0), memory_space=pltpu.SMEM),
            pl.BlockSpec(memory_space=pl.ANY),
            row,
            pl.BlockSpec((tm, LANE), lambda i: (i, 0)),
            g2_spec,
            pl.BlockSpec((1, d), lambda i: (0, 0)),
        ],
        out_specs=row,
        out_shape=jax.ShapeDtypeStruct((m, d), F32),
        scratch_shapes=[pltpu.VMEM((2, 2 * tm, d), F32), pltpu.SemaphoreType.DMA((2,))],
        compiler_params=_params("arbitrary"),
        name="combine_final_norm",
    )(d3, d3, ys, x1, route, g2, normf_g.reshape(1, d))


def _route_plan(route, n_experts, tm):
    t = route.shape[0]
    s = TOP_K * t
    e_flat = route[:, ROUTE_ID0:ROUTE_ID1 + 1].astype(I32).reshape(s)
    onehot = (e_flat[:, None] == jnp.arange(n_experts, dtype=I32)[None, :]).astype(I32)
    csum = jnp.cumsum(onehot, axis=0)
    rank = jnp.take_along_axis(csum, e_flat[:, None], axis=1)[:, 0] - 1
    counts = csum[-1]
    padded = (counts + tm - 1) // tm * tm
    ends = jnp.cumsum(padded)
    dest = (ends - padded)[e_flat] + rank
    n_tiles = pl.cdiv(s, tm) + n_experts
    tile_start = jnp.arange(n_tiles, dtype=I32) * tm
    tile_valid = (tile_start < ends[-1]).astype(I32)
    te = jnp.minimum(jnp.searchsorted(ends, tile_start, side="right").astype(I32), n_experts - 1)
    last_used = jnp.maximum(ends[-1] // tm - 1, 0)
    tile_expert = jnp.where(tile_valid == 1, te, te[last_used])
    tok_sorted = jnp.zeros((n_tiles * tm,), I32).at[dest].set(jnp.arange(s, dtype=I32) // TOP_K)
    return dest.reshape(t, TOP_K), tok_sorted, tile_expert, tile_valid


def kernel(x_prompt, x_sample, cache_k, cache_v, cache_logf, state_conv, c_prompt, c_sample, w_ada, b_ada, norm1_g, norm2_g, normf_g, w_in, conv_w, conv_b, b_forget, w_br_conv, w_br_attn, w_gate, b_gate, w_o, w_rg, b_rg, w_re, b_re, w1, w3, w2):
    depth = w_ada.shape[0]
    assert depth == 1, "single-layer step"
    l = 0
    bp, sp, d = x_prompt.shape
    bs, ss, _ = x_sample.shape
    _, _, past, n_heads, dh = cache_k.shape
    d_conv = state_conv.shape[-1]
    da = n_heads * dh
    n_groups = w_rg.shape[-1]
    n_experts = w_re.shape[-1]
    per_group = n_experts // n_groups
    assert n_groups + n_experts <= LANE
    scale = dh ** -0.5

    w_in_bf = w_in[l].astype(BF16)
    hp = _round_up(n_heads, BF16_ROWS)
    wf_t = jnp.zeros((hp, d), BF16).at[:n_heads].set(w_in[l][:, 3 * d_conv + 3 * da:].T.astype(BF16))
    b_f = jnp.zeros((hp, 1), F32).at[:n_heads, 0].set(b_forget[l])
    w_brc_bf = w_br_conv[l].astype(BF16)
    w_bra_bf = w_br_attn[l].astype(BF16)
    w_gate_bf = w_gate[l].astype(BF16)
    b_gate2 = b_gate[l].reshape(1, 2 * d)
    w_o_bf = w_o[l].astype(BF16)
    w_r = jnp.zeros((d, LANE), F32).at[:, :n_groups].set(w_rg[l]).at[:, n_groups:n_groups + n_experts].set(w_re[l])
    b_r = jnp.zeros((1, LANE), F32).at[0, :n_groups].set(b_rg[l]).at[0, n_groups:n_groups + n_experts].set(b_re[l])

    mod = _ada(jnp.concatenate([c_prompt, c_sample], axis=0), w_ada[l], b_ada[l])
    mods = {"p": jnp.split(mod[:bp], 6, axis=-1), "s": jnp.split(mod[bp:], 6, axis=-1)}

    def mixer_inputs(x, mod6, seq_len, prev):
        shift1, scale1 = mod6[0], mod6[1]
        h = _norm(x, norm1_g[l], shift1, scale1, seq_len)
        y_conv, tail = _conv_branch(h, w_in_bf, conv_w[l], conv_b[l], prev, seq_len, d_conv)
        q = _proj(h, w_in_bf, 3 * d_conv, da, BF16, scale)
        k = _proj(h, w_in_bf, 3 * d_conv + da, da, F32)
        v = _proj(h, w_in_bf, 3 * d_conv + 2 * da, da, F32)
        lf_t = _forget(h, wf_t, b_f)[:n_heads]
        return h, y_conv, tail, q, k, v, lf_t

    xp = x_prompt.reshape(bp * sp, d)
    xs = x_sample.reshape(bs * ss, d)
    hp_, ycp, tailp, qp, kp, vp, lftp = mixer_inputs(xp, mods["p"], sp, None)
    hs_, ycs, tails, qs, ks, vs, lfts = mixer_inputs(xs, mods["s"], ss, state_conv[l])

    lf_p = lftp.reshape(n_heads, bp, sp).transpose(1, 0, 2).reshape(bp * n_heads, sp)
    f_p = _cumsum_lanes(lf_p).reshape(bp * n_heads, 1, sp)
    yap = _attn_prompt(qp, kp, vp, f_p, bp, sp, n_heads, dh)

    lf_new = lfts.reshape(n_heads, bs, ss).transpose(1, 0, 2)
    lf_past = cache_logf[l].astype(F32).transpose(0, 2, 1)
    tot = _round_up(past + LANE, CUMSUM_CHUNK)
    lf_all = jnp.concatenate([lf_past, lf_new, jnp.zeros((bs, n_heads, tot - past - ss), F32)], axis=-1)
    f_s = _cumsum_lanes(lf_all.reshape(bs * n_heads, tot)).reshape(bs, n_heads, tot)
    yas = _attn_sample(qs, cache_k[l].reshape(bs, past, da), cache_v[l].reshape(bs, past, da), ks, vs, f_s, bs, ss, past, n_heads, dh)

    def mixer_out(x, mod6, seq_len, h, yc, ya):
        gate1, shift2, scale2 = mod6[2], mod6[3], mod6[4]
        merged = _merge(yc, ya, h, w_brc_bf, w_bra_bf, w_gate_bf, b_gate2)
        return _wo(merged, x, w_o_bf, gate1, norm2_g[l], shift2, scale2, w_r, b_r, seq_len, n_groups, per_group)

    x1p, h2p, rtp = mixer_out(xp, mods["p"], sp, hp_, ycp, yap)
    x1s, h2s, rts = mixer_out(xs, mods["s"], ss, hs_, ycs, yas)

    h2_all = jnp.concatenate([h2p, h2s], axis=0)
    route_all = jnp.concatenate([rtp, rts], axis=0)
    tm_e = 256 if TOP_K * h2_all.shape[0] >= 8192 else 64
    dest, tok_sorted, tile_expert, tile_valid = _route_plan(route_all, n_experts, tm_e)
    ys = _experts(h2_all, tile_expert, tile_valid, tok_sorted, w1[l], w3[l], w2[l], tm_e)

    mp = bp * sp
    y_p = _combine(ys, dest[:mp], x1p, rtp, mods["p"][5], normf_g, sp)
    y_s = _combine(ys, dest[mp:], x1s, rts, mods["s"][5], normf_g, ss)

    def cache_outs(k, v, lf_t, tail, n_seq, seq_len):
        return (
            k.reshape(1, n_seq, seq_len, n_heads, dh),
            v.reshape(1, n_seq, seq_len, n_heads, dh),
            lf_t.T.reshape(1, n_seq, seq_len, n_heads),
            tail[:, CONV_TAIL_ROWS - 2:, :].reshape(1, n_seq, 2, d_conv),
        )

    return (y_p.reshape(bp, sp, d), y_s.reshape(bs, ss, d)) + cache_outs(kp, vp, lftp, tailp, bp, sp) + cache_outs(ks, vs, lfts, tails, bs, ss)
```

```python
import functools

import jax
import jax.numpy as jnp
from jax import lax
from jax.experimental import pallas as pl
from jax.experimental.pallas import tpu as pltpu

F32 = jnp.float32
BF16 = jnp.bfloat16
I32 = jnp.int32

EPS = 1e-6
NEG_INF = -1e30
TOP_K = 2
LANE = 128
BF16_ROWS = 16
VMEM_LIMIT_BYTES = 56 * 1024 * 1024
NT_DIMS = (((1,), (1,)), ((), ()))


def _tile(n, pref, align):
    if n <= pref:
        return n
    t = (pref // align) * align
    while t >= align:
        if n % t == 0:
            return t
        t -= align
    return n


def _round_up(n, m):
    return (n + m - 1) // m * m


def _params(*sem):
    return pltpu.CompilerParams(dimension_semantics=sem, vmem_limit_bytes=VMEM_LIMIT_BYTES)


def _rowvec(vec, seq_len, tm):
    n_seq, d = vec.shape
    if tm <= seq_len:
        per = seq_len // tm
        return vec.reshape(n_seq, 1, d), pl.BlockSpec((1, 1, d), lambda i, *_: (i // per, 0, 0))
    return jnp.repeat(vec, seq_len, axis=0), pl.BlockSpec((tm, d), lambda i, *_: (i, 0))


def _bc(ref):
    v = ref[...]
    return v.reshape(v.shape[-2], v.shape[-1])


def _rms(x, g):
    return x * lax.rsqrt(jnp.mean(x * x, axis=-1, keepdims=True) + EPS) * g


def _ada_kernel(c_ref, w_ref, b_ref, o_ref):
    o_ref[...] = jnp.dot(c_ref[...].astype(BF16), w_ref[...].astype(BF16), preferred_element_type=F32) + b_ref[...]


def _ada(c_all, w, b):
    r, d = c_all.shape
    n = w.shape[1]
    tn = _tile(n, 1024, LANE)
    return pl.pallas_call(
        _ada_kernel,
        grid=(n // tn,),
        in_specs=[
            pl.BlockSpec((r, d), lambda j: (0, 0)),
            pl.BlockSpec((d, tn), lambda j: (0, j)),
            pl.BlockSpec((1, tn), lambda j: (0, j)),
        ],
        out_specs=pl.BlockSpec((r, tn), lambda j: (0, j)),
        out_shape=jax.ShapeDtypeStruct((r, n), F32),
        compiler_params=_params("parallel"),
        name="ada",
    )(c_all, w, b.reshape(1, n))


def _norm_kernel(x_ref, g_ref, sh_ref, sc_ref, h_ref):
    h = _rms(x_ref[...], g_ref[...]) * (1.0 + _bc(sc_ref)) + _bc(sh_ref)
    h_ref[...] = h.astype(h_ref.dtype)


def _norm(x, g, shift, scale, seq_len):
    m, d = x.shape
    tm = _tile(seq_len, 512, 8)
    sh, sh_spec = _rowvec(shift, seq_len, tm)
    sc, sc_spec = _rowvec(scale, seq_len, tm)
    return pl.pallas_call(
        _norm_kernel,
        grid=(m // tm,),
        in_specs=[pl.BlockSpec((tm, d), lambda i: (i, 0)), pl.BlockSpec((1, d), lambda i: (0, 0)), sh_spec, sc_spec],
        out_specs=pl.BlockSpec((tm, d), lambda i: (i, 0)),
        out_shape=jax.ShapeDtypeStruct((m, d), BF16),
        compiler_params=_params("parallel"),
        name="norm",
    )(x, g.reshape(1, d), sh, sc)


def _proj_kernel(a_ref, w_ref, o_ref, *, scale):
    acc = jnp.dot(a_ref[...], w_ref[...], preferred_element_type=F32)
    if scale != 1.0:
        acc = acc * scale
    o_ref[...] = acc.astype(o_ref.dtype)


def _proj(a, w, col_off, ncols, out_dtype, scale=1.0):
    m, k = a.shape
    tm = _tile(m, 1024, BF16_ROWS)
    tn = _tile(ncols, 512, LANE)
    while tn > LANE and (col_off % tn or ncols % tn):
        tn -= LANE
    assert col_off % tn == 0 and ncols % tn == 0
    off = col_off // tn
    return pl.pallas_call(
        functools.partial(_proj_kernel, scale=scale),
        grid=(m // tm, ncols // tn),
        in_specs=[pl.BlockSpec((tm, k), lambda i, j: (i, 0)), pl.BlockSpec((k, tn), lambda i, j: (0, off + j))],
        out_specs=pl.BlockSpec((tm, tn), lambda i, j: (i, j)),
        out_shape=jax.ShapeDtypeStruct((m, ncols), out_dtype),
        compiler_params=_params("parallel", "parallel"),
        name="proj",
    )(a, w)


CONV_TAIL_ROWS = 8


def _conv_kernel(*refs, seq_len, has_prev):
    if has_prev:
        h_ref, wb_ref, wc_ref, wv_ref, cw_ref, cb_ref, p1_ref, p2_ref, y_ref, tail_ref = refs
    else:
        h_ref, wb_ref, wc_ref, wv_ref, cw_ref, cb_ref, y_ref, tail_ref = refs
    h = h_ref[...]
    bg = jnp.dot(h, wb_ref[...], preferred_element_type=F32)
    cg = jnp.dot(h, wc_ref[...], preferred_element_type=F32)
    vc = jnp.dot(h, wv_ref[...], preferred_element_type=F32)
    u = cg * vc
    tm = u.shape[0]
    t = lax.broadcasted_iota(I32, (tm, 1), 0)
    if tm > seq_len:
        t = jnp.bitwise_and(t, seq_len - 1) if seq_len & (seq_len - 1) == 0 else lax.rem(t, seq_len)
    u1 = jnp.where(t >= 1, pltpu.roll(u, 1, 0), p1_ref[...] if has_prev else 0.0)
    u2 = jnp.where(t >= 2, pltpu.roll(u, 2, 0), p2_ref[...] if has_prev else 0.0)
    cw = cw_ref[...]
    conv = cb_ref[...] + ((cw[0:1] * u2 + cw[1:2] * u1) + cw[2:3] * u)
    y_ref[...] = (bg * conv).astype(y_ref.dtype)
    for s in range(tm // seq_len):
        tail_ref[s] = u[(s + 1) * seq_len - CONV_TAIL_ROWS:(s + 1) * seq_len, :]


def _conv_branch(h, w_in_bf, conv_w, conv_b, prev, seq_len, d_conv):
    m, d = h.shape
    assert conv_w.shape[0] == 3 and seq_len >= CONV_TAIL_ROWS
    tm = seq_len if seq_len >= 512 else m
    assert m % tm == 0 and tm % seq_len == 0
    tc = _tile(d_conv, 256, LANE)
    nc = d_conv // tc
    n_seq = m // seq_len
    seq_per_tile = tm // seq_len
    in_specs = [
        pl.BlockSpec((tm, d), lambda i, j: (i, 0)),
        pl.BlockSpec((d, tc), lambda i, j: (0, j)),
        pl.BlockSpec((d, tc), lambda i, j: (0, nc + j)),
        pl.BlockSpec((d, tc), lambda i, j: (0, 2 * nc + j)),
        pl.BlockSpec((3, tc), lambda i, j: (0, j)),
        pl.BlockSpec((1, tc), lambda i, j: (0, j)),
    ]
    args = [h, w_in_bf, w_in_bf, w_in_bf, conv_w, conv_b.reshape(1, d_conv)]
    if prev is not None:
        zeros = jnp.zeros((n_seq, seq_len, d_conv), F32)
        p1 = zeros.at[:, 0].set(prev[:, 1]).reshape(m, d_conv)
        p2 = zeros.at[:, 0].set(prev[:, 0]).at[:, 1].set(prev[:, 1]).reshape(m, d_conv)
        in_specs += [pl.BlockSpec((tm, tc), lambda i, j: (i, j))] * 2
        args += [p1, p2]
    return pl.pallas_call(
        functools.partial(_conv_kernel, seq_len=seq_len, has_prev=prev is not None),
        grid=(m // tm, nc),
        in_specs=in_specs,
        out_specs=[
            pl.BlockSpec((tm, tc), lambda i, j: (i, j)),
            pl.BlockSpec((seq_per_tile, CONV_TAIL_ROWS, tc), lambda i, j: (i, 0, j)),
        ],
        out_shape=[
            jax.ShapeDtypeStruct((m, d_conv), BF16),
            jax.ShapeDtypeStruct((n_seq, CONV_TAIL_ROWS, d_conv), F32),
        ],
        compiler_params=_params("parallel", "parallel"),
        name="conv_branch",
    )(*args)


def _forget_kernel(wf_ref, h_ref, b_ref, lf_ref):
    fl = lax.dot_general(wf_ref[...], h_ref[...], NT_DIMS, preferred_element_type=F32)
    lf_ref[...] = jax.nn.log_sigmoid(fl + b_ref[...])


def _forget(h, wf_t, b_f):
    m, d = h.shape
    hp = wf_t.shape[0]
    tm = _tile(m, 1024, LANE)
    return pl.pallas_call(
        _forget_kernel,
        grid=(m // tm,),
        in_specs=[
            pl.BlockSpec((hp, d), lambda i: (0, 0)),
            pl.BlockSpec((tm, d), lambda i: (i, 0)),
            pl.BlockSpec((hp, 1), lambda i: (0, 0)),
        ],
        out_specs=pl.BlockSpec((hp, tm), lambda i: (0, i)),
        out_shape=jax.ShapeDtypeStruct((hp, m), F32),
        compiler_params=_params("parallel"),
        name="forget",
    )(wf_t, h, b_f)


CUMSUM_CHUNK = LANE


def _cumsum_kernel(x_ref, o_ref):
    r, n = x_ref.shape
    c = CUMSUM_CHUNK
    tri = (lax.broadcasted_iota(I32, (c, c), 0) <= lax.broadcasted_iota(I32, (c, c), 1)).astype(BF16)
    carry = jnp.zeros((r, 1), F32)
    for j in range(n // c):
        x = x_ref[:, j * c:(j + 1) * c]
        cs = carry
        for _ in range(3):
            piece = x.astype(BF16)
            cs = cs + jnp.dot(piece, tri, preferred_element_type=F32)
            x = x - piece.astype(F32)
        o_ref[:, j * c:(j + 1) * c] = cs
        carry = cs[:, c - 1:c]


def _cumsum_lanes(x):
    r, n = x.shape
    tr = _tile(r, 128, 8)
    return pl.pallas_call(
        _cumsum_kernel,
        grid=(r // tr,),
        in_specs=[pl.BlockSpec((tr, n), lambda i: (i, 0))],
        out_specs=pl.BlockSpec((tr, n), lambda i: (i, 0)),
        out_shape=jax.ShapeDtypeStruct((r, n), F32),
        compiler_params=_params("parallel"),
        name="cumsum",
    )(x)


def _attn_prompt_kernel(q_ref, k_ref, v_ref, f_ref, o_ref, *, tq):
    seq_len = q_ref.shape[0]
    kb = k_ref[...].astype(BF16)
    vb = v_ref[...].astype(BF16)
    f = f_ref[...].reshape(1, seq_len)
    causal = lax.broadcasted_iota(I32, (tq, tq), 1) <= lax.broadcasted_iota(I32, (tq, tq), 0)
    for qi in range(seq_len // tq):
        lo = qi * tq
        q = q_ref[lo:lo + tq, :]
        sd = lax.dot_general(q, kb[lo:lo + tq], NT_DIMS, preferred_element_type=F32) - f[:, lo:lo + tq]
        sd = jnp.where(causal, sd, NEG_INF)
        m = jnp.max(sd, axis=-1, keepdims=True)
        if qi > 0:
            sp = lax.dot_general(q, kb[:lo], NT_DIMS, preferred_element_type=F32) - f[:, :lo]
            m = jnp.maximum(m, jnp.max(sp, axis=-1, keepdims=True))
            pp = jnp.exp(sp - m)
            l = jnp.sum(pp, axis=-1, keepdims=True)
            o = jnp.dot(pp.astype(BF16), vb[:lo], preferred_element_type=F32)
        pd = jnp.exp(sd - m)
        ld = jnp.sum(pd, axis=-1, keepdims=True)
        od = jnp.dot(pd.astype(BF16), vb[lo:lo + tq], preferred_element_type=F32)
        if qi > 0:
            l, o = l + ld, o + od
        else:
            l, o = ld, od
        o_ref[lo:lo + tq, :] = (o / l).astype(o_ref.dtype)


def _attn_prompt(q, k, v, f, n_seq, seq_len, n_heads, dh):
    m = n_seq * seq_len
    tq = _tile(seq_len, 256, LANE)
    blk = pl.BlockSpec((seq_len, dh), lambda b, h: (b, h))
    return pl.pallas_call(
        functools.partial(_attn_prompt_kernel, tq=tq),
        grid=(n_seq, n_heads),
        in_specs=[blk, blk, blk, pl.BlockSpec((1, 1, seq_len), lambda b, h: (b * n_heads + h, 0, 0))],
        out_specs=blk,
        out_shape=jax.ShapeDtypeStruct((m, n_heads * dh), BF16),
        compiler_params=_params("parallel", "parallel"),
        name="attn_prompt",
    )(q, k, v, f)


def _attn_sample_kernel(q_ref, kc_ref, vc_ref, fp_ref, kn_ref, vn_ref, fn_ref, o_ref, m_sc, l_sc, acc_sc, *, n_heads, dh):
    kt = pl.program_id(1)
    lq = q_ref.shape[0]

    @pl.when(kt == 0)
    def _():
        m_sc[...] = jnp.full(m_sc.shape, NEG_INF, F32)
        l_sc[...] = jnp.zeros(l_sc.shape, F32)
        acc_sc[...] = jnp.zeros(acc_sc.shape, F32)

    q = q_ref[...]

    def update(k_heads, v_heads, fk, mask):
        s_heads = []
        for h in range(n_heads):
            s = lax.dot_general(q[:, h * dh:(h + 1) * dh], k_heads[h], NT_DIMS, preferred_element_type=F32)
            s = s - fk[h:h + 1, :]
            if mask is not None:
                s = jnp.where(mask, s, NEG_INF)
            s_heads.append(s)
        s = jnp.concatenate(s_heads, axis=0)
        m_prev = m_sc[...]
        m_new = jnp.maximum(m_prev, jnp.max(s, axis=-1, keepdims=True))
        alpha = jnp.exp(m_prev - m_new)
        p = jnp.exp(s - m_new)
        l_sc[...] = alpha * l_sc[...] + jnp.sum(p, axis=-1, keepdims=True)
        pb = p.astype(BF16)
        o_heads = [jnp.dot(pb[h * lq:(h + 1) * lq], v_heads[h], preferred_element_type=F32) for h in range(n_heads)]
        acc_sc[...] = alpha * acc_sc[...] + jnp.concatenate(o_heads, axis=0)
        m_sc[...] = m_new

    def cached_heads(ref):
        x = pltpu.einshape("shd->hsd", ref[0, 0]).astype(BF16)
        return [x[h] for h in range(n_heads)]

    update(cached_heads(kc_ref), cached_heads(vc_ref), fp_ref[0], None)

    @pl.when(kt == pl.num_programs(1) - 1)
    def _():
        pad = jnp.zeros((LANE - lq, n_heads * dh), BF16)
        kn = jnp.concatenate([kn_ref[...].astype(BF16), pad], axis=0)
        vn = jnp.concatenate([vn_ref[...].astype(BF16), pad], axis=0)
        mask = lax.broadcasted_iota(I32, (lq, LANE), 1) <= lax.broadcasted_iota(I32, (lq, LANE), 0)
        heads = lambda x: [x[:, h * dh:(h + 1) * dh] for h in range(n_heads)]
        update(heads(kn), heads(vn), fn_ref[0], mask)
        out = acc_sc[...] / l_sc[...]
        o_ref[...] = jnp.concatenate([out[h * lq:(h + 1) * lq] for h in range(n_heads)], axis=1).astype(o_ref.dtype)


def _attn_sample(q, cache_k, cache_v, layer, k_new, v_new, f_all, n_seq, lq, past, n_heads, dh):
    da = n_heads * dh
    tk = _tile(past, 512, LANE)
    assert lq <= LANE and lq % BF16_ROWS == 0 and past % LANE == 0
    qblk = pl.BlockSpec((lq, da), lambda b, t: (b, 0))
    cblk = pl.BlockSpec((1, 1, tk, n_heads, dh), lambda b, t: (layer, b, t, 0, 0))
    return pl.pallas_call(
        functools.partial(_attn_sample_kernel, n_heads=n_heads, dh=dh),
        grid=(n_seq, past // tk),
        in_specs=[
            qblk,
            cblk,
            cblk,
            pl.BlockSpec((1, n_heads, tk), lambda b, t: (b, 0, t)),
            qblk,
            qblk,
            pl.BlockSpec((1, n_heads, LANE), lambda b, t: (b, 0, past // LANE)),
        ],
        out_specs=qblk,
        out_shape=jax.ShapeDtypeStruct((n_seq * lq, da), BF16),
        scratch_shapes=[
            pltpu.VMEM((n_heads * lq, 1), F32),
            pltpu.VMEM((n_heads * lq, 1), F32),
            pltpu.VMEM((n_heads * lq, dh), F32),
        ],
        compiler_params=_params("parallel", "arbitrary"),
        name="attn_sample",
    )(q, cache_k, cache_v, f_all, k_new, v_new, f_all)


def _merge_kernel(yc_ref, ya_ref, h_ref, wc_ref, wa_ref, wgc_ref, wga_ref, bgc_ref, bga_ref, o_ref):
    h = h_ref[...]
    a = jnp.dot(yc_ref[...], wc_ref[...], preferred_element_type=F32)
    b = jnp.dot(ya_ref[...], wa_ref[...], preferred_element_type=F32)
    gc = jax.nn.sigmoid(jnp.dot(h, wgc_ref[...], preferred_element_type=F32) + bgc_ref[...])
    ga = jax.nn.sigmoid(jnp.dot(h, wga_ref[...], preferred_element_type=F32) + bga_ref[...])
    o_ref[...] = (gc * a + ga * b).astype(o_ref.dtype)


def _merge(yc, ya, h, w_brc, w_bra, w_gate, b_gate):
    m, d = h.shape
    dc, da = yc.shape[1], ya.shape[1]
    tm = _tile(m, 1024, BF16_ROWS)
    tn = _tile(d, 512, LANE)
    nj = d // tn
    return pl.pallas_call(
        _merge_kernel,
        grid=(m // tm, nj),
        in_specs=[
            pl.BlockSpec((tm, dc), lambda i, j: (i, 0)),
            pl.BlockSpec((tm, da), lambda i, j: (i, 0)),
            pl.BlockSpec((tm, d), lambda i, j: (i, 0)),
            pl.BlockSpec((dc, tn), lambda i, j: (0, j)),
            pl.BlockSpec((da, tn), lambda i, j: (0, j)),
            pl.BlockSpec((d, tn), lambda i, j: (0, j)),
            pl.BlockSpec((d, tn), lambda i, j: (0, nj + j)),
            pl.BlockSpec((1, tn), lambda i, j: (0, j)),
            pl.BlockSpec((1, tn), lambda i, j: (0, nj + j)),
        ],
        out_specs=pl.BlockSpec((tm, tn), lambda i, j: (i, j)),
        out_shape=jax.ShapeDtypeStruct((m, d), BF16),
        compiler_params=_params("parallel", "parallel"),
        name="merge",
    )(yc, ya, h, w_brc, w_bra, w_gate, w_gate, b_gate, b_gate)


ROUTE_ID0, ROUTE_ID1, ROUTE_W0, ROUTE_W1 = 0, 1, 2, 3


def _wo_kernel(mg_ref, x_ref, wo_ref, g1_ref, ng_ref, sh_ref, sc_ref, wr_ref, br_ref, x1_ref, h2_ref, rt_ref, *, n_groups, per_group):
    o = jnp.dot(mg_ref[...], wo_ref[...], preferred_element_type=F32)
    x1 = x_ref[...] + _bc(g1_ref) * o
    x1_ref[...] = x1
    h2 = _rms(x1, ng_ref[...]) * (1.0 + _bc(sc_ref)) + _bc(sh_ref)
    h2_ref[...] = h2
    wr = wr_ref[...]
    hi = h2.astype(BF16)
    lo = (h2 - hi.astype(F32)).astype(BF16)
    r_hi = jnp.dot(hi, wr, preferred_element_type=F32)
    r_lo = jnp.dot(lo, wr[:, :LANE], preferred_element_type=F32)
    lg = r_hi[:, :LANE] + (r_hi[:, LANE:] + r_lo) + br_ref[...]
    col = lax.broadcasted_iota(I32, lg.shape, 1).astype(F32)
    big = float(LANE)
    gl = jnp.where(col < n_groups, lg, NEG_INF)
    gmax = jnp.max(gl, axis=-1, keepdims=True)
    g_val = 1.0 / jnp.sum(jnp.exp(gl - gmax), axis=-1, keepdims=True)
    g_idx = jnp.min(jnp.where(gl == gmax, col, big), axis=-1, keepdims=True)
    lo = n_groups + g_idx * per_group
    el = jnp.where((col >= lo) & (col < lo + per_group), lg, NEG_INF)
    m1 = jnp.max(el, axis=-1, keepdims=True)
    i1 = jnp.min(jnp.where(el == m1, col, big), axis=-1, keepdims=True)
    el2 = jnp.where(col == i1, NEG_INF, el)
    m2 = jnp.max(el2, axis=-1, keepdims=True)
    i2 = jnp.min(jnp.where(el2 == m2, col, big), axis=-1, keepdims=True)
    r = jnp.exp(m2 - m1)
    w0 = g_val / (1.0 + r)
    w1 = g_val * r / (1.0 + r)
    rt = jnp.where(col == ROUTE_ID0, i1 - n_groups, 0.0)
    rt = jnp.where(col == ROUTE_ID1, i2 - n_groups, rt)
    rt = jnp.where(col == ROUTE_W0, w0, rt)
    rt = jnp.where(col == ROUTE_W1, w1, rt)
    rt_ref[...] = rt


def _wo(merged, x, w_o, gate1, norm_g, shift2, scale2, w_r, b_r, seq_len, n_groups, per_group):
    m, d = x.shape
    tm = _tile(m, 256, BF16_ROWS) if seq_len >= 256 else m
    g1, g1_spec = _rowvec(gate1, seq_len, tm)
    sh, sh_spec = _rowvec(shift2, seq_len, tm)
    sc, sc_spec = _rowvec(scale2, seq_len, tm)
    row = pl.BlockSpec((tm, d), lambda i: (i, 0))
    const = lambda shape: pl.BlockSpec(shape, lambda i: (0, 0))
    return pl.pallas_call(
        functools.partial(_wo_kernel, n_groups=n_groups, per_group=per_group),
        grid=(m // tm,),
        in_specs=[row, row, const((d, d)), g1_spec, const((1, d)), sh_spec, sc_spec, const((d, 2 * LANE)), const((1, LANE))],
        out_specs=[row, row, pl.BlockSpec((tm, LANE), lambda i: (i, 0))],
        out_shape=[
            jax.ShapeDtypeStruct((m, d), F32),
            jax.ShapeDtypeStruct((m, d), F32),
            jax.ShapeDtypeStruct((m, LANE), F32),
        ],
        compiler_params=_params("parallel"),
        name="wo_norm_router",
    )(merged, x, w_o, g1, norm_g.reshape(1, d), sh, sc, w_r, b_r)


GATHER_UNROLL = 8


def _gather_start(idx_ref, src_hbm, dst, slot, sem, n_rows):
    def body(r, c):
        pltpu.make_async_copy(src_hbm.at[pl.ds(idx_ref[0, 0, r], 1)], dst.at[slot, pl.ds(r, 1)], sem.at[slot]).start()
        return c

    lax.fori_loop(0, n_rows, body, 0, unroll=GATHER_UNROLL)


def _gather_wait(src_hbm, dst, slot, sem, n_rows):
    pltpu.make_async_copy(src_hbm.at[pl.ds(0, n_rows)], dst.at[slot], sem.at[slot]).wait()


def _expert_kernel(te_ref, tv_ref, tok_ref, tokn_ref, h2_hbm, w1_ref, w3_ref, w2_ref, o_ref, xbuf, w1b, w3b, w2b, sem):
    i = pl.program_id(0)
    n = pl.num_programs(0)
    tm = xbuf.shape[1]
    slot = lax.rem(i, 2)
    nxt = jnp.minimum(i + 1, n - 1)
    valid = tv_ref[i] == 1

    @pl.when(jnp.logical_and(i == 0, valid))
    def _():
        _gather_start(tok_ref, h2_hbm, xbuf, 0, sem, tm)

    @pl.when(jnp.logical_and(i + 1 < n, tv_ref[nxt] == 1))
    def _():
        _gather_start(tokn_ref, h2_hbm, xbuf, 1 - slot, sem, tm)

    @pl.when(valid)
    def _():
        _gather_wait(h2_hbm, xbuf, slot, sem, tm)

        @pl.when(jnp.logical_or(i == 0, te_ref[i] != te_ref[jnp.maximum(i - 1, 0)]))
        def _():
            w1b[...] = w1_ref[0].astype(BF16)
            w3b[...] = w3_ref[0].astype(BF16)
            w2b[...] = w2_ref[0].astype(BF16)

        x = xbuf[slot].astype(BF16)
        a = jnp.dot(x, w1b[...], preferred_element_type=F32)
        b = jnp.dot(x, w3b[...], preferred_element_type=F32)
        hm = (jax.nn.silu(a) * b).astype(BF16)
        o_ref[...] = jnp.dot(hm, w2b[...], preferred_element_type=F32)

    @pl.when(jnp.logical_not(valid))
    def _():
        o_ref[...] = jnp.zeros(o_ref.shape, F32)


def _experts(h2_all, tile_expert, tile_valid, tok_sorted, w1, w3, w2, tm):
    n_tiles = tile_expert.shape[0]
    _, d, de = w1.shape
    tok3 = tok_sorted.reshape(n_tiles, 1, tm)
    grid_spec = pltpu.PrefetchScalarGridSpec(
        num_scalar_prefetch=2,
        grid=(n_tiles,),
        in_specs=[
            pl.BlockSpec((1, 1, tm), lambda i, te, tv: (i, 0, 0), memory_space=pltpu.SMEM),
            pl.BlockSpec((1, 1, tm), lambda i, te, tv: (jnp.minimum(i + 1, n_tiles - 1), 0, 0), memory_space=pltpu.SMEM),
            pl.BlockSpec(memory_space=pl.ANY),
            pl.BlockSpec((1, d, de), lambda i, te, tv: (te[i], 0, 0)),
            pl.BlockSpec((1, d, de), lambda i, te, tv: (te[i], 0, 0)),
            pl.BlockSpec((1, de, d), lambda i, te, tv: (te[i], 0, 0)),
        ],
        out_specs=pl.BlockSpec((tm, d), lambda i, te, tv: (i, 0)),
        scratch_shapes=[
            pltpu.VMEM((2, tm, d), F32),
            pltpu.VMEM((d, de), BF16),
            pltpu.VMEM((d, de), BF16),
            pltpu.VMEM((de, d), BF16),
            pltpu.SemaphoreType.DMA((2,)),
        ],
    )
    return pl.pallas_call(
        _expert_kernel,
        grid_spec=grid_spec,
        out_shape=jax.ShapeDtypeStruct((n_tiles * tm, d), F32),
        compiler_params=_params("arbitrary"),
        name="experts",
    )(tile_expert, tile_valid, tok3, tok3, h2_all, w1, w3, w2)


def _combine_kernel(d_ref, dn_ref, ys_hbm, x1_ref, rt_ref, g2_ref, ng_ref, y_ref, ybuf, sem):
    i = pl.program_id(0)
    n = pl.num_programs(0)
    tm = x1_ref.shape[0]
    slot = lax.rem(i, 2)

    @pl.when(i == 0)
    def _():
        _gather_start(d_ref, ys_hbm, ybuf, 0, sem, 2 * tm)

    @pl.when(i + 1 < n)
    def _():
        _gather_start(dn_ref, ys_hbm, ybuf, 1 - slot, sem, 2 * tm)

    _gather_wait(ys_hbm, ybuf, slot, sem, 2 * tm)
    rt = rt_ref[...]
    moe = rt[:, ROUTE_W0:ROUTE_W0 + 1] * ybuf[slot, 0:tm] + rt[:, ROUTE_W1:ROUTE_W1 + 1] * ybuf[slot, tm:2 * tm]
    x2 = x1_ref[...] + _bc(g2_ref) * moe
    y_ref[...] = _rms(x2, ng_ref[...])


def _combine(ys, dest, x1, route, gate2, normf_g, seq_len):
    m, d = x1.shape
    tm = _tile(seq_len, 256, 8) if seq_len >= 256 else _tile(m, 256, seq_len)
    nb = m // tm
    d3 = dest.reshape(nb, tm, 2).transpose(0, 2, 1).reshape(nb, 1, 2 * tm)
    g2, g2_spec = _rowvec(gate2, seq_len, tm)
    row = pl.BlockSpec((tm, d), lambda i: (i, 0))
    return pl.pallas_call(
        _combine_kernel,
        grid=(nb,),
        in_specs=[
            pl.BlockSpec((1, 1, 2 * tm), lambda i: (i, 0, 0), memory_space=pltpu.SMEM),
            pl.BlockSpec((1, 1, 2 * tm), lambda i: (jnp.minimum(i + 1, nb - 1), 0, 0), memory_space=pltpu.SMEM),
            pl.BlockSpec(memory_space=pl.ANY),
            row,
            pl.BlockSpec((tm, LANE), lambda i: (i, 0)),
            g2_spec,
            pl.BlockSpec((1, d), lambda i: (0, 0)),
        ],
        out_specs=row,
        out_shape=jax.ShapeDtypeStruct((m, d), F32),
        scratch_shapes=[pltpu.VMEM((2, 2 * tm, d), F32), pltpu.SemaphoreType.DMA((2,))],
        compiler_params=_params("arbitrary"),
        name="combine_final_norm",
    )(d3, d3, ys, x1, route, g2, normf_g.reshape(1, d))


def _route_plan(route, n_experts, tm):
    t = route.shape[0]
    s = TOP_K * t
    e_flat = route[:, ROUTE_ID0:ROUTE_ID1 + 1].astype(I32).reshape(s)
    onehot = (e_flat[:, None] == jnp.arange(n_experts, dtype=I32)[None, :]).astype(I32)
    csum = jnp.cumsum(onehot, axis=0)
    rank = jnp.take_along_axis(csum, e_flat[:, None], axis=1)[:, 0] - 1
    counts = csum[-1]
    padded = (counts + tm - 1) // tm * tm
    ends = jnp.cumsum(padded)
    dest = (ends - padded)[e_flat] + rank
    n_tiles = pl.cdiv(s, tm) + n_experts
    tile_start = jnp.arange(n_tiles, dtype=I32) * tm
    tile_valid = (tile_start < ends[-1]).astype(I32)
    te = jnp.minimum(jnp.sum((ends[None, :] <= tile_start[:, None]).astype(I32), axis=1), n_experts - 1)
    last_used = jnp.maximum(ends[-1] // tm - 1, 0)
    tile_expert = jnp.where(tile_valid == 1, te, te[last_used])
    tok_sorted = jnp.zeros((n_tiles * tm,), I32).at[dest].set(jnp.arange(s, dtype=I32) // TOP_K)
    return dest.reshape(t, TOP_K), tok_sorted, tile_expert, tile_valid


def kernel(x_prompt, x_sample, cache_k, cache_v, cache_logf, state_conv, c_prompt, c_sample, w_ada, b_ada, norm1_g, norm2_g, normf_g, w_in, conv_w, conv_b, b_forget, w_br_conv, w_br_attn, w_gate, b_gate, w_o, w_rg, b_rg, w_re, b_re, w1, w3, w2):
    depth = w_ada.shape[0]
    assert depth == 1, "single-layer step"
    l = 0
    bp, sp, d = x_prompt.shape
    bs, ss, _ = x_sample.shape
    _, _, past, n_heads, dh = cache_k.shape
    d_conv = state_conv.shape[-1]
    da = n_heads * dh
    n_groups = w_rg.shape[-1]
    n_experts = w_re.shape[-1]
    per_group = n_experts // n_groups
    assert n_groups + n_experts <= LANE
    scale = dh ** -0.5

    w_in_bf = w_in[l].astype(BF16)
    hp = _round_up(n_heads, BF16_ROWS)
    wf_t = jnp.zeros((hp, d), BF16).at[:n_heads].set(w_in[l][:, 3 * d_conv + 3 * da:].T.astype(BF16))
    b_f = jnp.zeros((hp, 1), F32).at[:n_heads, 0].set(b_forget[l])
    w_brc_bf = w_br_conv[l].astype(BF16)
    w_bra_bf = w_br_attn[l].astype(BF16)
    w_gate_bf = w_gate[l].astype(BF16)
    b_gate2 = b_gate[l].reshape(1, 2 * d)
    w_o_bf = w_o[l].astype(BF16)
    w_r32 = jnp.pad(jnp.concatenate([w_rg[l], w_re[l]], axis=1), ((0, 0), (0, LANE - n_groups - n_experts)))
    w_r_hi = w_r32.astype(BF16)
    w_r = jnp.concatenate([w_r_hi, (w_r32 - w_r_hi.astype(F32)).astype(BF16)], axis=1)
    b_r = jnp.zeros((1, LANE), F32).at[0, :n_groups].set(b_rg[l]).at[0, n_groups:n_groups + n_experts].set(b_re[l])

    mod = _ada(jnp.concatenate([c_prompt, c_sample], axis=0), w_ada[l], b_ada[l])
    mods = {"p": jnp.split(mod[:bp], 6, axis=-1), "s": jnp.split(mod[bp:], 6, axis=-1)}

    def mixer_inputs(x, mod6, seq_len, prev):
        shift1, scale1 = mod6[0], mod6[1]
        h = _norm(x, norm1_g[l], shift1, scale1, seq_len)
        y_conv, tail = _conv_branch(h, w_in_bf, conv_w[l], conv_b[l], prev, seq_len, d_conv)
        q = _proj(h, w_in_bf, 3 * d_conv, da, BF16, scale)
        k = _proj(h, w_in_bf, 3 * d_conv + da, da, F32)
        v = _proj(h, w_in_bf, 3 * d_conv + 2 * da, da, F32)
        lf_t = _forget(h, wf_t, b_f)[:n_heads]
        return h, y_conv, tail, q, k, v, lf_t

    xp = x_prompt.reshape(bp * sp, d)
    xs = x_sample.reshape(bs * ss, d)
    hp_, ycp, tailp, qp, kp, vp, lftp = mixer_inputs(xp, mods["p"], sp, None)
    hs_, ycs, tails, qs, ks, vs, lfts = mixer_inputs(xs, mods["s"], ss, state_conv[l])

    lf_p = lftp.reshape(n_heads, bp, sp).transpose(1, 0, 2).reshape(bp * n_heads, sp)
    f_p = _cumsum_lanes(lf_p).reshape(bp * n_heads, 1, sp)
    yap = _attn_prompt(qp, kp, vp, f_p, bp, sp, n_heads, dh)

    lf_new = lfts.reshape(n_heads, bs, ss).transpose(1, 0, 2)
    lf_past = cache_logf[l].astype(F32).transpose(0, 2, 1)
    tot = _round_up(past + LANE, CUMSUM_CHUNK)
    lf_all = jnp.concatenate([lf_past, lf_new, jnp.zeros((bs, n_heads, tot - past - ss), F32)], axis=-1)
    f_s = _cumsum_lanes(lf_all.reshape(bs * n_heads, tot)).reshape(bs, n_heads, tot)
    yas = _attn_sample(qs, cache_k, cache_v, l, ks, vs, f_s, bs, ss, past, n_heads, dh)

    def mixer_out(x, mod6, seq_len, h, yc, ya):
        gate1, shift2, scale2 = mod6[2], mod6[3], mod6[4]
        merged = _merge(yc, ya, h, w_brc_bf, w_bra_bf, w_gate_bf, b_gate2)
        return _wo(merged, x, w_o_bf, gate1, norm2_g[l], shift2, scale2, w_r, b_r, seq_len, n_groups, per_group)

    x1p, h2p, rtp = mixer_out(xp, mods["p"], sp, hp_, ycp, yap)
    x1s, h2s, rts = mixer_out(xs, mods["s"], ss, hs_, ycs, yas)

    h2_all = jnp.concatenate([h2p, h2s], axis=0)
    route_all = jnp.concatenate([rtp, rts], axis=0)
    tm_e = 256 if TOP_K * h2_all.shape[0] >= 8192 else 64
    dest, tok_sorted, tile_expert, tile_valid = _route_plan(route_all, n_experts, tm_e)
    ys = _experts(h2_all, tile_expert, tile_valid, tok_sorted, w1[l], w3[l], w2[l], tm_e)

    mp = bp * sp
    y_p = _combine(ys, dest[:mp], x1p, rtp, mods["p"][5], normf_g, sp)
    y_s = _combine(ys, dest[mp:], x1s, rts, mods["s"][5], normf_g, ss)

    def cache_outs(k, v, lf_t, tail, n_seq, seq_len):
        return (
            k.reshape(1, n_seq, seq_len, n_heads, dh),
            v.reshape(1, n_seq, seq_len, n_heads, dh),
            lf_t.T.reshape(1, n_seq, seq_len, n_heads),
            tail[:, CONV_TAIL_ROWS - 2:, :].reshape(1, n_seq, 2, d_conv),
        )

    return (y_p.reshape(bp, sp, d), y_s.reshape(bs, ss, d)) + cache_outs(kp, vp, lftp, tailp, bp, sp) + cache_outs(ks, vs, lfts, tails, bs, ss)
```

```python
import functools

import jax
import jax.numpy as jnp
from jax import lax
from jax.experimental import pallas as pl
from jax.experimental.pallas import tpu as pltpu

F32 = jnp.float32
BF16 = jnp.bfloat16
I32 = jnp.int32

EPS = 1e-6
NEG_INF = -1e30
TOP_K = 2
LANE = 128
BF16_ROWS = 16
VMEM_LIMIT_BYTES = 56 * 1024 * 1024
NT_DIMS = (((1,), (1,)), ((), ()))


def _tile(n, pref, align):
    if n <= pref:
        return n
    t = (pref // align) * align
    while t >= align:
        if n % t == 0:
            return t
        t -= align
    return n


def _round_up(n, m):
    return (n + m - 1) // m * m


def _params(*sem):
    return pltpu.CompilerParams(dimension_semantics=sem, vmem_limit_bytes=VMEM_LIMIT_BYTES)


def _rowvec(vec, seq_len, tm, tile=lambda i: i):
    n_seq, d = vec.shape
    if tm <= seq_len:
        per = seq_len // tm
        return vec.reshape(n_seq, 1, d), pl.BlockSpec((1, 1, d), lambda i, *_: (tile(i) // per, 0, 0))
    return jnp.repeat(vec, seq_len, axis=0), pl.BlockSpec((tm, d), lambda i, *_: (tile(i), 0))


def _bc(ref):
    v = ref[...]
    return v.reshape(v.shape[-2], v.shape[-1])


def _rms(x, g):
    return x * lax.rsqrt(jnp.mean(x * x, axis=-1, keepdims=True) + EPS) * g


def _ada_kernel(c_ref, w_ref, b_ref, o_ref):
    o_ref[...] = jnp.dot(c_ref[...].astype(BF16), w_ref[...].astype(BF16), preferred_element_type=F32) + b_ref[...]


def _ada(c_all, w, b):
    r, d = c_all.shape
    n = w.shape[1]
    tn = _tile(n, 1024, LANE)
    return pl.pallas_call(
        _ada_kernel,
        grid=(n // tn,),
        in_specs=[
            pl.BlockSpec((r, d), lambda j: (0, 0)),
            pl.BlockSpec((d, tn), lambda j: (0, j)),
            pl.BlockSpec((1, tn), lambda j: (0, j)),
        ],
        out_specs=pl.BlockSpec((r, tn), lambda j: (0, j)),
        out_shape=jax.ShapeDtypeStruct((r, n), F32),
        compiler_params=_params("parallel"),
        name="ada",
    )(c_all, w, b.reshape(1, n))


def _norm_kernel(x_ref, g_ref, sh_ref, sc_ref, h_ref):
    h = _rms(x_ref[...], g_ref[...]) * (1.0 + _bc(sc_ref)) + _bc(sh_ref)
    h_ref[...] = h.astype(h_ref.dtype)


def _norm(x, g, shift, scale, seq_len):
    m, d = x.shape
    tm = _tile(seq_len, 512, 8)
    sh, sh_spec = _rowvec(shift, seq_len, tm)
    sc, sc_spec = _rowvec(scale, seq_len, tm)
    return pl.pallas_call(
        _norm_kernel,
        grid=(m // tm,),
        in_specs=[pl.BlockSpec((tm, d), lambda i: (i, 0)), pl.BlockSpec((1, d), lambda i: (0, 0)), sh_spec, sc_spec],
        out_specs=pl.BlockSpec((tm, d), lambda i: (i, 0)),
        out_shape=jax.ShapeDtypeStruct((m, d), BF16),
        compiler_params=_params("parallel"),
        name="norm",
    )(x, g.reshape(1, d), sh, sc)


def _proj_kernel(a_ref, w_ref, o_ref, *, scale):
    acc = jnp.dot(a_ref[...], w_ref[...], preferred_element_type=F32)
    if scale != 1.0:
        acc = acc * scale
    o_ref[...] = acc.astype(o_ref.dtype)


RESIDENT_WEIGHT_BYTES = 8 * 1024 * 1024


def _proj(a, w, out_dtype, scale=1.0):
    m, k = a.shape
    n = w.shape[1]
    if k * n * w.dtype.itemsize <= RESIDENT_WEIGHT_BYTES:
        tm, tn = _tile(m, 512, BF16_ROWS), n
    else:
        tm, tn = _tile(m, 1024, BF16_ROWS), _tile(n, 512, LANE)
    return pl.pallas_call(
        functools.partial(_proj_kernel, scale=scale),
        grid=(m // tm, n // tn),
        in_specs=[pl.BlockSpec((tm, k), lambda i, j: (i, 0)), pl.BlockSpec((k, tn), lambda i, j: (0, j))],
        out_specs=pl.BlockSpec((tm, tn), lambda i, j: (i, j)),
        out_shape=jax.ShapeDtypeStruct((m, n), out_dtype),
        compiler_params=_params("parallel", "parallel"),
        name="proj",
    )(a, w)


CONV_TAIL_ROWS = 8


def _conv_kernel(*refs, seq_len, has_prev):
    if has_prev:
        h_ref, wb_ref, wc_ref, wv_ref, cw_ref, cb_ref, p1_ref, p2_ref, y_ref, tail_ref = refs
    else:
        h_ref, wb_ref, wc_ref, wv_ref, cw_ref, cb_ref, y_ref, tail_ref = refs
    h = h_ref[...]
    bg = jnp.dot(h, wb_ref[...], preferred_element_type=F32)
    cg = jnp.dot(h, wc_ref[...], preferred_element_type=F32)
    vc = jnp.dot(h, wv_ref[...], preferred_element_type=F32)
    u = cg * vc
    tm = u.shape[0]
    t = lax.broadcasted_iota(I32, (tm, 1), 0)
    if tm > seq_len:
        t = jnp.bitwise_and(t, seq_len - 1) if seq_len & (seq_len - 1) == 0 else lax.rem(t, seq_len)
    u1 = jnp.where(t >= 1, pltpu.roll(u, 1, 0), p1_ref[...] if has_prev else 0.0)
    u2 = jnp.where(t >= 2, pltpu.roll(u, 2, 0), p2_ref[...] if has_prev else 0.0)
    cw = cw_ref[...]
    conv = cb_ref[...] + ((cw[0:1] * u2 + cw[1:2] * u1) + cw[2:3] * u)
    y_ref[...] = (bg * conv).astype(y_ref.dtype)
    for s in range(tm // seq_len):
        tail_ref[s] = u[(s + 1) * seq_len - CONV_TAIL_ROWS:(s + 1) * seq_len, :]


def _conv_branch(h, w_in_bf, conv_w, conv_b, prev, seq_len, d_conv):
    m, d = h.shape
    assert conv_w.shape[0] == 3 and seq_len >= CONV_TAIL_ROWS
    tm = seq_len if seq_len >= 512 else m
    assert m % tm == 0 and tm % seq_len == 0
    tc = _tile(d_conv, 256, LANE)
    nc = d_conv // tc
    n_seq = m // seq_len
    seq_per_tile = tm // seq_len
    in_specs = [
        pl.BlockSpec((tm, d), lambda i, j: (i, 0)),
        pl.BlockSpec((d, tc), lambda i, j: (0, j)),
        pl.BlockSpec((d, tc), lambda i, j: (0, nc + j)),
        pl.BlockSpec((d, tc), lambda i, j: (0, 2 * nc + j)),
        pl.BlockSpec((3, tc), lambda i, j: (0, j)),
        pl.BlockSpec((1, tc), lambda i, j: (0, j)),
    ]
    args = [h, w_in_bf, w_in_bf, w_in_bf, conv_w, conv_b.reshape(1, d_conv)]
    if prev is not None:
        zeros = jnp.zeros((n_seq, seq_len, d_conv), F32)
        p1 = zeros.at[:, 0].set(prev[:, 1]).reshape(m, d_conv)
        p2 = zeros.at[:, 0].set(prev[:, 0]).at[:, 1].set(prev[:, 1]).reshape(m, d_conv)
        in_specs += [pl.BlockSpec((tm, tc), lambda i, j: (i, j))] * 2
        args += [p1, p2]
    return pl.pallas_call(
        functools.partial(_conv_kernel, seq_len=seq_len, has_prev=prev is not None),
        grid=(m // tm, nc),
        in_specs=in_specs,
        out_specs=[
            pl.BlockSpec((tm, tc), lambda i, j: (i, j)),
            pl.BlockSpec((seq_per_tile, CONV_TAIL_ROWS, tc), lambda i, j: (i, 0, j)),
        ],
        out_shape=[
            jax.ShapeDtypeStruct((m, d_conv), BF16),
            jax.ShapeDtypeStruct((n_seq, CONV_TAIL_ROWS, d_conv), F32),
        ],
        compiler_params=_params("parallel", "parallel"),
        name="conv_branch",
    )(*args)


def _forget_kernel(wf_ref, h_ref, b_ref, lf_ref):
    fl = lax.dot_general(wf_ref[...], h_ref[...], NT_DIMS, preferred_element_type=F32)
    lf_ref[...] = jax.nn.log_sigmoid(fl + b_ref[...])


def _forget(h, wf_t, b_f):
    m, d = h.shape
    hp = wf_t.shape[0]
    tm = _tile(m, 1024, LANE)
    return pl.pallas_call(
        _forget_kernel,
        grid=(m // tm,),
        in_specs=[
            pl.BlockSpec((hp, d), lambda i: (0, 0)),
            pl.BlockSpec((tm, d), lambda i: (i, 0)),
            pl.BlockSpec((hp, 1), lambda i: (0, 0)),
        ],
        out_specs=pl.BlockSpec((hp, tm), lambda i: (0, i)),
        out_shape=jax.ShapeDtypeStruct((hp, m), F32),
        compiler_params=_params("parallel"),
        name="forget",
    )(wf_t, h, b_f)


CUMSUM_CHUNK = LANE


def _cumsum_kernel(x_ref, o_ref):
    r, n = x_ref.shape
    c = CUMSUM_CHUNK
    tri = (lax.broadcasted_iota(I32, (c, c), 0) <= lax.broadcasted_iota(I32, (c, c), 1)).astype(BF16)
    carry = jnp.zeros((r, 1), F32)
    for j in range(n // c):
        x = x_ref[:, j * c:(j + 1) * c]
        cs = carry
        for _ in range(3):
            piece = x.astype(BF16)
            cs = cs + jnp.dot(piece, tri, preferred_element_type=F32)
            x = x - piece.astype(F32)
        o_ref[:, j * c:(j + 1) * c] = cs
        carry = cs[:, c - 1:c]


def _cumsum_lanes(x):
    r, n = x.shape
    tr = _tile(r, 128, 8)
    return pl.pallas_call(
        _cumsum_kernel,
        grid=(r // tr,),
        in_specs=[pl.BlockSpec((tr, n), lambda i: (i, 0))],
        out_specs=pl.BlockSpec((tr, n), lambda i: (i, 0)),
        out_shape=jax.ShapeDtypeStruct((r, n), F32),
        compiler_params=_params("parallel"),
        name="cumsum",
    )(x)


def _attn_prompt_kernel(q_ref, k_ref, v_ref, f_ref, o_ref, *, tq):
    seq_len = q_ref.shape[0]
    kb = k_ref[...].astype(BF16)
    vb = v_ref[...].astype(BF16)
    f = f_ref[...].reshape(1, seq_len)
    causal = lax.broadcasted_iota(I32, (tq, tq), 1) <= lax.broadcasted_iota(I32, (tq, tq), 0)
    for qi in range(seq_len // tq):
        lo = qi * tq
        q = q_ref[lo:lo + tq, :]
        sd = lax.dot_general(q, kb[lo:lo + tq], NT_DIMS, preferred_element_type=F32) - f[:, lo:lo + tq]
        sd = jnp.where(causal, sd, NEG_INF)
        m = jnp.max(sd, axis=-1, keepdims=True)
        if qi > 0:
            sp = lax.dot_general(q, kb[:lo], NT_DIMS, preferred_element_type=F32) - f[:, :lo]
            m = jnp.maximum(m, jnp.max(sp, axis=-1, keepdims=True))
            pp = jnp.exp(sp - m)
            l = jnp.sum(pp, axis=-1, keepdims=True)
            o = jnp.dot(pp.astype(BF16), vb[:lo], preferred_element_type=F32)
        pd = jnp.exp(sd - m)
        ld = jnp.sum(pd, axis=-1, keepdims=True)
        od = jnp.dot(pd.astype(BF16), vb[lo:lo + tq], preferred_element_type=F32)
        if qi > 0:
            l, o = l + ld, o + od
        else:
            l, o = ld, od
        o_ref[lo:lo + tq, :] = (o / l).astype(o_ref.dtype)


def _attn_prompt(q, k, v, f, n_seq, seq_len, n_heads, dh):
    m = n_seq * seq_len
    tq = _tile(seq_len, 256, LANE)
    blk = pl.BlockSpec((seq_len, dh), lambda b, h: (b, h))
    return pl.pallas_call(
        functools.partial(_attn_prompt_kernel, tq=tq),
        grid=(n_seq, n_heads),
        in_specs=[blk, blk, blk, pl.BlockSpec((1, 1, seq_len), lambda b, h: (b * n_heads + h, 0, 0))],
        out_specs=blk,
        out_shape=jax.ShapeDtypeStruct((m, n_heads * dh), BF16),
        compiler_params=_params("parallel", "parallel"),
        name="attn_prompt",
    )(q, k, v, f)


def _attn_sample_kernel(q_ref, kc_ref, vc_ref, fp_ref, kn_ref, vn_ref, fn_ref, o_ref, m_sc, l_sc, acc_sc, *, n_heads, dh):
    kt = pl.program_id(1)
    lq = q_ref.shape[0]

    @pl.when(kt == 0)
    def _():
        m_sc[...] = jnp.full(m_sc.shape, NEG_INF, F32)
        l_sc[...] = jnp.zeros(l_sc.shape, F32)
        acc_sc[...] = jnp.zeros(acc_sc.shape, F32)

    q = q_ref[...]

    def update(k_heads, v_heads, fk, mask):
        s_heads = []
        for h in range(n_heads):
            s = lax.dot_general(q[:, h * dh:(h + 1) * dh], k_heads[h], NT_DIMS, preferred_element_type=F32)
            s = s - fk[h:h + 1, :]
            if mask is not None:
                s = jnp.where(mask, s, NEG_INF)
            s_heads.append(s)
        s = jnp.concatenate(s_heads, axis=0)
        m_prev = m_sc[...]
        m_new = jnp.maximum(m_prev, jnp.max(s, axis=-1, keepdims=True))
        alpha = jnp.exp(m_prev - m_new)
        p = jnp.exp(s - m_new)
        l_sc[...] = alpha * l_sc[...] + jnp.sum(p, axis=-1, keepdims=True)
        pb = p.astype(BF16)
        o_heads = [jnp.dot(pb[h * lq:(h + 1) * lq], v_heads[h], preferred_element_type=F32) for h in range(n_heads)]
        acc_sc[...] = alpha * acc_sc[...] + jnp.concatenate(o_heads, axis=0)
        m_sc[...] = m_new

    def cached_heads(ref):
        x = jnp.swapaxes(ref[0, 0], 0, 1).astype(BF16)
        return [x[h] for h in range(n_heads)]

    update(cached_heads(kc_ref), cached_heads(vc_ref), fp_ref[0], None)

    @pl.when(kt == pl.num_programs(1) - 1)
    def _():
        pad = jnp.zeros((LANE - lq, n_heads * dh), BF16)
        kn = jnp.concatenate([kn_ref[...].astype(BF16), pad], axis=0)
        vn = jnp.concatenate([vn_ref[...].astype(BF16), pad], axis=0)
        mask = lax.broadcasted_iota(I32, (lq, LANE), 1) <= lax.broadcasted_iota(I32, (lq, LANE), 0)
        heads = lambda x: [x[:, h * dh:(h + 1) * dh] for h in range(n_heads)]
        update(heads(kn), heads(vn), fn_ref[0], mask)
        out = acc_sc[...] / l_sc[...]
        o_ref[...] = jnp.concatenate([out[h * lq:(h + 1) * lq] for h in range(n_heads)], axis=1).astype(o_ref.dtype)


def _attn_sample(q, cache_k, cache_v, layer, k_new, v_new, f_all, n_seq, lq, past, n_heads, dh):
    da = n_heads * dh
    tk = _tile(past, 512, LANE)
    assert lq <= LANE and lq % BF16_ROWS == 0 and past % LANE == 0
    qblk = pl.BlockSpec((lq, da), lambda b, t: (b, 0))
    cblk = pl.BlockSpec((1, 1, tk, n_heads, dh), lambda b, t: (layer, b, t, 0, 0))
    return pl.pallas_call(
        functools.partial(_attn_sample_kernel, n_heads=n_heads, dh=dh),
        grid=(n_seq, past // tk),
        in_specs=[
            qblk,
            cblk,
            cblk,
            pl.BlockSpec((1, n_heads, tk), lambda b, t: (b, 0, t)),
            qblk,
            qblk,
            pl.BlockSpec((1, n_heads, LANE), lambda b, t: (b, 0, past // LANE)),
        ],
        out_specs=qblk,
        out_shape=jax.ShapeDtypeStruct((n_seq * lq, da), BF16),
        scratch_shapes=[
            pltpu.VMEM((n_heads * lq, 1), F32),
            pltpu.VMEM((n_heads * lq, 1), F32),
            pltpu.VMEM((n_heads * lq, dh), F32),
        ],
        compiler_params=_params("parallel", "arbitrary"),
        name="attn_sample",
    )(q, cache_k, cache_v, f_all, k_new, v_new, f_all)


def _merge_kernel(yc_ref, ya_ref, h_ref, wc_ref, wa_ref, wgc_ref, wga_ref, bgc_ref, bga_ref, o_ref):
    h = h_ref[...]
    a = jnp.dot(yc_ref[...], wc_ref[...], preferred_element_type=F32)
    b = jnp.dot(ya_ref[...], wa_ref[...], preferred_element_type=F32)
    gc = jax.nn.sigmoid(jnp.dot(h, wgc_ref[...], preferred_element_type=F32) + bgc_ref[...])
    ga = jax.nn.sigmoid(jnp.dot(h, wga_ref[...], preferred_element_type=F32) + bga_ref[...])
    o_ref[...] = (gc * a + ga * b).astype(o_ref.dtype)


def _merge(yc, ya, h, w_brc, w_bra, w_gate, b_gate):
    m, d = h.shape
    dc, da = yc.shape[1], ya.shape[1]
    tm = _tile(m, 1024, BF16_ROWS)
    tn = _tile(d, 512, LANE)
    nj = d // tn
    return pl.pallas_call(
        _merge_kernel,
        grid=(m // tm, nj),
        in_specs=[
            pl.BlockSpec((tm, dc), lambda i, j: (i, 0)),
            pl.BlockSpec((tm, da), lambda i, j: (i, 0)),
            pl.BlockSpec((tm, d), lambda i, j: (i, 0)),
            pl.BlockSpec((dc, tn), lambda i, j: (0, j)),
            pl.BlockSpec((da, tn), lambda i, j: (0, j)),
            pl.BlockSpec((d, tn), lambda i, j: (0, j)),
            pl.BlockSpec((d, tn), lambda i, j: (0, nj + j)),
            pl.BlockSpec((1, tn), lambda i, j: (0, j)),
            pl.BlockSpec((1, tn), lambda i, j: (0, nj + j)),
        ],
        out_specs=pl.BlockSpec((tm, tn), lambda i, j: (i, j)),
        out_shape=jax.ShapeDtypeStruct((m, d), BF16),
        compiler_params=_params("parallel", "parallel"),
        name="merge",
    )(yc, ya, h, w_brc, w_bra, w_gate, w_gate, b_gate, b_gate)


ROUTE_ID0, ROUTE_ID1, ROUTE_W0, ROUTE_W1 = 0, 1, 2, 3


def _wo_kernel(*refs, n_tiles_a, n_groups, per_group):
    a_refs, b_refs, (wo_ref, ng_ref, wr_ref, br_ref), outs = refs[0:5], refs[5:10], refs[10:14], refs[14:17]
    i = pl.program_id(0)

    @pl.when(i < n_tiles_a)
    def _():
        _wo_tile(*a_refs, wo_ref, ng_ref, wr_ref, br_ref, *outs, n_groups=n_groups, per_group=per_group)

    @pl.when(i >= n_tiles_a)
    def _():
        _wo_tile(*b_refs, wo_ref, ng_ref, wr_ref, br_ref, *outs, n_groups=n_groups, per_group=per_group)


def _wo_tile(mg_ref, x_ref, g1_ref, sh_ref, sc_ref, wo_ref, ng_ref, wr_ref, br_ref, x1_ref, h2_ref, rt_ref, *, n_groups, per_group):
    o = jnp.dot(mg_ref[...], wo_ref[...], preferred_element_type=F32)
    x1 = x_ref[...] + _bc(g1_ref) * o
    x1_ref[...] = x1
    h2 = _rms(x1, ng_ref[...]) * (1.0 + _bc(sc_ref)) + _bc(sh_ref)
    h2_ref[...] = h2
    wr = wr_ref[...]
    hi = h2.astype(BF16)
    lo = (h2 - hi.astype(F32)).astype(BF16)
    r_hi = jnp.dot(hi, wr, preferred_element_type=F32)
    r_lo = jnp.dot(lo, wr[:, :LANE], preferred_element_type=F32)
    lg = r_hi[:, :LANE] + (r_hi[:, LANE:] + r_lo) + br_ref[...]
    col = lax.broadcasted_iota(I32, lg.shape, 1).astype(F32)
    big = float(LANE)
    gl = jnp.where(col < n_groups, lg, NEG_INF)
    gmax = jnp.max(gl, axis=-1, keepdims=True)
    g_val = 1.0 / jnp.sum(jnp.exp(gl - gmax), axis=-1, keepdims=True)
    g_idx = jnp.min(jnp.where(gl == gmax, col, big), axis=-1, keepdims=True)
    lo = n_groups + g_idx * per_group
    el = jnp.where((col >= lo) & (col < lo + per_group), lg, NEG_INF)
    m1 = jnp.max(el, axis=-1, keepdims=True)
    i1 = jnp.min(jnp.where(el == m1, col, big), axis=-1, keepdims=True)
    el2 = jnp.where(col == i1, NEG_INF, el)
    m2 = jnp.max(el2, axis=-1, keepdims=True)
    i2 = jnp.min(jnp.where(el2 == m2, col, big), axis=-1, keepdims=True)
    r = jnp.exp(m2 - m1)
    w0 = g_val / (1.0 + r)
    w1 = g_val * r / (1.0 + r)
    rt = jnp.where(col == ROUTE_ID0, i1 - n_groups, 0.0)
    rt = jnp.where(col == ROUTE_ID1, i2 - n_groups, rt)
    rt = jnp.where(col == ROUTE_W0, w0, rt)
    rt = jnp.where(col == ROUTE_W1, w1, rt)
    rt_ref[...] = rt


def _wo(path_a, path_b, w_o, norm_g, w_r, b_r, n_groups, per_group):
    d = w_o.shape[0]
    ma, mb = path_a[1].shape[0], path_b[1].shape[0]
    tm = _tile(min(ma, mb), 256, BF16_ROWS)
    assert ma % tm == 0 and mb % tm == 0
    na, nb = ma // tm, mb // tm
    tiles = (lambda i: jnp.minimum(i, na - 1), lambda i: jnp.maximum(i - na, 0))
    in_specs, args = [], []
    for (merged, x, gate1, shift2, scale2, seq_len), tile in zip((path_a, path_b), tiles):
        row = pl.BlockSpec((tm, d), lambda i, tile=tile: (tile(i), 0))
        in_specs += [row, row]
        args += [merged, x]
        for vec in (gate1, shift2, scale2):
            arr, spec = _rowvec(vec, seq_len, tm, tile)
            in_specs.append(spec)
            args.append(arr)
    const = lambda shape: pl.BlockSpec(shape, lambda i: (0, 0))
    in_specs += [const((d, d)), const((1, d)), const((d, 2 * LANE)), const((1, LANE))]
    args += [w_o, norm_g.reshape(1, d), w_r, b_r]
    row = pl.BlockSpec((tm, d), lambda i: (i, 0))
    return pl.pallas_call(
        functools.partial(_wo_kernel, n_tiles_a=na, n_groups=n_groups, per_group=per_group),
        grid=(na + nb,),
        in_specs=in_specs,
        out_specs=[row, row, pl.BlockSpec((tm, LANE), lambda i: (i, 0))],
        out_shape=[
            jax.ShapeDtypeStruct((ma + mb, d), F32),
            jax.ShapeDtypeStruct((ma + mb, d), F32),
            jax.ShapeDtypeStruct((ma + mb, LANE), F32),
        ],
        compiler_params=_params("arbitrary"),
        name="wo_norm_router",
    )(*args)


GATHER_UNROLL = 8


def _gather_start(idx_ref, src_hbm, dst, slot, sem, n_rows, first=0):
    for r in range(first, first + n_rows):
        pltpu.make_async_copy(src_hbm.at[pl.ds(idx_ref[0, 0, r], 1)], dst.at[slot, pl.ds(r, 1)], sem.at[slot]).start()


def _gather_wait(src_hbm, dst, slot, sem, n_rows):
    pltpu.make_async_copy(src_hbm.at[pl.ds(0, n_rows)], dst.at[slot], sem.at[slot]).wait()


def _expert_kernel(te_ref, tv_ref, tok_ref, tokn_ref, h2_hbm, w1_ref, w3_ref, w2_ref, o_ref, xbuf, w1b, w3b, w2b, sem):
    i = pl.program_id(0)
    n = pl.num_programs(0)
    tm = xbuf.shape[1]
    slot = lax.rem(i, 2)
    prev = jnp.maximum(i - 1, 0)
    valid = tv_ref[i] == 1

    @pl.when(jnp.logical_and(i == 0, valid))
    def _():
        _gather_start(tok_ref, h2_hbm, xbuf, 0, sem, tm)

    @pl.when(valid)
    def _():
        _gather_wait(h2_hbm, xbuf, slot, sem, tm)

        @pl.when(jnp.logical_or(i == 0, te_ref[i] != te_ref[prev]))
        def _():
            w1b[...] = w1_ref[0].astype(BF16)
            w3b[...] = w3_ref[0].astype(BF16)
            w2b[...] = w2_ref[0].astype(BF16)

        x = xbuf[slot].astype(BF16)
        a = jnp.dot(x, w1b[...], preferred_element_type=F32)
        _gather_start(tokn_ref, h2_hbm, xbuf, 1 - slot, sem, tm // 2)
        b = jnp.dot(x, w3b[...], preferred_element_type=F32)
        _gather_start(tokn_ref, h2_hbm, xbuf, 1 - slot, sem, tm // 2, first=tm // 2)
        hm = (jax.nn.silu(a) * b).astype(BF16)
        o_ref[...] = jnp.dot(hm, w2b[...], preferred_element_type=F32)

    @pl.when(jnp.logical_not(valid))
    def _():
        @pl.when(jnp.logical_and(i > 0, tv_ref[prev] == 1))
        def _():
            _gather_wait(h2_hbm, xbuf, slot, sem, tm)

        o_ref[...] = jnp.zeros(o_ref.shape, F32)


def _experts(h2_all, tile_expert, tile_valid, tok_sorted, w1, w3, w2, tm):
    n_tiles = tile_expert.shape[0]
    _, d, de = w1.shape
    tok3 = tok_sorted.reshape(n_tiles, 1, tm)
    grid_spec = pltpu.PrefetchScalarGridSpec(
        num_scalar_prefetch=2,
        grid=(n_tiles,),
        in_specs=[
            pl.BlockSpec((1, 1, tm), lambda i, te, tv: (i, 0, 0), memory_space=pltpu.SMEM),
            pl.BlockSpec((1, 1, tm), lambda i, te, tv: (jnp.minimum(i + 1, n_tiles - 1), 0, 0), memory_space=pltpu.SMEM),
            pl.BlockSpec(memory_space=pl.ANY),
            pl.BlockSpec((1, d, de), lambda i, te, tv: (te[i], 0, 0)),
            pl.BlockSpec((1, d, de), lambda i, te, tv: (te[i], 0, 0)),
            pl.BlockSpec((1, de, d), lambda i, te, tv: (te[i], 0, 0)),
        ],
        out_specs=pl.BlockSpec((tm, d), lambda i, te, tv: (i, 0)),
        scratch_shapes=[
            pltpu.VMEM((2, tm, d), F32),
            pltpu.VMEM((d, de), BF16),
            pltpu.VMEM((d, de), BF16),
            pltpu.VMEM((de, d), BF16),
            pltpu.SemaphoreType.DMA((2,)),
        ],
    )
    return pl.pallas_call(
        _expert_kernel,
        grid_spec=grid_spec,
        out_shape=jax.ShapeDtypeStruct((n_tiles * tm, d), F32),
        compiler_params=_params("arbitrary"),
        name="experts",
    )(tile_expert, tile_valid, tok3, tok3, h2_all, w1, w3, w2)


COMBINE_CHUNKS = 4


def _combine_kernel(d_ref, dn_ref, ys_hbm, x1_ref, rt_ref, g2_ref, ng_ref, y_ref, ybuf, sem):
    i = pl.program_id(0)
    tm = x1_ref.shape[0]
    slot = lax.rem(i, 2)

    @pl.when(i == 0)
    def _():
        _gather_start(d_ref, ys_hbm, ybuf, 0, sem, 2 * tm)

    _gather_wait(ys_hbm, ybuf, slot, sem, 2 * tm)
    g2 = _bc(g2_ref)
    ng = ng_ref[...]
    ck = tm // COMBINE_CHUNKS
    for c in range(COMBINE_CHUNKS):
        _gather_start(dn_ref, ys_hbm, ybuf, 1 - slot, sem, 2 * ck, first=2 * ck * c)
        rows = slice(c * ck, (c + 1) * ck)
        rt = rt_ref[rows, :]
        y0 = ybuf[slot, c * ck:(c + 1) * ck]
        y1 = ybuf[slot, tm + c * ck:tm + (c + 1) * ck]
        moe = rt[:, ROUTE_W0:ROUTE_W0 + 1] * y0 + rt[:, ROUTE_W1:ROUTE_W1 + 1] * y1
        g2c = g2 if g2.shape[0] == 1 else g2[rows]
        y_ref[rows, :] = _rms(x1_ref[rows, :] + g2c * moe, ng)

    @pl.when(i == pl.num_programs(0) - 1)
    def _():
        _gather_wait(ys_hbm, ybuf, 1 - slot, sem, 2 * tm)


def _combine(ys, dest, x1_all, route_all, row_off, m, gate2, normf_g, seq_len):
    d = x1_all.shape[1]
    tm = _tile(seq_len, 256, 8) if seq_len >= 256 else _tile(m, 256, seq_len)
    assert row_off % tm == 0 and tm % COMBINE_CHUNKS == 0
    nb, off = m // tm, row_off // tm
    d3 = dest.reshape(nb, tm, 2).transpose(0, 2, 1).reshape(nb, 1, 2 * tm)
    d3 = jnp.concatenate([d3, jnp.zeros((1, 1, 2 * tm), I32)], axis=0)
    g2, g2_spec = _rowvec(gate2, seq_len, tm)
    return pl.pallas_call(
        _combine_kernel,
        grid=(nb,),
        in_specs=[
            pl.BlockSpec((1, 1, 2 * tm), lambda i: (i, 0, 0), memory_space=pltpu.SMEM),
            pl.BlockSpec((1, 1, 2 * tm), lambda i: (i + 1, 0, 0), memory_space=pltpu.SMEM),
            pl.BlockSpec(memory_space=pl.ANY),
            pl.BlockSpec((tm, d), lambda i: (off + i, 0)),
            pl.BlockSpec((tm, LANE), lambda i: (off + i, 0)),
            g2_spec,
            pl.BlockSpec((1, d), lambda i: (0, 0)),
        ],
        out_specs=pl.BlockSpec((tm, d), lambda i: (i, 0)),
        out_shape=jax.ShapeDtypeStruct((m, d), F32),
        scratch_shapes=[pltpu.VMEM((2, 2 * tm, d), F32), pltpu.SemaphoreType.DMA((2,))],
        compiler_params=_params("arbitrary"),
        name="combine_final_norm",
    )(d3, d3, ys, x1_all, route_all, g2, normf_g.reshape(1, d))


def _route_plan(route, n_experts, tm):
    t = route.shape[0]
    s = TOP_K * t
    e_flat = route[:, ROUTE_ID0:ROUTE_ID1 + 1].astype(I32).reshape(s)
    onehot = (e_flat[:, None] == jnp.arange(n_experts, dtype=I32)[None, :]).astype(I32)
    csum = jnp.cumsum(onehot, axis=0)
    rank = jnp.take_along_axis(csum, e_flat[:, None], axis=1)[:, 0] - 1
    counts = csum[-1]
    padded = (counts + tm - 1) // tm * tm
    ends = jnp.cumsum(padded)
    dest = (ends - padded)[e_flat] + rank
    n_tiles = pl.cdiv(s, tm) + n_experts + 1
    tile_start = jnp.arange(n_tiles, dtype=I32) * tm
    tile_valid = (tile_start < ends[-1]).astype(I32)
    te = jnp.minimum(jnp.sum((ends[None, :] <= tile_start[:, None]).astype(I32), axis=1), n_experts - 1)
    last_used = jnp.maximum(ends[-1] // tm - 1, 0)
    tile_expert = jnp.where(tile_valid == 1, te, te[last_used])
    tok_sorted = jnp.zeros((n_tiles * tm,), I32).at[dest].set(jnp.arange(s, dtype=I32) // TOP_K)
    return dest.reshape(t, TOP_K), tok_sorted, tile_expert, tile_valid


def kernel(x_prompt, x_sample, cache_k, cache_v, cache_logf, state_conv, c_prompt, c_sample, w_ada, b_ada, norm1_g, norm2_g, normf_g, w_in, conv_w, conv_b, b_forget, w_br_conv, w_br_attn, w_gate, b_gate, w_o, w_rg, b_rg, w_re, b_re, w1, w3, w2):
    depth = w_ada.shape[0]
    assert depth == 1, "single-layer step"
    l = 0
    bp, sp, d = x_prompt.shape
    bs, ss, _ = x_sample.shape
    _, _, past, n_heads, dh = cache_k.shape
    d_conv = state_conv.shape[-1]
    da = n_heads * dh
    n_groups = w_rg.shape[-1]
    n_experts = w_re.shape[-1]
    per_group = n_experts // n_groups
    assert n_groups + n_experts <= LANE
    scale = dh ** -0.5

    w_in_bf = w_in[l].astype(BF16)
    q_off = 3 * d_conv
    w_q, w_k, w_v = (w_in_bf[:, q_off + j * da:q_off + (j + 1) * da] for j in range(3))
    hp = _round_up(n_heads, BF16_ROWS)
    wf_t = jnp.zeros((hp, d), BF16).at[:n_heads].set(w_in[l][:, 3 * d_conv + 3 * da:].T.astype(BF16))
    b_f = jnp.zeros((hp, 1), F32).at[:n_heads, 0].set(b_forget[l])
    w_brc_bf = w_br_conv[l].astype(BF16)
    w_bra_bf = w_br_attn[l].astype(BF16)
    w_gate_bf = w_gate[l].astype(BF16)
    b_gate2 = b_gate[l].reshape(1, 2 * d)
    w_o_bf = w_o[l].astype(BF16)
    w_r32 = jnp.pad(jnp.concatenate([w_rg[l], w_re[l]], axis=1), ((0, 0), (0, LANE - n_groups - n_experts)))
    w_r_hi = w_r32.astype(BF16)
    w_r = jnp.concatenate([w_r_hi, (w_r32 - w_r_hi.astype(F32)).astype(BF16)], axis=1)
    b_r = jnp.zeros((1, LANE), F32).at[0, :n_groups].set(b_rg[l]).at[0, n_groups:n_groups + n_experts].set(b_re[l])

    mod = _ada(jnp.concatenate([c_prompt, c_sample], axis=0), w_ada[l], b_ada[l])
    mods = {"p": jnp.split(mod[:bp], 6, axis=-1), "s": jnp.split(mod[bp:], 6, axis=-1)}

    def mixer_inputs(x, mod6, seq_len, prev):
        shift1, scale1 = mod6[0], mod6[1]
        h = _norm(x, norm1_g[l], shift1, scale1, seq_len)
        y_conv, tail = _conv_branch(h, w_in_bf, conv_w[l], conv_b[l], prev, seq_len, d_conv)
        q = _proj(h, w_q, BF16, scale)
        k = _proj(h, w_k, F32)
        v = _proj(h, w_v, F32)
        lf_t = _forget(h, wf_t, b_f)[:n_heads]
        return h, y_conv, tail, q, k, v, lf_t

    xp = x_prompt.reshape(bp * sp, d)
    xs = x_sample.reshape(bs * ss, d)
    hp_, ycp, tailp, qp, kp, vp, lftp = mixer_inputs(xp, mods["p"], sp, None)
    hs_, ycs, tails, qs, ks, vs, lfts = mixer_inputs(xs, mods["s"], ss, state_conv[l])

    lf_p = lftp.reshape(n_heads, bp, sp).transpose(1, 0, 2).reshape(bp * n_heads, sp)
    f_p = _cumsum_lanes(lf_p).reshape(bp * n_heads, 1, sp)
    yap = _attn_prompt(qp, kp, vp, f_p, bp, sp, n_heads, dh)

    lf_new = lfts.reshape(n_heads, bs, ss).transpose(1, 0, 2)
    lf_past = cache_logf[l].astype(F32).transpose(0, 2, 1)
    tot = _round_up(past + LANE, CUMSUM_CHUNK)
    lf_all = jnp.concatenate([lf_past, lf_new, jnp.zeros((bs, n_heads, tot - past - ss), F32)], axis=-1)
    f_s = _cumsum_lanes(lf_all.reshape(bs * n_heads, tot)).reshape(bs, n_heads, tot)
    yas = _attn_sample(qs, cache_k, cache_v, l, ks, vs, f_s, bs, ss, past, n_heads, dh)

    def wo_path(x, mod6, seq_len, h, yc, ya):
        gate1, shift2, scale2 = mod6[2], mod6[3], mod6[4]
        merged = _merge(yc, ya, h, w_brc_bf, w_bra_bf, w_gate_bf, b_gate2)
        return merged, x, gate1, shift2, scale2, seq_len

    x1_all, h2_all, route_all = _wo(
        wo_path(xp, mods["p"], sp, hp_, ycp, yap), wo_path(xs, mods["s"], ss, hs_, ycs, yas),
        w_o_bf, norm2_g[l], w_r, b_r, n_groups, per_group)
    tm_e = 256 if TOP_K * h2_all.shape[0] >= 8192 else 64
    dest, tok_sorted, tile_expert, tile_valid = _route_plan(route_all, n_experts, tm_e)
    ys = _experts(h2_all, tile_expert, tile_valid, tok_sorted, w1[l], w3[l], w2[l], tm_e)

    mp, ms = bp * sp, bs * ss
    y_p = _combine(ys, dest[:mp], x1_all, route_all, 0, mp, mods["p"][5], normf_g, sp)
    y_s = _combine(ys, dest[mp:], x1_all, route_all, mp, ms, mods["s"][5], normf_g, ss)

    def cache_outs(k, v, lf_t, tail, n_seq, seq_len):
        return (
            k.reshape(1, n_seq, seq_len, n_heads, dh),
            v.reshape(1, n_seq, seq_len, n_heads, dh),
            lf_t.T.reshape(1, n_seq, seq_len, n_heads),
            tail[:, CONV_TAIL_ROWS - 2:, :].reshape(1, n_seq, 2, d_conv),
        )

    return (y_p.reshape(bp, sp, d), y_s.reshape(bs, ss, d)) + cache_outs(kp, vp, lftp, tailp, bp, sp) + cache_outs(ks, vs, lfts, tails, bs, ss)
```

```python
import functools

import jax
import jax.numpy as jnp
from jax import lax
from jax.experimental import pallas as pl
from jax.experimental.pallas import tpu as pltpu

F32 = jnp.float32
BF16 = jnp.bfloat16
I32 = jnp.int32

EPS = 1e-6
NEG_INF = -1e30
LOG2_E = 1.4426950408889634
TOP_K = 2
LANE = 128
BF16_ROWS = 16
VMEM_LIMIT_BYTES = 56 * 1024 * 1024
NT_DIMS = (((1,), (1,)), ((), ()))


def _tile(n, pref, align):
    if n <= pref:
        return n
    t = (pref // align) * align
    while t >= align:
        if n % t == 0:
            return t
        t -= align
    return n


def _round_up(n, m):
    return (n + m - 1) // m * m


def _params(*sem):
    return pltpu.CompilerParams(dimension_semantics=sem, vmem_limit_bytes=VMEM_LIMIT_BYTES)


def _rowvec(vec, seq_len, tm, tile=lambda i: i):
    n_seq, d = vec.shape
    if tm <= seq_len:
        per = seq_len // tm
        return vec.reshape(n_seq, 1, d), pl.BlockSpec((1, 1, d), lambda i, *_: (tile(i) // per, 0, 0))
    return jnp.repeat(vec, seq_len, axis=0), pl.BlockSpec((tm, d), lambda i, *_: (tile(i), 0))


def _bc(ref):
    v = ref[...]
    return v.reshape(v.shape[-2], v.shape[-1])


def _rms(x, g):
    return x * lax.rsqrt(jnp.mean(x * x, axis=-1, keepdims=True) + EPS) * g


def _ada_kernel(c_ref, w_ref, b_ref, o_ref):
    o_ref[...] = jnp.dot(c_ref[...].astype(BF16), w_ref[...].astype(BF16), preferred_element_type=F32) + b_ref[...]


def _ada(c_all, w, b):
    r, d = c_all.shape
    n = w.shape[1]
    tn = _tile(n, 1024, LANE)
    return pl.pallas_call(
        _ada_kernel,
        grid=(n // tn,),
        in_specs=[
            pl.BlockSpec((r, d), lambda j: (0, 0)),
            pl.BlockSpec((d, tn), lambda j: (0, j)),
            pl.BlockSpec((1, tn), lambda j: (0, j)),
        ],
        out_specs=pl.BlockSpec((r, tn), lambda j: (0, j)),
        out_shape=jax.ShapeDtypeStruct((r, n), F32),
        compiler_params=_params("parallel"),
        name="ada",
    )(c_all, w, b.reshape(1, n))


def _norm_kernel(x_ref, g_ref, sh_ref, sc_ref, h_ref):
    h = _rms(x_ref[...], g_ref[...]) * (1.0 + _bc(sc_ref)) + _bc(sh_ref)
    h_ref[...] = h.astype(h_ref.dtype)


def _norm(x, g, shift, scale, seq_len):
    m, d = x.shape
    tm = _tile(seq_len, 512, 8)
    sh, sh_spec = _rowvec(shift, seq_len, tm)
    sc, sc_spec = _rowvec(scale, seq_len, tm)
    return pl.pallas_call(
        _norm_kernel,
        grid=(m // tm,),
        in_specs=[pl.BlockSpec((tm, d), lambda i: (i, 0)), pl.BlockSpec((1, d), lambda i: (0, 0)), sh_spec, sc_spec],
        out_specs=pl.BlockSpec((tm, d), lambda i: (i, 0)),
        out_shape=jax.ShapeDtypeStruct((m, d), BF16),
        compiler_params=_params("parallel"),
        name="norm",
    )(x, g.reshape(1, d), sh, sc)


def _proj_kernel(a_ref, w_ref, o_ref, *, scale):
    acc = jnp.dot(a_ref[...], w_ref[...], preferred_element_type=F32)
    if scale != 1.0:
        acc = acc * scale
    o_ref[...] = acc.astype(o_ref.dtype)


RESIDENT_WEIGHT_BYTES = 8 * 1024 * 1024


def _proj(a, w, out_dtype, scale=1.0):
    m, k = a.shape
    n = w.shape[1]
    if k * n * w.dtype.itemsize <= RESIDENT_WEIGHT_BYTES:
        tm, tn = _tile(m, 512, BF16_ROWS), n
    else:
        tm, tn = _tile(m, 1024, BF16_ROWS), _tile(n, 512, LANE)
    return pl.pallas_call(
        functools.partial(_proj_kernel, scale=scale),
        grid=(m // tm, n // tn),
        in_specs=[pl.BlockSpec((tm, k), lambda i, j: (i, 0)), pl.BlockSpec((k, tn), lambda i, j: (0, j))],
        out_specs=pl.BlockSpec((tm, tn), lambda i, j: (i, j)),
        out_shape=jax.ShapeDtypeStruct((m, n), out_dtype),
        compiler_params=_params("parallel", "parallel"),
        name="proj",
    )(a, w)


CONV_TAIL_ROWS = 8


def _conv_kernel(*refs, seq_len, has_prev):
    if has_prev:
        h_ref, wb_ref, wc_ref, wv_ref, cw_ref, cb_ref, p1_ref, p2_ref, y_ref, tail_ref = refs
    else:
        h_ref, wb_ref, wc_ref, wv_ref, cw_ref, cb_ref, y_ref, tail_ref = refs
    h = h_ref[...]
    bg = jnp.dot(h, wb_ref[...], preferred_element_type=F32)
    cg = jnp.dot(h, wc_ref[...], preferred_element_type=F32)
    vc = jnp.dot(h, wv_ref[...], preferred_element_type=F32)
    u = cg * vc
    tm = u.shape[0]
    t = lax.broadcasted_iota(I32, (tm, 1), 0)
    if tm > seq_len:
        t = jnp.bitwise_and(t, seq_len - 1) if seq_len & (seq_len - 1) == 0 else lax.rem(t, seq_len)
    u1 = jnp.where(t >= 1, pltpu.roll(u, 1, 0), p1_ref[...] if has_prev else 0.0)
    u2 = jnp.where(t >= 2, pltpu.roll(u, 2, 0), p2_ref[...] if has_prev else 0.0)
    cw = cw_ref[...]
    conv = cb_ref[...] + ((cw[0:1] * u2 + cw[1:2] * u1) + cw[2:3] * u)
    y_ref[...] = (bg * conv).astype(y_ref.dtype)
    for s in range(tm // seq_len):
        tail_ref[s] = u[(s + 1) * seq_len - CONV_TAIL_ROWS:(s + 1) * seq_len, :]


def _conv_branch(h, w_in_bf, conv_w, conv_b, prev, seq_len, d_conv):
    m, d = h.shape
    assert conv_w.shape[0] == 3 and seq_len >= CONV_TAIL_ROWS
    tm = seq_len if seq_len >= 512 else m
    assert m % tm == 0 and tm % seq_len == 0
    tc = _tile(d_conv, 256, LANE)
    nc = d_conv // tc
    n_seq = m // seq_len
    seq_per_tile = tm // seq_len
    in_specs = [
        pl.BlockSpec((tm, d), lambda i, j: (i, 0)),
        pl.BlockSpec((d, tc), lambda i, j: (0, j)),
        pl.BlockSpec((d, tc), lambda i, j: (0, nc + j)),
        pl.BlockSpec((d, tc), lambda i, j: (0, 2 * nc + j)),
        pl.BlockSpec((3, tc), lambda i, j: (0, j)),
        pl.BlockSpec((1, tc), lambda i, j: (0, j)),
    ]
    args = [h, w_in_bf, w_in_bf, w_in_bf, conv_w, conv_b.reshape(1, d_conv)]
    if prev is not None:
        zeros = jnp.zeros((n_seq, seq_len, d_conv), F32)
        p1 = zeros.at[:, 0].set(prev[:, 1]).reshape(m, d_conv)
        p2 = zeros.at[:, 0].set(prev[:, 0]).at[:, 1].set(prev[:, 1]).reshape(m, d_conv)
        in_specs += [pl.BlockSpec((tm, tc), lambda i, j: (i, j))] * 2
        args += [p1, p2]
    return pl.pallas_call(
        functools.partial(_conv_kernel, seq_len=seq_len, has_prev=prev is not None),
        grid=(m // tm, nc),
        in_specs=in_specs,
        out_specs=[
            pl.BlockSpec((tm, tc), lambda i, j: (i, j)),
            pl.BlockSpec((seq_per_tile, CONV_TAIL_ROWS, tc), lambda i, j: (i, 0, j)),
        ],
        out_shape=[
            jax.ShapeDtypeStruct((m, d_conv), BF16),
            jax.ShapeDtypeStruct((n_seq, CONV_TAIL_ROWS, d_conv), F32),
        ],
        compiler_params=_params("parallel", "parallel"),
        name="conv_branch",
    )(*args)


def _forget_kernel(wf_ref, h_ref, b_ref, lf_ref):
    fl = lax.dot_general(wf_ref[...], h_ref[...], NT_DIMS, preferred_element_type=F32)
    lf_ref[...] = jax.nn.log_sigmoid(fl + b_ref[...])


def _forget(h, wf_t, b_f):
    m, d = h.shape
    hp = wf_t.shape[0]
    tm = _tile(m, 1024, LANE)
    return pl.pallas_call(
        _forget_kernel,
        grid=(m // tm,),
        in_specs=[
            pl.BlockSpec((hp, d), lambda i: (0, 0)),
            pl.BlockSpec((tm, d), lambda i: (i, 0)),
            pl.BlockSpec((hp, 1), lambda i: (0, 0)),
        ],
        out_specs=pl.BlockSpec((hp, tm), lambda i: (0, i)),
        out_shape=jax.ShapeDtypeStruct((hp, m), F32),
        compiler_params=_params("parallel"),
        name="forget",
    )(wf_t, h, b_f)


CUMSUM_CHUNK = LANE


def _cumsum_kernel(x_ref, o_ref):
    r, n = x_ref.shape
    c = CUMSUM_CHUNK
    tri = (lax.broadcasted_iota(I32, (c, c), 0) <= lax.broadcasted_iota(I32, (c, c), 1)).astype(BF16)
    carry = jnp.zeros((r, 1), F32)
    for j in range(n // c):
        x = x_ref[:, j * c:(j + 1) * c]
        cs = carry
        for _ in range(3):
            piece = x.astype(BF16)
            cs = cs + jnp.dot(piece, tri, preferred_element_type=F32)
            x = x - piece.astype(F32)
        o_ref[:, j * c:(j + 1) * c] = cs
        carry = cs[:, c - 1:c]


def _cumsum_lanes(x):
    r, n = x.shape
    tr = _tile(r, 128, 8)
    return pl.pallas_call(
        _cumsum_kernel,
        grid=(r // tr,),
        in_specs=[pl.BlockSpec((tr, n), lambda i: (i, 0))],
        out_specs=pl.BlockSpec((tr, n), lambda i: (i, 0)),
        out_shape=jax.ShapeDtypeStruct((r, n), F32),
        compiler_params=_params("parallel"),
        name="cumsum",
    )(x)


def _attn_prompt_kernel(q_ref, k_ref, v_ref, f_ref, o_ref, *, tq, dh):
    seq_len = q_ref.shape[0]
    heads = q_ref.shape[1] // dh
    ones_col = (lax.broadcasted_iota(I32, (seq_len, dh), 1) == 0).astype(BF16)
    kb, vb, f = [], [], []
    for h in range(heads):
        cols = slice(h * dh, (h + 1) * dh)
        kb.append(k_ref[:, cols].astype(BF16))
        vb.append(jnp.concatenate([v_ref[:, cols].astype(BF16), ones_col], axis=1))
        f.append(f_ref[h] * LOG2_E)
    causal = lax.broadcasted_iota(I32, (tq, tq), 1) <= lax.broadcasted_iota(I32, (tq, tq), 0)
    for qi in range(seq_len // tq):
        lo = qi * tq
        for h in range(heads):
            cols = slice(h * dh, (h + 1) * dh)
            q = q_ref[lo:lo + tq, cols]
            sd = lax.dot_general(q, kb[h][lo:lo + tq], NT_DIMS, preferred_element_type=F32) - f[h][:, lo:lo + tq]
            sd = jnp.where(causal, sd, NEG_INF)
            m = jnp.max(sd, axis=-1, keepdims=True)
            if qi > 0:
                sp = lax.dot_general(q, kb[h][:lo], NT_DIMS, preferred_element_type=F32) - f[h][:, :lo]
                m = jnp.maximum(m, jnp.max(sp, axis=-1, keepdims=True))
                ol = jnp.dot(jnp.exp2(sp - m).astype(BF16), vb[h][:lo], preferred_element_type=F32)
            old = jnp.dot(jnp.exp2(sd - m).astype(BF16), vb[h][lo:lo + tq], preferred_element_type=F32)
            ol = ol + old if qi > 0 else old
            o_ref[lo:lo + tq, cols] = (ol[:, :dh] / ol[:, dh:dh + 1]).astype(o_ref.dtype)


ATTN_HEADS_PER_STEP = 4


def _attn_prompt(q, k, v, f, n_seq, seq_len, n_heads, dh):
    m = n_seq * seq_len
    tq = _tile(seq_len, 256, LANE)
    hps = ATTN_HEADS_PER_STEP if n_heads % ATTN_HEADS_PER_STEP == 0 else 1
    groups = n_heads // hps
    blk = pl.BlockSpec((seq_len, hps * dh), lambda b, g: (b, g))
    return pl.pallas_call(
        functools.partial(_attn_prompt_kernel, tq=tq, dh=dh),
        grid=(n_seq, groups),
        in_specs=[blk, blk, blk, pl.BlockSpec((hps, 1, seq_len), lambda b, g: (b * groups + g, 0, 0))],
        out_specs=blk,
        out_shape=jax.ShapeDtypeStruct((m, n_heads * dh), BF16),
        compiler_params=_params("parallel", "parallel"),
        name="attn_prompt",
    )(q, k, v, f)


def _attn_sample_kernel(q_ref, kc_ref, vc_ref, fp_ref, kn_ref, vn_ref, fn_ref, o_ref, m_sc, l_sc, acc_sc, *, n_heads, dh):
    kt = pl.program_id(1)
    lq = q_ref.shape[0]

    @pl.when(kt == 0)
    def _():
        m_sc[...] = jnp.full(m_sc.shape, NEG_INF, F32)
        l_sc[...] = jnp.zeros(l_sc.shape, F32)
        acc_sc[...] = jnp.zeros(acc_sc.shape, F32)

    q = q_ref[...]

    def update(k_heads, v_heads, fk, mask):
        s_heads = []
        for h in range(n_heads):
            s = lax.dot_general(q[:, h * dh:(h + 1) * dh], k_heads[h], NT_DIMS, preferred_element_type=F32)
            s = s - fk[h:h + 1, :]
            if mask is not None:
                s = jnp.where(mask, s, NEG_INF)
            s_heads.append(s)
        s = jnp.concatenate(s_heads, axis=0)
        m_prev = m_sc[...]
        m_new = jnp.maximum(m_prev, jnp.max(s, axis=-1, keepdims=True))
        alpha = jnp.exp2(m_prev - m_new)
        p = jnp.exp2(s - m_new)
        l_sc[...] = alpha * l_sc[...] + jnp.sum(p, axis=-1, keepdims=True)
        pb = p.astype(BF16)
        o_heads = [jnp.dot(pb[h * lq:(h + 1) * lq], v_heads[h], preferred_element_type=F32) for h in range(n_heads)]
        acc_sc[...] = alpha * acc_sc[...] + jnp.concatenate(o_heads, axis=0)
        m_sc[...] = m_new

    def cached_heads(ref):
        x = jnp.swapaxes(ref[0, 0], 0, 1).astype(BF16)
        return [x[h] for h in range(n_heads)]

    update(cached_heads(kc_ref), cached_heads(vc_ref), fp_ref[0] * LOG2_E, None)

    @pl.when(kt == pl.num_programs(1) - 1)
    def _():
        pad = jnp.zeros((LANE - lq, n_heads * dh), BF16)
        kn = jnp.concatenate([kn_ref[...].astype(BF16), pad], axis=0)
        vn = jnp.concatenate([vn_ref[...].astype(BF16), pad], axis=0)
        mask = lax.broadcasted_iota(I32, (lq, LANE), 1) <= lax.broadcasted_iota(I32, (lq, LANE), 0)
        heads = lambda x: [x[:, h * dh:(h + 1) * dh] for h in range(n_heads)]
        update(heads(kn), heads(vn), fn_ref[0] * LOG2_E, mask)
        out = acc_sc[...] / l_sc[...]
        o_ref[...] = jnp.concatenate([out[h * lq:(h + 1) * lq] for h in range(n_heads)], axis=1).astype(o_ref.dtype)


def _attn_sample(q, cache_k, cache_v, layer, k_new, v_new, f_all, n_seq, lq, past, n_heads, dh):
    da = n_heads * dh
    tk = _tile(past, 512, LANE)
    assert lq <= LANE and lq % BF16_ROWS == 0 and past % LANE == 0
    qblk = pl.BlockSpec((lq, da), lambda b, t: (b, 0))
    cblk = pl.BlockSpec((1, 1, tk, n_heads, dh), lambda b, t: (layer, b, t, 0, 0))
    return pl.pallas_call(
        functools.partial(_attn_sample_kernel, n_heads=n_heads, dh=dh),
        grid=(n_seq, past // tk),
        in_specs=[
            qblk,
            cblk,
            cblk,
            pl.BlockSpec((1, n_heads, tk), lambda b, t: (b, 0, t)),
            qblk,
            qblk,
            pl.BlockSpec((1, n_heads, LANE), lambda b, t: (b, 0, past // LANE)),
        ],
        out_specs=qblk,
        out_shape=jax.ShapeDtypeStruct((n_seq * lq, da), BF16),
        scratch_shapes=[
            pltpu.VMEM((n_heads * lq, 1), F32),
            pltpu.VMEM((n_heads * lq, 1), F32),
            pltpu.VMEM((n_heads * lq, dh), F32),
        ],
        compiler_params=_params("parallel", "arbitrary"),
        name="attn_sample",
    )(q, cache_k, cache_v, f_all, k_new, v_new, f_all)


def _merge_kernel(yc_ref, ya_ref, h_ref, wc_ref, wa_ref, wgc_ref, wga_ref, bgc_ref, bga_ref, o_ref):
    h = h_ref[...]
    a = jnp.dot(yc_ref[...], wc_ref[...], preferred_element_type=F32)
    b = jnp.dot(ya_ref[...], wa_ref[...], preferred_element_type=F32)
    gc = jax.nn.sigmoid(jnp.dot(h, wgc_ref[...], preferred_element_type=F32) + bgc_ref[...])
    ga = jax.nn.sigmoid(jnp.dot(h, wga_ref[...], preferred_element_type=F32) + bga_ref[...])
    o_ref[...] = (gc * a + ga * b).astype(o_ref.dtype)


def _merge(yc, ya, h, w_brc, w_bra, w_gate, b_gate):
    m, d = h.shape
    dc, da = yc.shape[1], ya.shape[1]
    tm = _tile(m, 1024, BF16_ROWS)
    tn = _tile(d, 512, LANE)
    nj = d // tn
    return pl.pallas_call(
        _merge_kernel,
        grid=(m // tm, nj),
        in_specs=[
            pl.BlockSpec((tm, dc), lambda i, j: (i, 0)),
            pl.BlockSpec((tm, da), lambda i, j: (i, 0)),
            pl.BlockSpec((tm, d), lambda i, j: (i, 0)),
            pl.BlockSpec((dc, tn), lambda i, j: (0, j)),
            pl.BlockSpec((da, tn), lambda i, j: (0, j)),
            pl.BlockSpec((d, tn), lambda i, j: (0, j)),
            pl.BlockSpec((d, tn), lambda i, j: (0, nj + j)),
            pl.BlockSpec((1, tn), lambda i, j: (0, j)),
            pl.BlockSpec((1, tn), lambda i, j: (0, nj + j)),
        ],
        out_specs=pl.BlockSpec((tm, tn), lambda i, j: (i, j)),
        out_shape=jax.ShapeDtypeStruct((m, d), BF16),
        compiler_params=_params("parallel", "parallel"),
        name="merge",
    )(yc, ya, h, w_brc, w_bra, w_gate, w_gate, b_gate, b_gate)


ROUTE_ID0, ROUTE_ID1, ROUTE_W0, ROUTE_W1 = 0, 1, 2, 3


def _wo_kernel(*refs, n_tiles_a, n_groups, per_group):
    a_refs, b_refs, (wo_ref, ng_ref, wr_ref, br_ref), outs = refs[0:5], refs[5:10], refs[10:14], refs[14:17]
    i = pl.program_id(0)

    @pl.when(i < n_tiles_a)
    def _():
        _wo_tile(*a_refs, wo_ref, ng_ref, wr_ref, br_ref, *outs, n_groups=n_groups, per_group=per_group)

    @pl.when(i >= n_tiles_a)
    def _():
        _wo_tile(*b_refs, wo_ref, ng_ref, wr_ref, br_ref, *outs, n_groups=n_groups, per_group=per_group)


def _wo_tile(mg_ref, x_ref, g1_ref, sh_ref, sc_ref, wo_ref, ng_ref, wr_ref, br_ref, x1_ref, h2_ref, rt_ref, *, n_groups, per_group):
    o = jnp.dot(mg_ref[...], wo_ref[...], preferred_element_type=F32)
    x1 = x_ref[...] + _bc(g1_ref) * o
    x1_ref[...] = x1
    h2 = _rms(x1, ng_ref[...]) * (1.0 + _bc(sc_ref)) + _bc(sh_ref)
    h2_ref[...] = h2
    wr = wr_ref[...]
    hi = h2.astype(BF16)
    lo = (h2 - hi.astype(F32)).astype(BF16)
    r_hi = jnp.dot(hi, wr, preferred_element_type=F32)
    r_lo = jnp.dot(lo, wr[:, :LANE], preferred_element_type=F32)
    lg = r_hi[:, :LANE] + (r_hi[:, LANE:] + r_lo) + br_ref[...]
    col = lax.broadcasted_iota(I32, lg.shape, 1).astype(F32)
    big = float(LANE)
    gl = jnp.where(col < n_groups, lg, NEG_INF)
    gmax = jnp.max(gl, axis=-1, keepdims=True)
    g_val = 1.0 / jnp.sum(jnp.exp(gl - gmax), axis=-1, keepdims=True)
    g_idx = jnp.min(jnp.where(gl == gmax, col, big), axis=-1, keepdims=True)
    lo = n_groups + g_idx * per_group
    el = jnp.where((col >= lo) & (col < lo + per_group), lg, NEG_INF)
    m1 = jnp.max(el, axis=-1, keepdims=True)
    i1 = jnp.min(jnp.where(el == m1, col, big), axis=-1, keepdims=True)
    el2 = jnp.where(col == i1, NEG_INF, el)
    m2 = jnp.max(el2, axis=-1, keepdims=True)
    i2 = jnp.min(jnp.where(el2 == m2, col, big), axis=-1, keepdims=True)
    r = jnp.exp(m2 - m1)
    w0 = g_val / (1.0 + r)
    w1 = g_val * r / (1.0 + r)
    rt = jnp.where(col == ROUTE_ID0, i1 - n_groups, 0.0)
    rt = jnp.where(col == ROUTE_ID1, i2 - n_groups, rt)
    rt = jnp.where(col == ROUTE_W0, w0, rt)
    rt = jnp.where(col == ROUTE_W1, w1, rt)
    rt_ref[...] = rt


def _wo(path_a, path_b, w_o, norm_g, w_r, b_r, n_groups, per_group):
    d = w_o.shape[0]
    ma, mb = path_a[1].shape[0], path_b[1].shape[0]
    tm = _tile(min(ma, mb), 256, BF16_ROWS)
    assert ma % tm == 0 and mb % tm == 0
    na, nb = ma // tm, mb // tm
    tiles = (lambda i: jnp.minimum(i, na - 1), lambda i: jnp.maximum(i - na, 0))
    in_specs, args = [], []
    for (merged, x, gate1, shift2, scale2, seq_len), tile in zip((path_a, path_b), tiles):
        row = pl.BlockSpec((tm, d), lambda i, tile=tile: (tile(i), 0))
        in_specs += [row, row]
        args += [merged, x]
        for vec in (gate1, shift2, scale2):
            arr, spec = _rowvec(vec, seq_len, tm, tile)
            in_specs.append(spec)
            args.append(arr)
    const = lambda shape: pl.BlockSpec(shape, lambda i: (0, 0))
    in_specs += [const((d, d)), const((1, d)), const((d, 2 * LANE)), const((1, LANE))]
    args += [w_o, norm_g.reshape(1, d), w_r, b_r]
    row = pl.BlockSpec((tm, d), lambda i: (i, 0))
    return pl.pallas_call(
        functools.partial(_wo_kernel, n_tiles_a=na, n_groups=n_groups, per_group=per_group),
        grid=(na + nb,),
        in_specs=in_specs,
        out_specs=[row, row, pl.BlockSpec((tm, LANE), lambda i: (i, 0))],
        out_shape=[
            jax.ShapeDtypeStruct((ma + mb, d), F32),
            jax.ShapeDtypeStruct((ma + mb, d), F32),
            jax.ShapeDtypeStruct((ma + mb, LANE), F32),
        ],
        compiler_params=_params("arbitrary"),
        name="wo_norm_router",
    )(*args)


RING_SLOTS = 3
ROW_COPY_PRIORITY = 1


def _gather_start(idx_ref, src_hbm, dst, slot, sem, n_rows, first=0):
    for r in range(first, first + n_rows):
        pltpu.make_async_copy(src_hbm.at[pl.ds(idx_ref[0, 0, r], 1)], dst.at[slot, pl.ds(r, 1)], sem.at[slot]).start(
            priority=ROW_COPY_PRIORITY)


def _gather_wait(src_hbm, dst, slot, sem, n_rows):
    pltpu.make_async_copy(src_hbm.at[pl.ds(0, n_rows)], dst.at[slot], sem.at[slot]).wait()


def _expert_kernel(te_ref, tv_ref, tok0_ref, tok1_ref, tokn_ref, h2_hbm, w1_ref, w3_ref, w2_ref, o_ref, xbuf, w1b, w3b, w2b, sem):
    i = pl.program_id(0)
    tm = xbuf.shape[1]
    slot = lax.rem(i, RING_SLOTS)
    valid = tv_ref[i] == 1
    started = jnp.where(i < 2, tv_ref[0], tv_ref[jnp.maximum(i - 2, 0)]) == 1

    @pl.when(jnp.logical_and(i == 0, valid))
    def _():
        _gather_start(tok0_ref, h2_hbm, xbuf, 0, sem, tm)
        _gather_start(tok1_ref, h2_hbm, xbuf, 1, sem, tm)

    @pl.when(valid)
    def _():
        _gather_wait(h2_hbm, xbuf, slot, sem, tm)

        @pl.when(jnp.logical_or(i == 0, te_ref[i] != te_ref[jnp.maximum(i - 1, 0)]))
        def _():
            w1b[...] = w1_ref[0].astype(BF16)
            w3b[...] = w3_ref[0].astype(BF16)
            w2b[...] = w2_ref[0].astype(BF16)

        nslot = lax.rem(i + 2, RING_SLOTS)
        x = xbuf[slot].astype(BF16)
        a = jnp.dot(x, w1b[...], preferred_element_type=F32)
        _gather_start(tokn_ref, h2_hbm, xbuf, nslot, sem, tm // 2)
        b = jnp.dot(x, w3b[...], preferred_element_type=F32)
        _gather_start(tokn_ref, h2_hbm, xbuf, nslot, sem, tm // 2, first=tm // 2)
        hm = (jax.nn.silu(a) * b).astype(BF16)
        o_ref[...] = jnp.dot(hm, w2b[...], preferred_element_type=F32)

    @pl.when(jnp.logical_not(valid))
    def _():
        @pl.when(started)
        def _():
            _gather_wait(h2_hbm, xbuf, slot, sem, tm)

        o_ref[...] = jnp.zeros(o_ref.shape, F32)


def _experts(h2_all, tile_expert, tile_valid, tok_sorted, w1, w3, w2, tm):
    n_tiles = tile_expert.shape[0]
    _, d, de = w1.shape
    tok3 = tok_sorted.reshape(n_tiles, 1, tm)
    assert n_tiles >= RING_SLOTS
    idx_spec = lambda tile: pl.BlockSpec((1, 1, tm), lambda i, te, tv: (tile(i), 0, 0), memory_space=pltpu.SMEM)
    grid_spec = pltpu.PrefetchScalarGridSpec(
        num_scalar_prefetch=2,
        grid=(n_tiles,),
        in_specs=[
            idx_spec(lambda i: 0),
            idx_spec(lambda i: 1),
            idx_spec(lambda i: jnp.minimum(i + 2, n_tiles - 1)),
            pl.BlockSpec(memory_space=pl.ANY),
            pl.BlockSpec((1, d, de), lambda i, te, tv: (te[i], 0, 0)),
            pl.BlockSpec((1, d, de), lambda i, te, tv: (te[i], 0, 0)),
            pl.BlockSpec((1, de, d), lambda i, te, tv: (te[i], 0, 0)),
        ],
        out_specs=pl.BlockSpec((tm, d), lambda i, te, tv: (i, 0)),
        scratch_shapes=[
            pltpu.VMEM((RING_SLOTS, tm, d), F32),
            pltpu.VMEM((d, de), BF16),
            pltpu.VMEM((d, de), BF16),
            pltpu.VMEM((de, d), BF16),
            pltpu.SemaphoreType.DMA((RING_SLOTS,)),
        ],
    )
    return pl.pallas_call(
        _expert_kernel,
        grid_spec=grid_spec,
        out_shape=jax.ShapeDtypeStruct((n_tiles * tm, d), F32),
        compiler_params=_params("arbitrary"),
        name="experts",
    )(tile_expert, tile_valid, tok3, tok3, tok3, h2_all, w1, w3, w2)


COMBINE_CHUNKS = 4


def _combine_kernel(d_ref, dn_ref, ys_hbm, x1_ref, rt_ref, g2_ref, ng_ref, y_ref, ybuf, sem):
    i = pl.program_id(0)
    tm = x1_ref.shape[0]
    slot = lax.rem(i, 2)

    @pl.when(i == 0)
    def _():
        _gather_start(d_ref, ys_hbm, ybuf, 0, sem, 2 * tm)

    _gather_wait(ys_hbm, ybuf, slot, sem, 2 * tm)
    g2 = _bc(g2_ref)
    ng = ng_ref[...]
    ck = tm // COMBINE_CHUNKS
    for c in range(COMBINE_CHUNKS):
        _gather_start(dn_ref, ys_hbm, ybuf, 1 - slot, sem, 2 * ck, first=2 * ck * c)
        rows = slice(c * ck, (c + 1) * ck)
        rt = rt_ref[rows, :]
        y0 = ybuf[slot, c * ck:(c + 1) * ck]
        y1 = ybuf[slot, tm + c * ck:tm + (c + 1) * ck]
        moe = rt[:, ROUTE_W0:ROUTE_W0 + 1] * y0 + rt[:, ROUTE_W1:ROUTE_W1 + 1] * y1
        g2c = g2 if g2.shape[0] == 1 else g2[rows]
        y_ref[rows, :] = _rms(x1_ref[rows, :] + g2c * moe, ng)

    @pl.when(i == pl.num_programs(0) - 1)
    def _():
        _gather_wait(ys_hbm, ybuf, 1 - slot, sem, 2 * tm)


def _combine(ys, dest, x1_all, route_all, row_off, m, gate2, normf_g, seq_len):
    d = x1_all.shape[1]
    tm = _tile(seq_len, 256, 8) if seq_len >= 256 else _tile(m, 256, seq_len)
    assert row_off % tm == 0 and tm % COMBINE_CHUNKS == 0
    nb, off = m // tm, row_off // tm
    d3 = dest.reshape(nb, tm, 2).transpose(0, 2, 1).reshape(nb, 1, 2 * tm)
    d3 = jnp.concatenate([d3, jnp.zeros((1, 1, 2 * tm), I32)], axis=0)
    g2, g2_spec = _rowvec(gate2, seq_len, tm)
    return pl.pallas_call(
        _combine_kernel,
        grid=(nb,),
        in_specs=[
            pl.BlockSpec((1, 1, 2 * tm), lambda i: (i, 0, 0), memory_space=pltpu.SMEM),
            pl.BlockSpec((1, 1, 2 * tm), lambda i: (i + 1, 0, 0), memory_space=pltpu.SMEM),
            pl.BlockSpec(memory_space=pl.ANY),
            pl.BlockSpec((tm, d), lambda i: (off + i, 0)),
            pl.BlockSpec((tm, LANE), lambda i: (off + i, 0)),
            g2_spec,
            pl.BlockSpec((1, d), lambda i: (0, 0)),
        ],
        out_specs=pl.BlockSpec((tm, d), lambda i: (i, 0)),
        out_shape=jax.ShapeDtypeStruct((m, d), F32),
        scratch_shapes=[pltpu.VMEM((2, 2 * tm, d), F32), pltpu.SemaphoreType.DMA((2,))],
        compiler_params=_params("arbitrary"),
        name="combine_final_norm",
    )(d3, d3, ys, x1_all, route_all, g2, normf_g.reshape(1, d))


def _route_plan(route, n_experts, tm):
    t = route.shape[0]
    s = TOP_K * t
    e_flat = route[:, ROUTE_ID0:ROUTE_ID1 + 1].astype(I32).reshape(s)
    onehot = (e_flat[:, None] == jnp.arange(n_experts, dtype=I32)[None, :]).astype(I32)
    csum = jnp.cumsum(onehot, axis=0)
    rank = jnp.take_along_axis(csum, e_flat[:, None], axis=1)[:, 0] - 1
    counts = csum[-1]
    padded = (counts + tm - 1) // tm * tm
    ends = jnp.cumsum(padded)
    dest = (ends - padded)[e_flat] + rank
    n_tiles = pl.cdiv(s, tm) + n_experts + 2
    tile_start = jnp.arange(n_tiles, dtype=I32) * tm
    tile_valid = (tile_start < ends[-1]).astype(I32)
    te = jnp.minimum(jnp.sum((ends[None, :] <= tile_start[:, None]).astype(I32), axis=1), n_experts - 1)
    last_used = jnp.maximum(ends[-1] // tm - 1, 0)
    tile_expert = jnp.where(tile_valid == 1, te, te[last_used])
    tok_sorted = jnp.zeros((n_tiles * tm,), I32).at[dest].set(
        jnp.arange(s, dtype=I32) // TOP_K, unique_indices=True, mode="promise_in_bounds")
    return dest.reshape(t, TOP_K), tok_sorted, tile_expert, tile_valid


def kernel(x_prompt, x_sample, cache_k, cache_v, cache_logf, state_conv, c_prompt, c_sample, w_ada, b_ada, norm1_g, norm2_g, normf_g, w_in, conv_w, conv_b, b_forget, w_br_conv, w_br_attn, w_gate, b_gate, w_o, w_rg, b_rg, w_re, b_re, w1, w3, w2):
    depth = w_ada.shape[0]
    assert depth == 1, "single-layer step"
    l = 0
    bp, sp, d = x_prompt.shape
    bs, ss, _ = x_sample.shape
    _, _, past, n_heads, dh = cache_k.shape
    d_conv = state_conv.shape[-1]
    da = n_heads * dh
    n_groups = w_rg.shape[-1]
    n_experts = w_re.shape[-1]
    per_group = n_experts // n_groups
    assert n_groups + n_experts <= LANE
    scale = LOG2_E * dh ** -0.5

    w_in_bf = w_in[l].astype(BF16)
    q_off = 3 * d_conv
    w_q, w_k, w_v = (w_in_bf[:, q_off + j * da:q_off + (j + 1) * da] for j in range(3))
    hp = _round_up(n_heads, BF16_ROWS)
    wf_t = jnp.zeros((hp, d), BF16).at[:n_heads].set(w_in[l][:, 3 * d_conv + 3 * da:].T.astype(BF16))
    b_f = jnp.zeros((hp, 1), F32).at[:n_heads, 0].set(b_forget[l])
    w_brc_bf = w_br_conv[l].astype(BF16)
    w_bra_bf = w_br_attn[l].astype(BF16)
    w_gate_bf = w_gate[l].astype(BF16)
    b_gate2 = b_gate[l].reshape(1, 2 * d)
    w_o_bf = w_o[l].astype(BF16)
    w_r32 = jnp.pad(jnp.concatenate([w_rg[l], w_re[l]], axis=1), ((0, 0), (0, LANE - n_groups - n_experts)))
    w_r_hi = w_r32.astype(BF16)
    w_r = jnp.concatenate([w_r_hi, (w_r32 - w_r_hi.astype(F32)).astype(BF16)], axis=1)
    b_r = jnp.zeros((1, LANE), F32).at[0, :n_groups].set(b_rg[l]).at[0, n_groups:n_groups + n_experts].set(b_re[l])

    mod = _ada(jnp.concatenate([c_prompt, c_sample], axis=0), w_ada[l], b_ada[l])
    mods = {"p": jnp.split(mod[:bp], 6, axis=-1), "s": jnp.split(mod[bp:], 6, axis=-1)}

    def mixer_inputs(x, mod6, seq_len, prev):
        shift1, scale1 = mod6[0], mod6[1]
        h = _norm(x, norm1_g[l], shift1, scale1, seq_len)
        y_conv, tail = _conv_branch(h, w_in_bf, conv_w[l], conv_b[l], prev, seq_len, d_conv)
        q = _proj(h, w_q, BF16, scale)
        k = _proj(h, w_k, F32)
        v = _proj(h, w_v, F32)
        lf_t = _forget(h, wf_t, b_f)[:n_heads]
        return h, y_conv, tail, q, k, v, lf_t

    xp = x_prompt.reshape(bp * sp, d)
    xs = x_sample.reshape(bs * ss, d)
    hp_, ycp, tailp, qp, kp, vp, lftp = mixer_inputs(xp, mods["p"], sp, None)
    hs_, ycs, tails, qs, ks, vs, lfts = mixer_inputs(xs, mods["s"], ss, state_conv[l])

    lf_p = lftp.reshape(n_heads, bp, sp).transpose(1, 0, 2).reshape(bp * n_heads, sp)
    f_p = _cumsum_lanes(lf_p).reshape(bp * n_heads, 1, sp)
    yap = _attn_prompt(qp, kp, vp, f_p, bp, sp, n_heads, dh)

    lf_new = lfts.reshape(n_heads, bs, ss).transpose(1, 0, 2)
    lf_past = cache_logf[l].astype(F32).transpose(0, 2, 1)
    tot = _round_up(past + LANE, CUMSUM_CHUNK)
    lf_all = jnp.concatenate([lf_past, lf_new, jnp.zeros((bs, n_heads, tot - past - ss), F32)], axis=-1)
    f_s = _cumsum_lanes(lf_all.reshape(bs * n_heads, tot)).reshape(bs, n_heads, tot)
    yas = _attn_sample(qs, cache_k, cache_v, l, ks, vs, f_s, bs, ss, past, n_heads, dh)

    def wo_path(x, mod6, seq_len, h, yc, ya):
        gate1, shift2, scale2 = mod6[2], mod6[3], mod6[4]
        merged = _merge(yc, ya, h, w_brc_bf, w_bra_bf, w_gate_bf, b_gate2)
        return merged, x, gate1, shift2, scale2, seq_len

    x1_all, h2_all, route_all = _wo(
        wo_path(xp, mods["p"], sp, hp_, ycp, yap), wo_path(xs, mods["s"], ss, hs_, ycs, yas),
        w_o_bf, norm2_g[l], w_r, b_r, n_groups, per_group)
    tm_e = 256 if TOP_K * h2_all.shape[0] >= 8192 else 64
    dest, tok_sorted, tile_expert, tile_valid = _route_plan(route_all, n_experts, tm_e)
    ys = _experts(h2_all, tile_expert, tile_valid, tok_sorted, w1[l], w3[l], w2[l], tm_e)

    mp, ms = bp * sp, bs * ss
    y_p = _combine(ys, dest[:mp], x1_all, route_all, 0, mp, mods["p"][5], normf_g, sp)
    y_s = _combine(ys, dest[mp:], x1_all, route_all, mp, ms, mods["s"][5], normf_g, ss)

    def cache_outs(k, v, lf_t, tail, n_seq, seq_len):
        return (
            k.reshape(1, n_seq, seq_len, n_heads, dh),
            v.reshape(1, n_seq, seq_len, n_heads, dh),
            lf_t.T.reshape(1, n_seq, seq_len, n_heads),
            tail[:, CONV_TAIL_ROWS - 2:, :].reshape(1, n_seq, 2, d_conv),
        )

    return (y_p.reshape(bp, sp, d), y_s.reshape(bs, ss, d)) + cache_outs(kp, vp, lftp, tailp, bp, sp) + cache_outs(ks, vs, lfts, tails, bs, ss)
```

```python
import functools

import jax
import jax.numpy as jnp
from jax import lax
from jax.experimental import pallas as pl
from jax.experimental.pallas import tpu as pltpu

F32 = jnp.float32
BF16 = jnp.bfloat16
I32 = jnp.int32

EPS = 1e-6
NEG_INF = -1e30
LOG2_E = 1.4426950408889634
TOP_K = 2
LANE = 128
BF16_ROWS = 16
VMEM_LIMIT_BYTES = 56 * 1024 * 1024
NT_DIMS = (((1,), (1,)), ((), ()))


def _tile(n, pref, align):
    if n <= pref:
        return n
    t = (pref // align) * align
    while t >= align:
        if n % t == 0:
            return t
        t -= align
    return n


def _round_up(n, m):
    return (n + m - 1) // m * m


def _params(*sem):
    return pltpu.CompilerParams(dimension_semantics=sem, vmem_limit_bytes=VMEM_LIMIT_BYTES)


def _rowvec(vec, seq_len, tm, tile=lambda i: i):
    n_seq, d = vec.shape
    if tm <= seq_len:
        per = seq_len // tm
        return vec.reshape(n_seq, 1, d), pl.BlockSpec((1, 1, d), lambda i, *_: (tile(i) // per, 0, 0))
    return jnp.repeat(vec, seq_len, axis=0), pl.BlockSpec((tm, d), lambda i, *_: (tile(i), 0))


def _bc(ref):
    v = ref[...]
    return v.reshape(v.shape[-2], v.shape[-1])


def _rows_to_slabs(x):
    c = x.shape[1] // LANE
    return jnp.swapaxes(jnp.stack([x[:, j * LANE:(j + 1) * LANE] for j in range(c)], axis=0), 0, 1)


def _slabs_to_rows(x3):
    xt = jnp.swapaxes(x3, 0, 1)
    return jnp.concatenate([xt[j] for j in range(xt.shape[0])], axis=1)


def _rms(x, g):
    return x * lax.rsqrt(jnp.mean(x * x, axis=-1, keepdims=True) + EPS) * g


def _ada_kernel(c_ref, w_ref, b_ref, o_ref):
    o_ref[...] = jnp.dot(c_ref[...].astype(BF16), w_ref[...].astype(BF16), preferred_element_type=F32) + b_ref[...]


def _ada(c_all, w, b):
    r, d = c_all.shape
    n = w.shape[1]
    tn = _tile(n, 1024, LANE)
    return pl.pallas_call(
        _ada_kernel,
        grid=(n // tn,),
        in_specs=[
            pl.BlockSpec((r, d), lambda j: (0, 0)),
            pl.BlockSpec((d, tn), lambda j: (0, j)),
            pl.BlockSpec((1, tn), lambda j: (0, j)),
        ],
        out_specs=pl.BlockSpec((r, tn), lambda j: (0, j)),
        out_shape=jax.ShapeDtypeStruct((r, n), F32),
        compiler_params=_params("parallel"),
        name="ada",
    )(c_all, w, b.reshape(1, n))


def _norm_kernel(x_ref, g_ref, sh_ref, sc_ref, h_ref):
    h = _rms(x_ref[...], g_ref[...]) * (1.0 + _bc(sc_ref)) + _bc(sh_ref)
    h_ref[...] = h.astype(h_ref.dtype)


def _norm(x, g, shift, scale, seq_len):
    m, d = x.shape
    tm = _tile(seq_len, 512, 8)
    sh, sh_spec = _rowvec(shift, seq_len, tm)
    sc, sc_spec = _rowvec(scale, seq_len, tm)
    return pl.pallas_call(
        _norm_kernel,
        grid=(m // tm,),
        in_specs=[pl.BlockSpec((tm, d), lambda i: (i, 0)), pl.BlockSpec((1, d), lambda i: (0, 0)), sh_spec, sc_spec],
        out_specs=pl.BlockSpec((tm, d), lambda i: (i, 0)),
        out_shape=jax.ShapeDtypeStruct((m, d), BF16),
        compiler_params=_params("parallel"),
        name="norm",
    )(x, g.reshape(1, d), sh, sc)


def _proj_kernel(a_ref, w_ref, o_ref, *, scale):
    acc = jnp.dot(a_ref[...], w_ref[...], preferred_element_type=F32)
    if scale != 1.0:
        acc = acc * scale
    o_ref[...] = acc.astype(o_ref.dtype)


RESIDENT_WEIGHT_BYTES = 8 * 1024 * 1024


def _proj(a, w, out_dtype, scale=1.0):
    m, k = a.shape
    n = w.shape[1]
    if k * n * w.dtype.itemsize <= RESIDENT_WEIGHT_BYTES:
        tm, tn = _tile(m, 512, BF16_ROWS), n
    else:
        tm, tn = _tile(m, 1024, BF16_ROWS), _tile(n, 512, LANE)
    return pl.pallas_call(
        functools.partial(_proj_kernel, scale=scale),
        grid=(m // tm, n // tn),
        in_specs=[pl.BlockSpec((tm, k), lambda i, j: (i, 0)), pl.BlockSpec((k, tn), lambda i, j: (0, j))],
        out_specs=pl.BlockSpec((tm, tn), lambda i, j: (i, j)),
        out_shape=jax.ShapeDtypeStruct((m, n), out_dtype),
        compiler_params=_params("parallel", "parallel"),
        name="proj",
    )(a, w)


CONV_TAIL_ROWS = 8


def _conv_kernel(*refs, seq_len, has_prev):
    if has_prev:
        h_ref, wb_ref, wc_ref, wv_ref, cw_ref, cb_ref, p1_ref, p2_ref, y_ref, tail_ref = refs
    else:
        h_ref, wb_ref, wc_ref, wv_ref, cw_ref, cb_ref, y_ref, tail_ref = refs
    h = h_ref[...]
    bg = jnp.dot(h, wb_ref[...], preferred_element_type=F32)
    cg = jnp.dot(h, wc_ref[...], preferred_element_type=F32)
    vc = jnp.dot(h, wv_ref[...], preferred_element_type=F32)
    u = cg * vc
    tm = u.shape[0]
    t = lax.broadcasted_iota(I32, (tm, 1), 0)
    if tm > seq_len:
        t = jnp.bitwise_and(t, seq_len - 1) if seq_len & (seq_len - 1) == 0 else lax.rem(t, seq_len)
    u1 = jnp.where(t >= 1, pltpu.roll(u, 1, 0), p1_ref[...] if has_prev else 0.0)
    u2 = jnp.where(t >= 2, pltpu.roll(u, 2, 0), p2_ref[...] if has_prev else 0.0)
    cw = cw_ref[...]
    conv = cb_ref[...] + ((cw[0:1] * u2 + cw[1:2] * u1) + cw[2:3] * u)
    y_ref[...] = (bg * conv).astype(y_ref.dtype)
    for s in range(tm // seq_len):
        tail_ref[s] = u[(s + 1) * seq_len - CONV_TAIL_ROWS:(s + 1) * seq_len, :]


def _conv_branch(h, w_in_bf, conv_w, conv_b, prev, seq_len, d_conv):
    m, d = h.shape
    assert conv_w.shape[0] == 3 and seq_len >= CONV_TAIL_ROWS
    tm = seq_len if seq_len >= 512 else m
    assert m % tm == 0 and tm % seq_len == 0
    tc = _tile(d_conv, 256, LANE)
    nc = d_conv // tc
    n_seq = m // seq_len
    seq_per_tile = tm // seq_len
    in_specs = [
        pl.BlockSpec((tm, d), lambda i, j: (i, 0)),
        pl.BlockSpec((d, tc), lambda i, j: (0, j)),
        pl.BlockSpec((d, tc), lambda i, j: (0, nc + j)),
        pl.BlockSpec((d, tc), lambda i, j: (0, 2 * nc + j)),
        pl.BlockSpec((3, tc), lambda i, j: (0, j)),
        pl.BlockSpec((1, tc), lambda i, j: (0, j)),
    ]
    args = [h, w_in_bf, w_in_bf, w_in_bf, conv_w, conv_b.reshape(1, d_conv)]
    if prev is not None:
        zeros = jnp.zeros((n_seq, seq_len, d_conv), F32)
        p1 = zeros.at[:, 0].set(prev[:, 1]).reshape(m, d_conv)
        p2 = zeros.at[:, 0].set(prev[:, 0]).at[:, 1].set(prev[:, 1]).reshape(m, d_conv)
        in_specs += [pl.BlockSpec((tm, tc), lambda i, j: (i, j))] * 2
        args += [p1, p2]
    return pl.pallas_call(
        functools.partial(_conv_kernel, seq_len=seq_len, has_prev=prev is not None),
        grid=(m // tm, nc),
        in_specs=in_specs,
        out_specs=[
            pl.BlockSpec((tm, tc), lambda i, j: (i, j)),
            pl.BlockSpec((seq_per_tile, CONV_TAIL_ROWS, tc), lambda i, j: (i, 0, j)),
        ],
        out_shape=[
            jax.ShapeDtypeStruct((m, d_conv), BF16),
            jax.ShapeDtypeStruct((n_seq, CONV_TAIL_ROWS, d_conv), F32),
        ],
        compiler_params=_params("parallel", "parallel"),
        name="conv_branch",
    )(*args)


def _forget_kernel(wf_ref, h_ref, b_ref, lf_ref):
    fl = lax.dot_general(wf_ref[...], h_ref[...], NT_DIMS, preferred_element_type=F32)
    lf_ref[...] = jax.nn.log_sigmoid(fl + b_ref[...])


def _forget(h, wf_t, b_f):
    m, d = h.shape
    hp = wf_t.shape[0]
    tm = _tile(m, 1024, LANE)
    return pl.pallas_call(
        _forget_kernel,
        grid=(m // tm,),
        in_specs=[
            pl.BlockSpec((hp, d), lambda i: (0, 0)),
            pl.BlockSpec((tm, d), lambda i: (i, 0)),
            pl.BlockSpec((hp, 1), lambda i: (0, 0)),
        ],
        out_specs=pl.BlockSpec((hp, tm), lambda i: (0, i)),
        out_shape=jax.ShapeDtypeStruct((hp, m), F32),
        compiler_params=_params("parallel"),
        name="forget",
    )(wf_t, h, b_f)


CUMSUM_CHUNK = LANE


def _cumsum_kernel(x_ref, o_ref):
    r, n = x_ref.shape
    c = CUMSUM_CHUNK
    tri = (lax.broadcasted_iota(I32, (c, c), 0) <= lax.broadcasted_iota(I32, (c, c), 1)).astype(BF16)
    carry = jnp.zeros((r, 1), F32)
    for j in range(n // c):
        x = x_ref[:, j * c:(j + 1) * c]
        cs = carry
        for _ in range(3):
            piece = x.astype(BF16)
            cs = cs + jnp.dot(piece, tri, preferred_element_type=F32)
            x = x - piece.astype(F32)
        o_ref[:, j * c:(j + 1) * c] = cs
        carry = cs[:, c - 1:c]


def _cumsum_lanes(x):
    r, n = x.shape
    tr = _tile(r, 128, 8)
    return pl.pallas_call(
        _cumsum_kernel,
        grid=(r // tr,),
        in_specs=[pl.BlockSpec((tr, n), lambda i: (i, 0))],
        out_specs=pl.BlockSpec((tr, n), lambda i: (i, 0)),
        out_shape=jax.ShapeDtypeStruct((r, n), F32),
        compiler_params=_params("parallel"),
        name="cumsum",
    )(x)


def _attn_prompt_kernel(q_ref, k_ref, v_ref, f_ref, o_ref, *, tq, dh):
    seq_len = q_ref.shape[0]
    heads = q_ref.shape[1] // dh
    ones_col = (lax.broadcasted_iota(I32, (seq_len, dh), 1) == 0).astype(BF16)
    kb, vb, f = [], [], []
    for h in range(heads):
        cols = slice(h * dh, (h + 1) * dh)
        kb.append(k_ref[:, cols].astype(BF16))
        vb.append(jnp.concatenate([v_ref[:, cols].astype(BF16), ones_col], axis=1))
        f.append(f_ref[h] * LOG2_E)
    causal = lax.broadcasted_iota(I32, (tq, tq), 1) <= lax.broadcasted_iota(I32, (tq, tq), 0)
    for qi in range(seq_len // tq):
        lo = qi * tq
        for h in range(heads):
            cols = slice(h * dh, (h + 1) * dh)
            q = q_ref[lo:lo + tq, cols]
            sd = lax.dot_general(q, kb[h][lo:lo + tq], NT_DIMS, preferred_element_type=F32) - f[h][:, lo:lo + tq]
            sd = jnp.where(causal, sd, NEG_INF)
            m = jnp.max(sd, axis=-1, keepdims=True)
            if qi > 0:
                sp = lax.dot_general(q, kb[h][:lo], NT_DIMS, preferred_element_type=F32) - f[h][:, :lo]
                m = jnp.maximum(m, jnp.max(sp, axis=-1, keepdims=True))
                ol = jnp.dot(jnp.exp2(sp - m).astype(BF16), vb[h][:lo], preferred_element_type=F32)
            old = jnp.dot(jnp.exp2(sd - m).astype(BF16), vb[h][lo:lo + tq], preferred_element_type=F32)
            ol = ol + old if qi > 0 else old
            o_ref[lo:lo + tq, cols] = (ol[:, :dh] / ol[:, dh:dh + 1]).astype(o_ref.dtype)


ATTN_HEADS_PER_STEP = 4


def _attn_prompt(q, k, v, f, n_seq, seq_len, n_heads, dh):
    m = n_seq * seq_len
    tq = _tile(seq_len, 256, LANE)
    hps = ATTN_HEADS_PER_STEP if n_heads % ATTN_HEADS_PER_STEP == 0 else 1
    groups = n_heads // hps
    blk = pl.BlockSpec((seq_len, hps * dh), lambda b, g: (b, g))
    return pl.pallas_call(
        functools.partial(_attn_prompt_kernel, tq=tq, dh=dh),
        grid=(n_seq, groups),
        in_specs=[blk, blk, blk, pl.BlockSpec((hps, 1, seq_len), lambda b, g: (b * groups + g, 0, 0))],
        out_specs=blk,
        out_shape=jax.ShapeDtypeStruct((m, n_heads * dh), BF16),
        compiler_params=_params("parallel", "parallel"),
        name="attn_prompt",
    )(q, k, v, f)


def _attn_sample_kernel(q_ref, kc_ref, vc_ref, fp_ref, kn_ref, vn_ref, fn_ref, o_ref, m_sc, l_sc, acc_sc, *, n_heads, dh):
    kt = pl.program_id(1)
    lq = q_ref.shape[0]

    @pl.when(kt == 0)
    def _():
        m_sc[...] = jnp.full(m_sc.shape, NEG_INF, F32)
        l_sc[...] = jnp.zeros(l_sc.shape, F32)
        acc_sc[...] = jnp.zeros(acc_sc.shape, F32)

    q = q_ref[...]

    def update(k_heads, v_heads, fk, mask):
        s_heads = []
        for h in range(n_heads):
            s = lax.dot_general(q[:, h * dh:(h + 1) * dh], k_heads[h], NT_DIMS, preferred_element_type=F32)
            s = s - fk[h:h + 1, :]
            if mask is not None:
                s = jnp.where(mask, s, NEG_INF)
            s_heads.append(s)
        s = jnp.concatenate(s_heads, axis=0)
        m_prev = m_sc[...]
        m_new = jnp.maximum(m_prev, jnp.max(s, axis=-1, keepdims=True))
        alpha = jnp.exp2(m_prev - m_new)
        p = jnp.exp2(s - m_new)
        l_sc[...] = alpha * l_sc[...] + jnp.sum(p, axis=-1, keepdims=True)
        pb = p.astype(BF16)
        o_heads = [jnp.dot(pb[h * lq:(h + 1) * lq], v_heads[h], preferred_element_type=F32) for h in range(n_heads)]
        acc_sc[...] = alpha * acc_sc[...] + jnp.concatenate(o_heads, axis=0)
        m_sc[...] = m_new

    def cached_heads(ref):
        x = jnp.swapaxes(ref[0, 0], 0, 1).astype(BF16)
        return [x[h] for h in range(n_heads)]

    update(cached_heads(kc_ref), cached_heads(vc_ref), fp_ref[0] * LOG2_E, None)

    @pl.when(kt == pl.num_programs(1) - 1)
    def _():
        pad = jnp.zeros((LANE - lq, n_heads * dh), BF16)
        kn = jnp.concatenate([kn_ref[...].astype(BF16), pad], axis=0)
        vn = jnp.concatenate([vn_ref[...].astype(BF16), pad], axis=0)
        mask = lax.broadcasted_iota(I32, (lq, LANE), 1) <= lax.broadcasted_iota(I32, (lq, LANE), 0)
        heads = lambda x: [x[:, h * dh:(h + 1) * dh] for h in range(n_heads)]
        update(heads(kn), heads(vn), fn_ref[0] * LOG2_E, mask)
        out = acc_sc[...] / l_sc[...]
        o_ref[...] = jnp.concatenate([out[h * lq:(h + 1) * lq] for h in range(n_heads)], axis=1).astype(o_ref.dtype)


def _attn_sample(q, cache_k, cache_v, layer, k_new, v_new, f_all, n_seq, lq, past, n_heads, dh):
    da = n_heads * dh
    tk = _tile(past, 512, LANE)
    assert lq <= LANE and lq % BF16_ROWS == 0 and past % LANE == 0
    qblk = pl.BlockSpec((lq, da), lambda b, t: (b, 0))
    cblk = pl.BlockSpec((1, 1, tk, n_heads, dh), lambda b, t: (layer, b, t, 0, 0))
    return pl.pallas_call(
        functools.partial(_attn_sample_kernel, n_heads=n_heads, dh=dh),
        grid=(n_seq, past // tk),
        in_specs=[
            qblk,
            cblk,
            cblk,
            pl.BlockSpec((1, n_heads, tk), lambda b, t: (b, 0, t)),
            qblk,
            qblk,
            pl.BlockSpec((1, n_heads, LANE), lambda b, t: (b, 0, past // LANE)),
        ],
        out_specs=qblk,
        out_shape=jax.ShapeDtypeStruct((n_seq * lq, da), BF16),
        scratch_shapes=[
            pltpu.VMEM((n_heads * lq, 1), F32),
            pltpu.VMEM((n_heads * lq, 1), F32),
            pltpu.VMEM((n_heads * lq, dh), F32),
        ],
        compiler_params=_params("parallel", "arbitrary"),
        name="attn_sample",
    )(q, cache_k, cache_v, f_all, k_new, v_new, f_all)


def _merge_kernel(yc_ref, ya_ref, h_ref, wc_ref, wa_ref, wgc_ref, wga_ref, bgc_ref, bga_ref, o_ref):
    h = h_ref[...]
    a = jnp.dot(yc_ref[...], wc_ref[...], preferred_element_type=F32)
    b = jnp.dot(ya_ref[...], wa_ref[...], preferred_element_type=F32)
    gc = jax.nn.sigmoid(jnp.dot(h, wgc_ref[...], preferred_element_type=F32) + bgc_ref[...])
    ga = jax.nn.sigmoid(jnp.dot(h, wga_ref[...], preferred_element_type=F32) + bga_ref[...])
    o_ref[...] = (gc * a + ga * b).astype(o_ref.dtype)


def _merge(yc, ya, h, w_brc, w_bra, w_gate, b_gate):
    m, d = h.shape
    dc, da = yc.shape[1], ya.shape[1]
    tm = _tile(m, 1024, BF16_ROWS)
    tn = _tile(d, 512, LANE)
    nj = d // tn
    return pl.pallas_call(
        _merge_kernel,
        grid=(m // tm, nj),
        in_specs=[
            pl.BlockSpec((tm, dc), lambda i, j: (i, 0)),
            pl.BlockSpec((tm, da), lambda i, j: (i, 0)),
            pl.BlockSpec((tm, d), lambda i, j: (i, 0)),
            pl.BlockSpec((dc, tn), lambda i, j: (0, j)),
            pl.BlockSpec((da, tn), lambda i, j: (0, j)),
            pl.BlockSpec((d, tn), lambda i, j: (0, j)),
            pl.BlockSpec((d, tn), lambda i, j: (0, nj + j)),
            pl.BlockSpec((1, tn), lambda i, j: (0, j)),
            pl.BlockSpec((1, tn), lambda i, j: (0, nj + j)),
        ],
        out_specs=pl.BlockSpec((tm, tn), lambda i, j: (i, j)),
        out_shape=jax.ShapeDtypeStruct((m, d), BF16),
        compiler_params=_params("parallel", "parallel"),
        name="merge",
    )(yc, ya, h, w_brc, w_bra, w_gate, w_gate, b_gate, b_gate)


ROUTE_ID0, ROUTE_ID1, ROUTE_W0, ROUTE_W1 = 0, 1, 2, 3


def _wo_kernel(*refs, n_tiles_a, n_groups, per_group):
    a_refs, b_refs, (wo_ref, ng_ref, wr_ref, br_ref), outs = refs[0:5], refs[5:10], refs[10:14], refs[14:17]
    i = pl.program_id(0)

    @pl.when(i < n_tiles_a)
    def _():
        _wo_tile(*a_refs, wo_ref, ng_ref, wr_ref, br_ref, *outs, n_groups=n_groups, per_group=per_group)

    @pl.when(i >= n_tiles_a)
    def _():
        _wo_tile(*b_refs, wo_ref, ng_ref, wr_ref, br_ref, *outs, n_groups=n_groups, per_group=per_group)


def _wo_tile(mg_ref, x_ref, g1_ref, sh_ref, sc_ref, wo_ref, ng_ref, wr_ref, br_ref, x1_ref, h2_ref, rt_ref, *, n_groups, per_group):
    o = jnp.dot(mg_ref[...], wo_ref[...], preferred_element_type=F32)
    x1 = x_ref[...] + _bc(g1_ref) * o
    x1_ref[...] = x1
    h2 = _rms(x1, ng_ref[...]) * (1.0 + _bc(sc_ref)) + _bc(sh_ref)
    h2_ref[...] = _rows_to_slabs(h2)
    wr = wr_ref[...]
    hi = h2.astype(BF16)
    lo = (h2 - hi.astype(F32)).astype(BF16)
    r_hi = jnp.dot(hi, wr, preferred_element_type=F32)
    r_lo = jnp.dot(lo, wr[:, :LANE], preferred_element_type=F32)
    lg = r_hi[:, :LANE] + (r_hi[:, LANE:] + r_lo) + br_ref[...]
    col = lax.broadcasted_iota(I32, lg.shape, 1).astype(F32)
    big = float(LANE)
    gl = jnp.where(col < n_groups, lg, NEG_INF)
    gmax = jnp.max(gl, axis=-1, keepdims=True)
    g_val = 1.0 / jnp.sum(jnp.exp(gl - gmax), axis=-1, keepdims=True)
    g_idx = jnp.min(jnp.where(gl == gmax, col, big), axis=-1, keepdims=True)
    lo = n_groups + g_idx * per_group
    el = jnp.where((col >= lo) & (col < lo + per_group), lg, NEG_INF)
    m1 = jnp.max(el, axis=-1, keepdims=True)
    i1 = jnp.min(jnp.where(el == m1, col, big), axis=-1, keepdims=True)
    el2 = jnp.where(col == i1, NEG_INF, el)
    m2 = jnp.max(el2, axis=-1, keepdims=True)
    i2 = jnp.min(jnp.where(el2 == m2, col, big), axis=-1, keepdims=True)
    r = jnp.exp(m2 - m1)
    w0 = g_val / (1.0 + r)
    w1 = g_val * r / (1.0 + r)
    rt = jnp.where(col == ROUTE_ID0, i1 - n_groups, 0.0)
    rt = jnp.where(col == ROUTE_ID1, i2 - n_groups, rt)
    rt = jnp.where(col == ROUTE_W0, w0, rt)
    rt = jnp.where(col == ROUTE_W1, w1, rt)
    rt_ref[...] = rt


def _wo(path_a, path_b, w_o, norm_g, w_r, b_r, n_groups, per_group):
    d = w_o.shape[0]
    ma, mb = path_a[1].shape[0], path_b[1].shape[0]
    tm = _tile(min(ma, mb), 256, BF16_ROWS)
    assert ma % tm == 0 and mb % tm == 0
    na, nb = ma // tm, mb // tm
    tiles = (lambda i: jnp.minimum(i, na - 1), lambda i: jnp.maximum(i - na, 0))
    in_specs, args = [], []
    for (merged, x, gate1, shift2, scale2, seq_len), tile in zip((path_a, path_b), tiles):
        row = pl.BlockSpec((tm, d), lambda i, tile=tile: (tile(i), 0))
        in_specs += [row, row]
        args += [merged, x]
        for vec in (gate1, shift2, scale2):
            arr, spec = _rowvec(vec, seq_len, tm, tile)
            in_specs.append(spec)
            args.append(arr)
    const = lambda shape: pl.BlockSpec(shape, lambda i: (0, 0))
    in_specs += [const((d, d)), const((1, d)), const((d, 2 * LANE)), const((1, LANE))]
    args += [w_o, norm_g.reshape(1, d), w_r, b_r]
    row = pl.BlockSpec((tm, d), lambda i: (i, 0))
    return pl.pallas_call(
        functools.partial(_wo_kernel, n_tiles_a=na, n_groups=n_groups, per_group=per_group),
        grid=(na + nb,),
        in_specs=in_specs,
        out_specs=[row, pl.BlockSpec((tm, d // LANE, LANE), lambda i: (i, 0, 0)), pl.BlockSpec((tm, LANE), lambda i: (i, 0))],
        out_shape=[
            jax.ShapeDtypeStruct((ma + mb, d), F32),
            jax.ShapeDtypeStruct((ma + mb, d // LANE, LANE), F32),
            jax.ShapeDtypeStruct((ma + mb, LANE), F32),
        ],
        compiler_params=_params("arbitrary"),
        name="wo_norm_router",
    )(*args)


RING_SLOTS = 3
ROW_COPY_PRIORITY = 1


def _gather_start(idx_ref, src_hbm, dst, slot, sem, n_rows, first=0):
    for r in range(first, first + n_rows):
        pltpu.make_async_copy(src_hbm.at[pl.ds(idx_ref[0, 0, r], 1)], dst.at[slot, pl.ds(r, 1)], sem.at[slot]).start(
            priority=ROW_COPY_PRIORITY)


def _gather_wait(src_hbm, dst, slot, sem, n_rows):
    pltpu.make_async_copy(src_hbm.at[pl.ds(0, n_rows)], dst.at[slot], sem.at[slot]).wait()


def _expert_kernel(te_ref, tv_ref, tok0_ref, tok1_ref, tokn_ref, h2_hbm, w1_ref, w3_ref, w2_ref, o_ref, xbuf, w1b, w3b, w2b, sem):
    i = pl.program_id(0)
    tm = xbuf.shape[1]
    slot = lax.rem(i, RING_SLOTS)
    valid = tv_ref[i] == 1
    started = jnp.where(i < 2, tv_ref[0], tv_ref[jnp.maximum(i - 2, 0)]) == 1

    @pl.when(jnp.logical_and(i == 0, valid))
    def _():
        _gather_start(tok0_ref, h2_hbm, xbuf, 0, sem, tm)
        _gather_start(tok1_ref, h2_hbm, xbuf, 1, sem, tm)

    @pl.when(valid)
    def _():
        _gather_wait(h2_hbm, xbuf, slot, sem, tm)

        @pl.when(jnp.logical_or(i == 0, te_ref[i] != te_ref[jnp.maximum(i - 1, 0)]))
        def _():
            w1b[...] = w1_ref[0].astype(BF16)
            w3b[...] = w3_ref[0].astype(BF16)
            w2b[...] = w2_ref[0].astype(BF16)

        nslot = lax.rem(i + 2, RING_SLOTS)
        x = _slabs_to_rows(xbuf[slot]).astype(BF16)
        a = jnp.dot(x, w1b[...], preferred_element_type=F32)
        _gather_start(tokn_ref, h2_hbm, xbuf, nslot, sem, tm // 2)
        b = jnp.dot(x, w3b[...], preferred_element_type=F32)
        _gather_start(tokn_ref, h2_hbm, xbuf, nslot, sem, tm // 2, first=tm // 2)
        hm = (jax.nn.silu(a) * b).astype(BF16)
        o_ref[...] = _rows_to_slabs(jnp.dot(hm, w2b[...], preferred_element_type=F32))

    @pl.when(jnp.logical_not(valid))
    def _():
        @pl.when(started)
        def _():
            _gather_wait(h2_hbm, xbuf, slot, sem, tm)

        o_ref[...] = jnp.zeros(o_ref.shape, F32)


def _experts(h2_all, tile_expert, tile_valid, tok_sorted, w1, w3, w2, tm):
    n_tiles = tile_expert.shape[0]
    _, d, de = w1.shape
    tok3 = tok_sorted.reshape(n_tiles, 1, tm)
    assert n_tiles >= RING_SLOTS
    idx_spec = lambda tile: pl.BlockSpec((1, 1, tm), lambda i, te, tv: (tile(i), 0, 0), memory_space=pltpu.SMEM)
    grid_spec = pltpu.PrefetchScalarGridSpec(
        num_scalar_prefetch=2,
        grid=(n_tiles,),
        in_specs=[
            idx_spec(lambda i: 0),
            idx_spec(lambda i: 1),
            idx_spec(lambda i: jnp.minimum(i + 2, n_tiles - 1)),
            pl.BlockSpec(memory_space=pl.ANY),
            pl.BlockSpec((1, d, de), lambda i, te, tv: (te[i], 0, 0)),
            pl.BlockSpec((1, d, de), lambda i, te, tv: (te[i], 0, 0)),
            pl.BlockSpec((1, de, d), lambda i, te, tv: (te[i], 0, 0)),
        ],
        out_specs=pl.BlockSpec((tm, d // LANE, LANE), lambda i, te, tv: (i, 0, 0)),
        scratch_shapes=[
            pltpu.VMEM((RING_SLOTS, tm, d // LANE, LANE), F32),
            pltpu.VMEM((d, de), BF16),
            pltpu.VMEM((d, de), BF16),
            pltpu.VMEM((de, d), BF16),
            pltpu.SemaphoreType.DMA((RING_SLOTS,)),
        ],
    )
    return pl.pallas_call(
        _expert_kernel,
        grid_spec=grid_spec,
        out_shape=jax.ShapeDtypeStruct((n_tiles * tm, d // LANE, LANE), F32),
        compiler_params=_params("arbitrary"),
        name="experts",
    )(tile_expert, tile_valid, tok3, tok3, tok3, h2_all, w1, w3, w2)


COMBINE_CHUNKS = 4


def _combine_kernel(d_ref, dn_ref, ys_hbm, x1_ref, rt_ref, g2_ref, ng_ref, y_ref, ybuf, sem):
    i = pl.program_id(0)
    tm = x1_ref.shape[0]
    slot = lax.rem(i, 2)

    @pl.when(i == 0)
    def _():
        _gather_start(d_ref, ys_hbm, ybuf, 0, sem, 2 * tm)

    _gather_wait(ys_hbm, ybuf, slot, sem, 2 * tm)
    g2 = _bc(g2_ref)
    ng = ng_ref[...]
    ck = tm // COMBINE_CHUNKS
    for c in range(COMBINE_CHUNKS):
        _gather_start(dn_ref, ys_hbm, ybuf, 1 - slot, sem, 2 * ck, first=2 * ck * c)
        rows = slice(c * ck, (c + 1) * ck)
        rt = rt_ref[rows, :]
        y0 = _slabs_to_rows(ybuf[slot, c * ck:(c + 1) * ck])
        y1 = _slabs_to_rows(ybuf[slot, tm + c * ck:tm + (c + 1) * ck])
        moe = rt[:, ROUTE_W0:ROUTE_W0 + 1] * y0 + rt[:, ROUTE_W1:ROUTE_W1 + 1] * y1
        g2c = g2 if g2.shape[0] == 1 else g2[rows]
        y_ref[rows, :] = _rms(x1_ref[rows, :] + g2c * moe, ng)

    @pl.when(i == pl.num_programs(0) - 1)
    def _():
        _gather_wait(ys_hbm, ybuf, 1 - slot, sem, 2 * tm)


def _combine(ys, dest, x1_all, route_all, row_off, m, gate2, normf_g, seq_len):
    d = x1_all.shape[1]
    tm = _tile(seq_len, 256, 8) if seq_len >= 256 else _tile(m, 256, seq_len)
    assert row_off % tm == 0 and tm % COMBINE_CHUNKS == 0
    nb, off = m // tm, row_off // tm
    d3 = dest.reshape(nb, tm, 2).transpose(0, 2, 1).reshape(nb, 1, 2 * tm)
    d3 = jnp.concatenate([d3, jnp.zeros((1, 1, 2 * tm), I32)], axis=0)
    g2, g2_spec = _rowvec(gate2, seq_len, tm)
    return pl.pallas_call(
        _combine_kernel,
        grid=(nb,),
        in_specs=[
            pl.BlockSpec((1, 1, 2 * tm), lambda i: (i, 0, 0), memory_space=pltpu.SMEM),
            pl.BlockSpec((1, 1, 2 * tm), lambda i: (i + 1, 0, 0), memory_space=pltpu.SMEM),
            pl.BlockSpec(memory_space=pl.ANY),
            pl.BlockSpec((tm, d), lambda i: (off + i, 0)),
            pl.BlockSpec((tm, LANE), lambda i: (off + i, 0)),
            g2_spec,
            pl.BlockSpec((1, d), lambda i: (0, 0)),
        ],
        out_specs=pl.BlockSpec((tm, d), lambda i: (i, 0)),
        out_shape=jax.ShapeDtypeStruct((m, d), F32),
        scratch_shapes=[pltpu.VMEM((2, 2 * tm, d // LANE, LANE), F32), pltpu.SemaphoreType.DMA((2,))],
        compiler_params=_params("arbitrary"),
        name="combine_final_norm",
    )(d3, d3, ys, x1_all, route_all, g2, normf_g.reshape(1, d))


def _route_plan(route, n_experts, tm):
    t = route.shape[0]
    s = TOP_K * t
    e_flat = route[:, ROUTE_ID0:ROUTE_ID1 + 1].astype(I32).reshape(s)
    onehot = (e_flat[:, None] == jnp.arange(n_experts, dtype=I32)[None, :]).astype(I32)
    csum = jnp.cumsum(onehot, axis=0)
    rank = jnp.take_along_axis(csum, e_flat[:, None], axis=1)[:, 0] - 1
    counts = csum[-1]
    padded = (counts + tm - 1) // tm * tm
    ends = jnp.cumsum(padded)
    dest = (ends - padded)[e_flat] + rank
    n_tiles = pl.cdiv(s, tm) + n_experts + 2
    tile_start = jnp.arange(n_tiles, dtype=I32) * tm
    tile_valid = (tile_start < ends[-1]).astype(I32)
    te = jnp.minimum(jnp.sum((ends[None, :] <= tile_start[:, None]).astype(I32), axis=1), n_experts - 1)
    last_used = jnp.maximum(ends[-1] // tm - 1, 0)
    tile_expert = jnp.where(tile_valid == 1, te, te[last_used])
    tok_sorted = jnp.zeros((n_tiles * tm,), I32).at[dest].set(
        jnp.arange(s, dtype=I32) // TOP_K, unique_indices=True, mode="promise_in_bounds")
    return dest.reshape(t, TOP_K), tok_sorted, tile_expert, tile_valid


def kernel(x_prompt, x_sample, cache_k, cache_v, cache_logf, state_conv, c_prompt, c_sample, w_ada, b_ada, norm1_g, norm2_g, normf_g, w_in, conv_w, conv_b, b_forget, w_br_conv, w_br_attn, w_gate, b_gate, w_o, w_rg, b_rg, w_re, b_re, w1, w3, w2):
    depth = w_ada.shape[0]
    assert depth == 1, "single-layer step"
    l = 0
    bp, sp, d = x_prompt.shape
    bs, ss, _ = x_sample.shape
    _, _, past, n_heads, dh = cache_k.shape
    d_conv = state_conv.shape[-1]
    da = n_heads * dh
    n_groups = w_rg.shape[-1]
    n_experts = w_re.shape[-1]
    per_group = n_experts // n_groups
    assert n_groups + n_experts <= LANE
    scale = LOG2_E * dh ** -0.5

    w_in_bf = w_in[l].astype(BF16)
    q_off = 3 * d_conv
    w_q, w_k, w_v = (w_in_bf[:, q_off + j * da:q_off + (j + 1) * da] for j in range(3))
    hp = _round_up(n_heads, BF16_ROWS)
    wf_t = jnp.zeros((hp, d), BF16).at[:n_heads].set(w_in[l][:, 3 * d_conv + 3 * da:].T.astype(BF16))
    b_f = jnp.zeros((hp, 1), F32).at[:n_heads, 0].set(b_forget[l])
    w_brc_bf = w_br_conv[l].astype(BF16)
    w_bra_bf = w_br_attn[l].astype(BF16)
    w_gate_bf = w_gate[l].astype(BF16)
    b_gate2 = b_gate[l].reshape(1, 2 * d)
    w_o_bf = w_o[l].astype(BF16)
    w_r32 = jnp.pad(jnp.concatenate([w_rg[l], w_re[l]], axis=1), ((0, 0), (0, LANE - n_groups - n_experts)))
    w_r_hi = w_r32.astype(BF16)
    w_r = jnp.concatenate([w_r_hi, (w_r32 - w_r_hi.astype(F32)).astype(BF16)], axis=1)
    b_r = jnp.zeros((1, LANE), F32).at[0, :n_groups].set(b_rg[l]).at[0, n_groups:n_groups + n_experts].set(b_re[l])

    mod = _ada(jnp.concatenate([c_prompt, c_sample], axis=0), w_ada[l], b_ada[l])
    mods = {"p": jnp.split(mod[:bp], 6, axis=-1), "s": jnp.split(mod[bp:], 6, axis=-1)}

    def mixer_inputs(x, mod6, seq_len, prev):
        shift1, scale1 = mod6[0], mod6[1]
        h = _norm(x, norm1_g[l], shift1, scale1, seq_len)
        y_conv, tail = _conv_branch(h, w_in_bf, conv_w[l], conv_b[l], prev, seq_len, d_conv)
        q = _proj(h, w_q, BF16, scale)
        k = _proj(h, w_k, F32)
        v = _proj(h, w_v, F32)
        lf_t = _forget(h, wf_t, b_f)[:n_heads]
        return h, y_conv, tail, q, k, v, lf_t

    xp = x_prompt.reshape(bp * sp, d)
    xs = x_sample.reshape(bs * ss, d)
    hp_, ycp, tailp, qp, kp, vp, lftp = mixer_inputs(xp, mods["p"], sp, None)
    hs_, ycs, tails, qs, ks, vs, lfts = mixer_inputs(xs, mods["s"], ss, state_conv[l])

    lf_p = lftp.reshape(n_heads, bp, sp).transpose(1, 0, 2).reshape(bp * n_heads, sp)
    f_p = _cumsum_lanes(lf_p).reshape(bp * n_heads, 1, sp)
    yap = _attn_prompt(qp, kp, vp, f_p, bp, sp, n_heads, dh)

    lf_new = lfts.reshape(n_heads, bs, ss).transpose(1, 0, 2)
    lf_past = cache_logf[l].astype(F32).transpose(0, 2, 1)
    tot = _round_up(past + LANE, CUMSUM_CHUNK)
    lf_all = jnp.concatenate([lf_past, lf_new, jnp.zeros((bs, n_heads, tot - past - ss), F32)], axis=-1)
    f_s = _cumsum_lanes(lf_all.reshape(bs * n_heads, tot)).reshape(bs, n_heads, tot)
    yas = _attn_sample(qs, cache_k, cache_v, l, ks, vs, f_s, bs, ss, past, n_heads, dh)

    def wo_path(x, mod6, seq_len, h, yc, ya):
        gate1, shift2, scale2 = mod6[2], mod6[3], mod6[4]
        merged = _merge(yc, ya, h, w_brc_bf, w_bra_bf, w_gate_bf, b_gate2)
        return merged, x, gate1, shift2, scale2, seq_len

    x1_all, h2_all, route_all = _wo(
        wo_path(xp, mods["p"], sp, hp_, ycp, yap), wo_path(xs, mods["s"], ss, hs_, ycs, yas),
        w_o_bf, norm2_g[l], w_r, b_r, n_groups, per_group)
    tm_e = 256 if TOP_K * h2_all.shape[0] >= 8192 else 64
    dest, tok_sorted, tile_expert, tile_valid = _route_plan(route_all, n_experts, tm_e)
    ys = _experts(h2_all, tile_expert, tile_valid, tok_sorted, w1[l], w3[l], w2[l], tm_e)

    mp, ms = bp * sp, bs * ss
    y_p = _combine(ys, dest[:mp], x1_all, route_all, 0, mp, mods["p"][5], normf_g, sp)
    y_s = _combine(ys, dest[mp:], x1_all, route_all, mp, ms, mods["s"][5], normf_g, ss)

    def cache_outs(k, v, lf_t, tail, n_seq, seq_len):
        return (
            k.reshape(1, n_seq, seq_len, n_heads, dh),
            v.reshape(1, n_seq, seq_len, n_heads, dh),
            lf_t.T.reshape(1, n_seq, seq_len, n_heads),
            tail[:, CONV_TAIL_ROWS - 2:, :].reshape(1, n_seq, 2, d_conv),
        )

    return (y_p.reshape(bp, sp, d), y_s.reshape(bs, ss, d)) + cache_outs(kp, vp, lftp, tailp, bp, sp) + cache_outs(ks, vs, lfts, tails, bs, ss)
```

```python
import functools

import jax
import jax.numpy as jnp
from jax import lax
from jax.experimental import pallas as pl
from jax.experimental.pallas import tpu as pltpu

F32 = jnp.float32
BF16 = jnp.bfloat16
I32 = jnp.int32

EPS = 1e-6
NEG_INF = -1e30
LOG2_E = 1.4426950408889634
TOP_K = 2
LANE = 128
BF16_ROWS = 16
VMEM_LIMIT_BYTES = 56 * 1024 * 1024
NT_DIMS = (((1,), (1,)), ((), ()))


def _tile(n, pref, align):
    if n <= pref:
        return n
    t = (pref // align) * align
    while t >= align:
        if n % t == 0:
            return t
        t -= align
    return n


def _round_up(n, m):
    return (n + m - 1) // m * m


def _params(*sem):
    return pltpu.CompilerParams(dimension_semantics=sem, vmem_limit_bytes=VMEM_LIMIT_BYTES)


def _rowvec(vec, seq_len, tm, tile=lambda i: i):
    n_seq, d = vec.shape
    if tm <= seq_len:
        per = seq_len // tm
        return vec.reshape(n_seq, 1, d), pl.BlockSpec((1, 1, d), lambda i, *_: (tile(i) // per, 0, 0))
    return jnp.repeat(vec, seq_len, axis=0), pl.BlockSpec((tm, d), lambda i, *_: (tile(i), 0))


def _bc(ref):
    v = ref[...]
    return v.reshape(v.shape[-2], v.shape[-1])


def _rms(x, g):
    return x * lax.rsqrt(jnp.mean(x * x, axis=-1, keepdims=True) + EPS) * g


def _ada_kernel(c_ref, w_ref, b_ref, o_ref):
    o_ref[...] = jnp.dot(c_ref[...].astype(BF16), w_ref[...].astype(BF16), preferred_element_type=F32) + b_ref[...]


def _ada(c_all, w, b):
    r, d = c_all.shape
    n = w.shape[1]
    tn = _tile(n, 1024, LANE)
    return pl.pallas_call(
        _ada_kernel,
        grid=(n // tn,),
        in_specs=[
            pl.BlockSpec((r, d), lambda j: (0, 0)),
            pl.BlockSpec((d, tn), lambda j: (0, j)),
            pl.BlockSpec((1, tn), lambda j: (0, j)),
        ],
        out_specs=pl.BlockSpec((r, tn), lambda j: (0, j)),
        out_shape=jax.ShapeDtypeStruct((r, n), F32),
        compiler_params=_params("parallel"),
        name="ada",
    )(c_all, w, b.reshape(1, n))


def _norm_kernel(x_ref, g_ref, sh_ref, sc_ref, h_ref):
    h = _rms(x_ref[...], g_ref[...]) * (1.0 + _bc(sc_ref)) + _bc(sh_ref)
    h_ref[...] = h.astype(h_ref.dtype)


def _norm(x, g, shift, scale, seq_len):
    m, d = x.shape
    tm = _tile(seq_len, 512, 8)
    sh, sh_spec = _rowvec(shift, seq_len, tm)
    sc, sc_spec = _rowvec(scale, seq_len, tm)
    return pl.pallas_call(
        _norm_kernel,
        grid=(m // tm,),
        in_specs=[pl.BlockSpec((tm, d), lambda i: (i, 0)), pl.BlockSpec((1, d), lambda i: (0, 0)), sh_spec, sc_spec],
        out_specs=pl.BlockSpec((tm, d), lambda i: (i, 0)),
        out_shape=jax.ShapeDtypeStruct((m, d), BF16),
        compiler_params=_params("parallel"),
        name="norm",
    )(x, g.reshape(1, d), sh, sc)


def _proj_kernel(a_ref, w_ref, o_ref, *, scale):
    acc = jnp.dot(a_ref[...], w_ref[...], preferred_element_type=F32)
    if scale != 1.0:
        acc = acc * scale
    o_ref[...] = acc.astype(o_ref.dtype)


RESIDENT_WEIGHT_BYTES = 8 * 1024 * 1024


def _proj(a, w, out_dtype, scale=1.0):
    m, k = a.shape
    n = w.shape[1]
    if k * n * w.dtype.itemsize <= RESIDENT_WEIGHT_BYTES:
        tm, tn = _tile(m, 512, BF16_ROWS), n
    else:
        tm, tn = _tile(m, 1024, BF16_ROWS), _tile(n, 512, LANE)
    return pl.pallas_call(
        functools.partial(_proj_kernel, scale=scale),
        grid=(m // tm, n // tn),
        in_specs=[pl.BlockSpec((tm, k), lambda i, j: (i, 0)), pl.BlockSpec((k, tn), lambda i, j: (0, j))],
        out_specs=pl.BlockSpec((tm, tn), lambda i, j: (i, j)),
        out_shape=jax.ShapeDtypeStruct((m, n), out_dtype),
        compiler_params=_params("parallel", "parallel"),
        name="proj",
    )(a, w)


CONV_TAIL_ROWS = 8


def _conv_kernel(*refs, seq_len, has_prev):
    if has_prev:
        h_ref, wb_ref, wc_ref, wv_ref, cw_ref, cb_ref, p1_ref, p2_ref, y_ref, tail_ref = refs
    else:
        h_ref, wb_ref, wc_ref, wv_ref, cw_ref, cb_ref, y_ref, tail_ref = refs
    h = h_ref[...]
    bg = jnp.dot(h, wb_ref[...], preferred_element_type=F32)
    cg = jnp.dot(h, wc_ref[...], preferred_element_type=F32)
    vc = jnp.dot(h, wv_ref[...], preferred_element_type=F32)
    u = cg * vc
    tm = u.shape[0]
    t = lax.broadcasted_iota(I32, (tm, 1), 0)
    if tm > seq_len:
        t = jnp.bitwise_and(t, seq_len - 1) if seq_len & (seq_len - 1) == 0 else lax.rem(t, seq_len)
    u1 = jnp.where(t >= 1, pltpu.roll(u, 1, 0), p1_ref[...] if has_prev else 0.0)
    u2 = jnp.where(t >= 2, pltpu.roll(u, 2, 0), p2_ref[...] if has_prev else 0.0)
    cw = cw_ref[...]
    conv = cb_ref[...] + ((cw[0:1] * u2 + cw[1:2] * u1) + cw[2:3] * u)
    y_ref[...] = (bg * conv).astype(y_ref.dtype)
    for s in range(tm // seq_len):
        tail_ref[s] = u[(s + 1) * seq_len - CONV_TAIL_ROWS:(s + 1) * seq_len, :]


def _conv_branch(h, w_in_bf, conv_w, conv_b, prev, seq_len, d_conv):
    m, d = h.shape
    assert conv_w.shape[0] == 3 and seq_len >= CONV_TAIL_ROWS
    tm = seq_len if seq_len >= 512 else m
    assert m % tm == 0 and tm % seq_len == 0
    tc = _tile(d_conv, 256, LANE)
    nc = d_conv // tc
    n_seq = m // seq_len
    seq_per_tile = tm // seq_len
    in_specs = [
        pl.BlockSpec((tm, d), lambda i, j: (i, 0)),
        pl.BlockSpec((d, tc), lambda i, j: (0, j)),
        pl.BlockSpec((d, tc), lambda i, j: (0, nc + j)),
        pl.BlockSpec((d, tc), lambda i, j: (0, 2 * nc + j)),
        pl.BlockSpec((3, tc), lambda i, j: (0, j)),
        pl.BlockSpec((1, tc), lambda i, j: (0, j)),
    ]
    args = [h, w_in_bf, w_in_bf, w_in_bf, conv_w, conv_b.reshape(1, d_conv)]
    if prev is not None:
        zeros = jnp.zeros((n_seq, seq_len, d_conv), F32)
        p1 = zeros.at[:, 0].set(prev[:, 1]).reshape(m, d_conv)
        p2 = zeros.at[:, 0].set(prev[:, 0]).at[:, 1].set(prev[:, 1]).reshape(m, d_conv)
        in_specs += [pl.BlockSpec((tm, tc), lambda i, j: (i, j))] * 2
        args += [p1, p2]
    return pl.pallas_call(
        functools.partial(_conv_kernel, seq_len=seq_len, has_prev=prev is not None),
        grid=(m // tm, nc),
        in_specs=in_specs,
        out_specs=[
            pl.BlockSpec((tm, tc), lambda i, j: (i, j)),
            pl.BlockSpec((seq_per_tile, CONV_TAIL_ROWS, tc), lambda i, j: (i, 0, j)),
        ],
        out_shape=[
            jax.ShapeDtypeStruct((m, d_conv), BF16),
            jax.ShapeDtypeStruct((n_seq, CONV_TAIL_ROWS, d_conv), F32),
        ],
        compiler_params=_params("parallel", "parallel"),
        name="conv_branch",
    )(*args)


def _forget_kernel(wf_ref, h_ref, b_ref, lf_ref):
    fl = lax.dot_general(wf_ref[...], h_ref[...], NT_DIMS, preferred_element_type=F32)
    lf_ref[...] = jax.nn.log_sigmoid(fl + b_ref[...])


def _forget(h, wf_t, b_f):
    m, d = h.shape
    hp = wf_t.shape[0]
    tm = _tile(m, 1024, LANE)
    return pl.pallas_call(
        _forget_kernel,
        grid=(m // tm,),
        in_specs=[
            pl.BlockSpec((hp, d), lambda i: (0, 0)),
            pl.BlockSpec((tm, d), lambda i: (i, 0)),
            pl.BlockSpec((hp, 1), lambda i: (0, 0)),
        ],
        out_specs=pl.BlockSpec((hp, tm), lambda i: (0, i)),
        out_shape=jax.ShapeDtypeStruct((hp, m), F32),
        compiler_params=_params("parallel"),
        name="forget",
    )(wf_t, h, b_f)


CUMSUM_CHUNK = LANE


def _cumsum_kernel(x_ref, o_ref):
    r, n = x_ref.shape
    c = CUMSUM_CHUNK
    tri = (lax.broadcasted_iota(I32, (c, c), 0) <= lax.broadcasted_iota(I32, (c, c), 1)).astype(BF16)
    carry = jnp.zeros((r, 1), F32)
    for j in range(n // c):
        x = x_ref[:, j * c:(j + 1) * c]
        cs = carry
        for _ in range(3):
            piece = x.astype(BF16)
            cs = cs + jnp.dot(piece, tri, preferred_element_type=F32)
            x = x - piece.astype(F32)
        o_ref[:, j * c:(j + 1) * c] = cs
        carry = cs[:, c - 1:c]


def _cumsum_lanes(x):
    r, n = x.shape
    tr = _tile(r, 128, 8)
    return pl.pallas_call(
        _cumsum_kernel,
        grid=(r // tr,),
        in_specs=[pl.BlockSpec((tr, n), lambda i: (i, 0))],
        out_specs=pl.BlockSpec((tr, n), lambda i: (i, 0)),
        out_shape=jax.ShapeDtypeStruct((r, n), F32),
        compiler_params=_params("parallel"),
        name="cumsum",
    )(x)


def _attn_prompt_kernel(q_ref, k_ref, v_ref, f_ref, o_ref, *, tq, dh):
    seq_len = q_ref.shape[0]
    heads = q_ref.shape[1] // dh
    ones_col = (lax.broadcasted_iota(I32, (seq_len, dh), 1) == 0).astype(BF16)
    kb, vb, f = [], [], []
    for h in range(heads):
        cols = slice(h * dh, (h + 1) * dh)
        kb.append(k_ref[:, cols].astype(BF16))
        vb.append(jnp.concatenate([v_ref[:, cols].astype(BF16), ones_col], axis=1))
        f.append(f_ref[h] * LOG2_E)
    causal = lax.broadcasted_iota(I32, (tq, tq), 1) <= lax.broadcasted_iota(I32, (tq, tq), 0)
    for qi in range(seq_len // tq):
        lo = qi * tq
        for h in range(heads):
            cols = slice(h * dh, (h + 1) * dh)
            q = q_ref[lo:lo + tq, cols]
            sd = lax.dot_general(q, kb[h][lo:lo + tq], NT_DIMS, preferred_element_type=F32) - f[h][:, lo:lo + tq]
            sd = jnp.where(causal, sd, NEG_INF)
            m = jnp.max(sd, axis=-1, keepdims=True)
            if qi > 0:
                sp = lax.dot_general(q, kb[h][:lo], NT_DIMS, preferred_element_type=F32) - f[h][:, :lo]
                m = jnp.maximum(m, jnp.max(sp, axis=-1, keepdims=True))
                ol = jnp.dot(jnp.exp2(sp - m).astype(BF16), vb[h][:lo], preferred_element_type=F32)
            old = jnp.dot(jnp.exp2(sd - m).astype(BF16), vb[h][lo:lo + tq], preferred_element_type=F32)
            ol = ol + old if qi > 0 else old
            o_ref[lo:lo + tq, cols] = (ol[:, :dh] / ol[:, dh:dh + 1]).astype(o_ref.dtype)


ATTN_HEADS_PER_STEP = 4


def _attn_prompt(q, k, v, f, n_seq, seq_len, n_heads, dh):
    m = n_seq * seq_len
    tq = _tile(seq_len, 256, LANE)
    hps = ATTN_HEADS_PER_STEP if n_heads % ATTN_HEADS_PER_STEP == 0 else 1
    groups = n_heads // hps
    blk = pl.BlockSpec((seq_len, hps * dh), lambda b, g: (b, g))
    return pl.pallas_call(
        functools.partial(_attn_prompt_kernel, tq=tq, dh=dh),
        grid=(n_seq, groups),
        in_specs=[blk, blk, blk, pl.BlockSpec((hps, 1, seq_len), lambda b, g: (b * groups + g, 0, 0))],
        out_specs=blk,
        out_shape=jax.ShapeDtypeStruct((m, n_heads * dh), BF16),
        compiler_params=_params("parallel", "parallel"),
        name="attn_prompt",
    )(q, k, v, f)


def _attn_sample_kernel(q_ref, kc_ref, vc_ref, fp_ref, kn_ref, vn_ref, fn_ref, o_ref, m_sc, l_sc, acc_sc, *, n_heads, dh):
    kt = pl.program_id(1)
    lq = q_ref.shape[0]

    @pl.when(kt == 0)
    def _():
        m_sc[...] = jnp.full(m_sc.shape, NEG_INF, F32)
        l_sc[...] = jnp.zeros(l_sc.shape, F32)
        acc_sc[...] = jnp.zeros(acc_sc.shape, F32)

    q = q_ref[...]

    def update(k_heads, v_heads, fk, mask):
        s_heads = []
        for h in range(n_heads):
            s = lax.dot_general(q[:, h * dh:(h + 1) * dh], k_heads[h], NT_DIMS, preferred_element_type=F32)
            s = s - fk[h:h + 1, :]
            if mask is not None:
                s = jnp.where(mask, s, NEG_INF)
            s_heads.append(s)
        s = jnp.concatenate(s_heads, axis=0)
        m_prev = m_sc[...]
        m_new = jnp.maximum(m_prev, jnp.max(s, axis=-1, keepdims=True))
        alpha = jnp.exp2(m_prev - m_new)
        p = jnp.exp2(s - m_new)
        l_sc[...] = alpha * l_sc[...] + jnp.sum(p, axis=-1, keepdims=True)
        pb = p.astype(BF16)
        o_heads = [jnp.dot(pb[h * lq:(h + 1) * lq], v_heads[h], preferred_element_type=F32) for h in range(n_heads)]
        acc_sc[...] = alpha * acc_sc[...] + jnp.concatenate(o_heads, axis=0)
        m_sc[...] = m_new

    def cached_heads(ref):
        x = jnp.swapaxes(ref[0, 0].astype(BF16), 0, 1)
        return [x[h] for h in range(n_heads)]

    update(cached_heads(kc_ref), cached_heads(vc_ref), fp_ref[0] * LOG2_E, None)

    @pl.when(kt == pl.num_programs(1) - 1)
    def _():
        pad = jnp.zeros((LANE - lq, n_heads * dh), BF16)
        kn = jnp.concatenate([kn_ref[...].astype(BF16), pad], axis=0)
        vn = jnp.concatenate([vn_ref[...].astype(BF16), pad], axis=0)
        mask = lax.broadcasted_iota(I32, (lq, LANE), 1) <= lax.broadcasted_iota(I32, (lq, LANE), 0)
        heads = lambda x: [x[:, h * dh:(h + 1) * dh] for h in range(n_heads)]
        update(heads(kn), heads(vn), fn_ref[0] * LOG2_E, mask)
        out = acc_sc[...] / l_sc[...]
        o_ref[...] = jnp.concatenate([out[h * lq:(h + 1) * lq] for h in range(n_heads)], axis=1).astype(o_ref.dtype)


def _attn_sample(q, cache_k, cache_v, layer, k_new, v_new, f_all, n_seq, lq, past, n_heads, dh):
    da = n_heads * dh
    tk = _tile(past, 512, LANE)
    assert lq <= LANE and lq % BF16_ROWS == 0 and past % LANE == 0
    qblk = pl.BlockSpec((lq, da), lambda b, t: (b, 0))
    cblk = pl.BlockSpec((1, 1, tk, n_heads, dh), lambda b, t: (layer, b, t, 0, 0))
    return pl.pallas_call(
        functools.partial(_attn_sample_kernel, n_heads=n_heads, dh=dh),
        grid=(n_seq, past // tk),
        in_specs=[
            qblk,
            cblk,
            cblk,
            pl.BlockSpec((1, n_heads, tk), lambda b, t: (b, 0, t)),
            qblk,
            qblk,
            pl.BlockSpec((1, n_heads, LANE), lambda b, t: (b, 0, past // LANE)),
        ],
        out_specs=qblk,
        out_shape=jax.ShapeDtypeStruct((n_seq * lq, da), BF16),
        scratch_shapes=[
            pltpu.VMEM((n_heads * lq, 1), F32),
            pltpu.VMEM((n_heads * lq, 1), F32),
            pltpu.VMEM((n_heads * lq, dh), F32),
        ],
        compiler_params=_params("parallel", "arbitrary"),
        name="attn_sample",
    )(q, cache_k, cache_v, f_all, k_new, v_new, f_all)


def _merge_kernel(yc_ref, ya_ref, h_ref, wc_ref, wa_ref, wgc_ref, wga_ref, bgc_ref, bga_ref, o_ref):
    h = h_ref[...]
    a = jnp.dot(yc_ref[...], wc_ref[...], preferred_element_type=F32)
    b = jnp.dot(ya_ref[...], wa_ref[...], preferred_element_type=F32)
    gc = jax.nn.sigmoid(jnp.dot(h, wgc_ref[...], preferred_element_type=F32) + bgc_ref[...])
    ga = jax.nn.sigmoid(jnp.dot(h, wga_ref[...], preferred_element_type=F32) + bga_ref[...])
    o_ref[...] = (gc * a + ga * b).astype(o_ref.dtype)


def _merge(yc, ya, h, w_brc, w_bra, w_gate, b_gate):
    m, d = h.shape
    dc, da = yc.shape[1], ya.shape[1]
    tm = _tile(m, 1024, BF16_ROWS)
    tn = _tile(d, 512, LANE)
    nj = d // tn
    return pl.pallas_call(
        _merge_kernel,
        grid=(m // tm, nj),
        in_specs=[
            pl.BlockSpec((tm, dc), lambda i, j: (i, 0)),
            pl.BlockSpec((tm, da), lambda i, j: (i, 0)),
            pl.BlockSpec((tm, d), lambda i, j: (i, 0)),
            pl.BlockSpec((dc, tn), lambda i, j: (0, j)),
            pl.BlockSpec((da, tn), lambda i, j: (0, j)),
            pl.BlockSpec((d, tn), lambda i, j: (0, j)),
            pl.BlockSpec((d, tn), lambda i, j: (0, nj + j)),
            pl.BlockSpec((1, tn), lambda i, j: (0, j)),
            pl.BlockSpec((1, tn), lambda i, j: (0, nj + j)),
        ],
        out_specs=pl.BlockSpec((tm, tn), lambda i, j: (i, j)),
        out_shape=jax.ShapeDtypeStruct((m, d), BF16),
        compiler_params=_params("parallel", "parallel"),
        name="merge",
    )(yc, ya, h, w_brc, w_bra, w_gate, w_gate, b_gate, b_gate)


ROUTE_ID0, ROUTE_ID1, ROUTE_W0, ROUTE_W1 = 0, 1, 2, 3


def _wo_kernel(*refs, n_tiles_a, n_groups, per_group):
    a_refs, b_refs, (wo_ref, ng_ref, wr_ref, br_ref), outs = refs[0:5], refs[5:10], refs[10:14], refs[14:17]
    i = pl.program_id(0)

    @pl.when(i < n_tiles_a)
    def _():
        _wo_tile(*a_refs, wo_ref, ng_ref, wr_ref, br_ref, *outs, n_groups=n_groups, per_group=per_group)

    @pl.when(i >= n_tiles_a)
    def _():
        _wo_tile(*b_refs, wo_ref, ng_ref, wr_ref, br_ref, *outs, n_groups=n_groups, per_group=per_group)


def _wo_tile(mg_ref, x_ref, g1_ref, sh_ref, sc_ref, wo_ref, ng_ref, wr_ref, br_ref, x1_ref, h2_ref, rt_ref, *, n_groups, per_group):
    o = jnp.dot(mg_ref[...], wo_ref[...], preferred_element_type=F32)
    x1 = x_ref[...] + _bc(g1_ref) * o
    x1_ref[...] = x1
    h2 = _rms(x1, ng_ref[...]) * (1.0 + _bc(sc_ref)) + _bc(sh_ref)
    h2_ref[...] = h2
    wr = wr_ref[...]
    hi = h2.astype(BF16)
    lo = (h2 - hi.astype(F32)).astype(BF16)
    r_hi = jnp.dot(hi, wr, preferred_element_type=F32)
    r_lo = jnp.dot(lo, wr[:, :LANE], preferred_element_type=F32)
    lg = r_hi[:, :LANE] + (r_hi[:, LANE:] + r_lo) + br_ref[...]
    col = lax.broadcasted_iota(I32, lg.shape, 1).astype(F32)
    big = float(LANE)
    gl = jnp.where(col < n_groups, lg, NEG_INF)
    gmax = jnp.max(gl, axis=-1, keepdims=True)
    g_val = 1.0 / jnp.sum(jnp.exp(gl - gmax), axis=-1, keepdims=True)
    g_idx = jnp.min(jnp.where(gl == gmax, col, big), axis=-1, keepdims=True)
    lo = n_groups + g_idx * per_group
    el = jnp.where((col >= lo) & (col < lo + per_group), lg, NEG_INF)
    m1 = jnp.max(el, axis=-1, keepdims=True)
    i1 = jnp.min(jnp.where(el == m1, col, big), axis=-1, keepdims=True)
    el2 = jnp.where(col == i1, NEG_INF, el)
    m2 = jnp.max(el2, axis=-1, keepdims=True)
    i2 = jnp.min(jnp.where(el2 == m2, col, big), axis=-1, keepdims=True)
    r = jnp.exp(m2 - m1)
    w0 = g_val / (1.0 + r)
    w1 = g_val * r / (1.0 + r)
    rt = jnp.where(col == ROUTE_ID0, i1 - n_groups, 0.0)
    rt = jnp.where(col == ROUTE_ID1, i2 - n_groups, rt)
    rt = jnp.where(col == ROUTE_W0, w0, rt)
    rt = jnp.where(col == ROUTE_W1, w1, rt)
    rt_ref[...] = rt


def _wo(path_a, path_b, w_o, norm_g, w_r, b_r, n_groups, per_group):
    d = w_o.shape[0]
    ma, mb = path_a[1].shape[0], path_b[1].shape[0]
    tm = _tile(min(ma, mb), 256, BF16_ROWS)
    assert ma % tm == 0 and mb % tm == 0
    na, nb = ma // tm, mb // tm
    tiles = (lambda i: jnp.minimum(i, na - 1), lambda i: jnp.maximum(i - na, 0))
    in_specs, args = [], []
    for (merged, x, gate1, shift2, scale2, seq_len), tile in zip((path_a, path_b), tiles):
        row = pl.BlockSpec((tm, d), lambda i, tile=tile: (tile(i), 0))
        in_specs += [row, row]
        args += [merged, x]
        for vec in (gate1, shift2, scale2):
            arr, spec = _rowvec(vec, seq_len, tm, tile)
            in_specs.append(spec)
            args.append(arr)
    const = lambda shape: pl.BlockSpec(shape, lambda i: (0, 0))
    in_specs += [const((d, d)), const((1, d)), const((d, 2 * LANE)), const((1, LANE))]
    args += [w_o, norm_g.reshape(1, d), w_r, b_r]
    row = pl.BlockSpec((tm, d), lambda i: (i, 0))
    return pl.pallas_call(
        functools.partial(_wo_kernel, n_tiles_a=na, n_groups=n_groups, per_group=per_group),
        grid=(na + nb,),
        in_specs=in_specs,
        out_specs=[row, row, pl.BlockSpec((tm, LANE), lambda i: (i, 0))],
        out_shape=[
            jax.ShapeDtypeStruct((ma + mb, d), F32),
            jax.ShapeDtypeStruct((ma + mb, d), F32),
            jax.ShapeDtypeStruct((ma + mb, LANE), F32),
        ],
        compiler_params=_params("arbitrary"),
        name="wo_norm_router",
    )(*args)


RING_SLOTS = 3
DMA_QUEUES = 2


def _gather_start(idx_ref, src_hbm, dst, slot, sem, n_rows, first=0):
    for r in range(first, first + n_rows):
        pltpu.make_async_copy(src_hbm.at[pl.ds(idx_ref[0, 0, r], 1)], dst.at[slot, pl.ds(r, 1)], sem.at[slot]).start(
            priority=r % DMA_QUEUES)


def _gather_wait(src_hbm, dst, slot, sem, n_rows):
    pltpu.make_async_copy(src_hbm.at[pl.ds(0, n_rows)], dst.at[slot], sem.at[slot]).wait()


def _expert_kernel(te_ref, tv_ref, tok0_ref, tok1_ref, tokn_ref, h2_hbm, w1_ref, w3_ref, w2_ref, o_ref, xbuf, w1b, w3b, w2b, sem):
    i = pl.program_id(0)
    tm = xbuf.shape[1]
    slot = lax.rem(i, RING_SLOTS)
    valid = tv_ref[i] == 1
    started = jnp.where(i < 2, tv_ref[0], tv_ref[jnp.maximum(i - 2, 0)]) == 1

    @pl.when(jnp.logical_and(i == 0, valid))
    def _():
        _gather_start(tok0_ref, h2_hbm, xbuf, 0, sem, tm)
        _gather_start(tok1_ref, h2_hbm, xbuf, 1, sem, tm)

    @pl.when(valid)
    def _():
        _gather_wait(h2_hbm, xbuf, slot, sem, tm)

        @pl.when(jnp.logical_or(i == 0, te_ref[i] != te_ref[jnp.maximum(i - 1, 0)]))
        def _():
            w1b[...] = w1_ref[0].astype(BF16)
            w3b[...] = w3_ref[0].astype(BF16)
            w2b[...] = w2_ref[0].astype(BF16)

        nslot = lax.rem(i + 2, RING_SLOTS)
        x = xbuf[slot].astype(BF16)
        a = jnp.dot(x, w1b[...], preferred_element_type=F32)
        _gather_start(tokn_ref, h2_hbm, xbuf, nslot, sem, tm // 2)
        b = jnp.dot(x, w3b[...], preferred_element_type=F32)
        _gather_start(tokn_ref, h2_hbm, xbuf, nslot, sem, tm // 2, first=tm // 2)
        hm = (jax.nn.silu(a) * b).astype(BF16)
        o_ref[...] = jnp.dot(hm, w2b[...], preferred_element_type=F32)

    @pl.when(jnp.logical_not(valid))
    def _():
        @pl.when(started)
        def _():
            _gather_wait(h2_hbm, xbuf, slot, sem, tm)

        o_ref[...] = jnp.zeros(o_ref.shape, F32)


def _experts(h2_all, tile_expert, tile_valid, tok_sorted, w1, w3, w2, tm):
    n_tiles = tile_expert.shape[0]
    _, d, de = w1.shape
    tok3 = tok_sorted.reshape(n_tiles, 1, tm)
    assert n_tiles >= RING_SLOTS
    idx_spec = lambda tile: pl.BlockSpec((1, 1, tm), lambda i, te, tv: (tile(i), 0, 0), memory_space=pltpu.SMEM)
    grid_spec = pltpu.PrefetchScalarGridSpec(
        num_scalar_prefetch=2,
        grid=(n_tiles,),
        in_specs=[
            idx_spec(lambda i: 0),
            idx_spec(lambda i: 1),
            idx_spec(lambda i: jnp.minimum(i + 2, n_tiles - 1)),
            pl.BlockSpec(memory_space=pl.ANY),
            pl.BlockSpec((1, d, de), lambda i, te, tv: (te[i], 0, 0)),
            pl.BlockSpec((1, d, de), lambda i, te, tv: (te[i], 0, 0)),
            pl.BlockSpec((1, de, d), lambda i, te, tv: (te[i], 0, 0)),
        ],
        out_specs=pl.BlockSpec((tm, d), lambda i, te, tv: (i, 0)),
        scratch_shapes=[
            pltpu.VMEM((RING_SLOTS, tm, d), F32),
            pltpu.VMEM((d, de), BF16),
            pltpu.VMEM((d, de), BF16),
            pltpu.VMEM((de, d), BF16),
            pltpu.SemaphoreType.DMA((RING_SLOTS,)),
        ],
    )
    return pl.pallas_call(
        _expert_kernel,
        grid_spec=grid_spec,
        out_shape=jax.ShapeDtypeStruct((n_tiles * tm, d), F32),
        compiler_params=_params("arbitrary"),
        name="experts",
    )(tile_expert, tile_valid, tok3, tok3, tok3, h2_all, w1, w3, w2)


COMBINE_CHUNKS = 4


def _combine_kernel(d_ref, dn_ref, ys_hbm, x1_ref, rt_ref, g2_ref, ng_ref, y_ref, ybuf, sem):
    i = pl.program_id(0)
    tm = x1_ref.shape[0]
    slot = lax.rem(i, 2)

    @pl.when(i == 0)
    def _():
        _gather_start(d_ref, ys_hbm, ybuf, 0, sem, 2 * tm)

    _gather_wait(ys_hbm, ybuf, slot, sem, 2 * tm)
    g2 = _bc(g2_ref)
    ng = ng_ref[...]
    ck = tm // COMBINE_CHUNKS
    for c in range(COMBINE_CHUNKS):
        _gather_start(dn_ref, ys_hbm, ybuf, 1 - slot, sem, 2 * ck, first=2 * ck * c)
        rows = slice(c * ck, (c + 1) * ck)
        rt = rt_ref[rows, :]
        y0 = ybuf[slot, c * ck:(c + 1) * ck]
        y1 = ybuf[slot, tm + c * ck:tm + (c + 1) * ck]
        moe = rt[:, ROUTE_W0:ROUTE_W0 + 1] * y0 + rt[:, ROUTE_W1:ROUTE_W1 + 1] * y1
        g2c = g2 if g2.shape[0] == 1 else g2[rows]
        y_ref[rows, :] = _rms(x1_ref[rows, :] + g2c * moe, ng)

    @pl.when(i == pl.num_programs(0) - 1)
    def _():
        _gather_wait(ys_hbm, ybuf, 1 - slot, sem, 2 * tm)


def _combine(ys, dest, x1_all, route_all, row_off, m, gate2, normf_g, seq_len):
    d = x1_all.shape[1]
    tm = _tile(seq_len, 256, 8) if seq_len >= 256 else _tile(m, 256, seq_len)
    assert row_off % tm == 0 and tm % COMBINE_CHUNKS == 0
    nb, off = m // tm, row_off // tm
    d3 = dest.reshape(2, nb, tm).transpose(1, 0, 2).reshape(nb, 1, 2 * tm)
    d3 = jnp.concatenate([d3, jnp.zeros((1, 1, 2 * tm), I32)], axis=0)
    g2, g2_spec = _rowvec(gate2, seq_len, tm)
    return pl.pallas_call(
        _combine_kernel,
        grid=(nb,),
        in_specs=[
            pl.BlockSpec((1, 1, 2 * tm), lambda i: (i, 0, 0), memory_space=pltpu.SMEM),
            pl.BlockSpec((1, 1, 2 * tm), lambda i: (i + 1, 0, 0), memory_space=pltpu.SMEM),
            pl.BlockSpec(memory_space=pl.ANY),
            pl.BlockSpec((tm, d), lambda i: (off + i, 0)),
            pl.BlockSpec((tm, LANE), lambda i: (off + i, 0)),
            g2_spec,
            pl.BlockSpec((1, d), lambda i: (0, 0)),
        ],
        out_specs=pl.BlockSpec((tm, d), lambda i: (i, 0)),
        out_shape=jax.ShapeDtypeStruct((m, d), F32),
        scratch_shapes=[pltpu.VMEM((2, 2 * tm, d), F32), pltpu.SemaphoreType.DMA((2,))],
        compiler_params=_params("arbitrary"),
        name="combine_final_norm",
    )(d3, d3, ys, x1_all, route_all, g2, normf_g.reshape(1, d))


def _route_plan(route, n_experts, tm):
    t = route.shape[0]
    s = TOP_K * t
    e_flat = jnp.concatenate([route[:, ROUTE_ID0], route[:, ROUTE_ID1]]).astype(I32)
    onehot = (e_flat[:, None] == jnp.arange(n_experts, dtype=I32)[None, :]).astype(I32)
    csum = jnp.cumsum(onehot, axis=0)
    counts = csum[-1]
    padded = (counts + tm - 1) // tm * tm
    ends = jnp.cumsum(padded)
    dest = jnp.sum(onehot * (csum - 1 + (ends - padded)[None, :]), axis=1)
    n_tiles = pl.cdiv(s, tm) + n_experts + 2
    tile_start = jnp.arange(n_tiles, dtype=I32) * tm
    tile_valid = (tile_start < ends[-1]).astype(I32)
    te = jnp.minimum(jnp.sum((ends[None, :] <= tile_start[:, None]).astype(I32), axis=1), n_experts - 1)
    last_used = jnp.maximum(ends[-1] // tm - 1, 0)
    tile_expert = jnp.where(tile_valid == 1, te, te[last_used])
    tok_sorted = jnp.zeros((n_tiles * tm,), I32).at[dest].set(
        jnp.arange(s, dtype=I32) % t, unique_indices=True, mode="promise_in_bounds")
    return dest.reshape(TOP_K, t), tok_sorted, tile_expert, tile_valid


def kernel(x_prompt, x_sample, cache_k, cache_v, cache_logf, state_conv, c_prompt, c_sample, w_ada, b_ada, norm1_g, norm2_g, normf_g, w_in, conv_w, conv_b, b_forget, w_br_conv, w_br_attn, w_gate, b_gate, w_o, w_rg, b_rg, w_re, b_re, w1, w3, w2):
    depth = w_ada.shape[0]
    assert depth == 1, "single-layer step"
    l = 0
    bp, sp, d = x_prompt.shape
    bs, ss, _ = x_sample.shape
    _, _, past, n_heads, dh = cache_k.shape
    d_conv = state_conv.shape[-1]
    da = n_heads * dh
    n_groups = w_rg.shape[-1]
    n_experts = w_re.shape[-1]
    per_group = n_experts // n_groups
    assert n_groups + n_experts <= LANE
    scale = LOG2_E * dh ** -0.5

    w_in_bf = w_in[l].astype(BF16)
    q_off = 3 * d_conv
    w_q, w_k, w_v = (w_in_bf[:, q_off + j * da:q_off + (j + 1) * da] for j in range(3))
    hp = _round_up(n_heads, BF16_ROWS)
    wf_t = jnp.zeros((hp, d), BF16).at[:n_heads].set(w_in[l][:, 3 * d_conv + 3 * da:].T.astype(BF16))
    b_f = jnp.zeros((hp, 1), F32).at[:n_heads, 0].set(b_forget[l])
    w_brc_bf = w_br_conv[l].astype(BF16)
    w_bra_bf = w_br_attn[l].astype(BF16)
    w_gate_bf = w_gate[l].astype(BF16)
    b_gate2 = b_gate[l].reshape(1, 2 * d)
    w_o_bf = w_o[l].astype(BF16)
    w_r32 = jnp.pad(jnp.concatenate([w_rg[l], w_re[l]], axis=1), ((0, 0), (0, LANE - n_groups - n_experts)))
    w_r_hi = w_r32.astype(BF16)
    w_r = jnp.concatenate([w_r_hi, (w_r32 - w_r_hi.astype(F32)).astype(BF16)], axis=1)
    b_r = jnp.zeros((1, LANE), F32).at[0, :n_groups].set(b_rg[l]).at[0, n_groups:n_groups + n_experts].set(b_re[l])

    mod = _ada(jnp.concatenate([c_prompt, c_sample], axis=0), w_ada[l], b_ada[l])
    mods = {"p": jnp.split(mod[:bp], 6, axis=-1), "s": jnp.split(mod[bp:], 6, axis=-1)}

    def mixer_inputs(x, mod6, seq_len, prev):
        shift1, scale1 = mod6[0], mod6[1]
        h = _norm(x, norm1_g[l], shift1, scale1, seq_len)
        y_conv, tail = _conv_branch(h, w_in_bf, conv_w[l], conv_b[l], prev, seq_len, d_conv)
        q = _proj(h, w_q, BF16, scale)
        k = _proj(h, w_k, F32)
        v = _proj(h, w_v, F32)
        lf_t = _forget(h, wf_t, b_f)[:n_heads]
        return h, y_conv, tail, q, k, v, lf_t

    xp = x_prompt.reshape(bp * sp, d)
    xs = x_sample.reshape(bs * ss, d)
    hp_, ycp, tailp, qp, kp, vp, lftp = mixer_inputs(xp, mods["p"], sp, None)
    hs_, ycs, tails, qs, ks, vs, lfts = mixer_inputs(xs, mods["s"], ss, state_conv[l])

    lf_p = lftp.reshape(n_heads, bp, sp).transpose(1, 0, 2).reshape(bp * n_heads, sp)
    f_p = _cumsum_lanes(lf_p).reshape(bp * n_heads, 1, sp)
    yap = _attn_prompt(qp, kp, vp, f_p, bp, sp, n_heads, dh)

    lf_new = lfts.reshape(n_heads, bs, ss).transpose(1, 0, 2)
    lf_past = cache_logf[l].astype(F32).transpose(0, 2, 1)
    tot = _round_up(past + LANE, CUMSUM_CHUNK)
    lf_all = jnp.concatenate([lf_past, lf_new, jnp.zeros((bs, n_heads, tot - past - ss), F32)], axis=-1)
    f_s = _cumsum_lanes(lf_all.reshape(bs * n_heads, tot)).reshape(bs, n_heads, tot)
    yas = _attn_sample(qs, cache_k, cache_v, l, ks, vs, f_s, bs, ss, past, n_heads, dh)

    def wo_path(x, mod6, seq_len, h, yc, ya):
        gate1, shift2, scale2 = mod6[2], mod6[3], mod6[4]
        merged = _merge(yc, ya, h, w_brc_bf, w_bra_bf, w_gate_bf, b_gate2)
        return merged, x, gate1, shift2, scale2, seq_len

    x1_all, h2_all, route_all = _wo(
        wo_path(xp, mods["p"], sp, hp_, ycp, yap), wo_path(xs, mods["s"], ss, hs_, ycs, yas),
        w_o_bf, norm2_g[l], w_r, b_r, n_groups, per_group)
    tm_e = 256 if TOP_K * h2_all.shape[0] >= 8192 else 64
    dest, tok_sorted, tile_expert, tile_valid = _route_plan(route_all, n_experts, tm_e)
    ys = _experts(h2_all, tile_expert, tile_valid, tok_sorted, w1[l], w3[l], w2[l], tm_e)

    mp, ms = bp * sp, bs * ss
    y_p = _combine(ys, dest[:, :mp], x1_all, route_all, 0, mp, mods["p"][5], normf_g, sp)
    y_s = _combine(ys, dest[:, mp:], x1_all, route_all, mp, ms, mods["s"][5], normf_g, ss)

    def cache_outs(k, v, lf_t, tail, n_seq, seq_len):
        return (
            k.reshape(1, n_seq, seq_len, n_heads, dh),
            v.reshape(1, n_seq, seq_len, n_heads, dh),
            lf_t.T.reshape(1, n_seq, seq_len, n_heads),
            tail[:, CONV_TAIL_ROWS - 2:, :].reshape(1, n_seq, 2, d_conv),
        )

    return (y_p.reshape(bp, sp, d), y_s.reshape(bs, ss, d)) + cache_outs(kp, vp, lftp, tailp, bp, sp) + cache_outs(ks, vs, lfts, tails, bs, ss)
```

```python
import functools

import jax
import jax.numpy as jnp
from jax import lax
from jax.experimental import pallas as pl
from jax.experimental.pallas import tpu as pltpu

F32 = jnp.float32
BF16 = jnp.bfloat16
I32 = jnp.int32

EPS = 1e-6
NEG_INF = -1e30
LOG2_E = 1.4426950408889634
TOP_K = 2
LANE = 128
BF16_ROWS = 16
VMEM_LIMIT_BYTES = 56 * 1024 * 1024
NT_DIMS = (((1,), (1,)), ((), ()))


def _tile(n, pref, align):
    if n <= pref:
        return n
    t = (pref // align) * align
    while t >= align:
        if n % t == 0:
            return t
        t -= align
    return n


def _round_up(n, m):
    return (n + m - 1) // m * m


def _params(*sem):
    return pltpu.CompilerParams(dimension_semantics=sem, vmem_limit_bytes=VMEM_LIMIT_BYTES)


def _rowvec(vec, seq_len, tm, tile=lambda i: i):
    n_seq, d = vec.shape
    vec = vec.reshape(n_seq, 1, d)
    if tm <= seq_len:
        per = seq_len // tm
        return vec, pl.BlockSpec((1, 1, d), lambda i, *_: (tile(i) // per, 0, 0))
    return vec, pl.BlockSpec((tm // seq_len, 1, d), lambda i, *_: (tile(i), 0, 0))


def _bc(ref, rows):
    v = ref[...]
    k, _, d = v.shape
    if k == 1:
        return v.reshape(1, d)
    return jnp.broadcast_to(v, (k, rows // k, d)).reshape(rows, d)


def _rms(x, g):
    return x * lax.rsqrt(jnp.mean(x * x, axis=-1, keepdims=True) + EPS) * g


def _ada_kernel(c_ref, w_ref, b_ref, o_ref):
    o_ref[...] = jnp.dot(c_ref[...].astype(BF16), w_ref[...].astype(BF16), preferred_element_type=F32) + b_ref[...]


def _ada(c_all, w, b):
    r, d = c_all.shape
    n = w.shape[1]
    tn = _tile(n, 1024, LANE)
    return pl.pallas_call(
        _ada_kernel,
        grid=(n // tn,),
        in_specs=[
            pl.BlockSpec((r, d), lambda j: (0, 0)),
            pl.BlockSpec((d, tn), lambda j: (0, j)),
            pl.BlockSpec((1, tn), lambda j: (0, j)),
        ],
        out_specs=pl.BlockSpec((r, tn), lambda j: (0, j)),
        out_shape=jax.ShapeDtypeStruct((r, n), F32),
        compiler_params=_params("parallel"),
        name="ada",
    )(c_all, w, b.reshape(1, n))


def _norm_kernel(x_ref, g_ref, sh_ref, sc_ref, wf_ref, bf_ref, h_ref, lf_ref):
    tm = x_ref.shape[0]
    h = _rms(x_ref[...], g_ref[...]) * (1.0 + _bc(sc_ref, tm)) + _bc(sh_ref, tm)
    hb = h.astype(h_ref.dtype)
    h_ref[...] = hb
    fl = lax.dot_general(wf_ref[...], hb, NT_DIMS, preferred_element_type=F32)
    lf_ref[...] = jax.nn.log_sigmoid(fl + bf_ref[...])


def _norm(x, g, shift, scale, wf_t, b_f, seq_len):
    m, d = x.shape
    hp = wf_t.shape[0]
    tm = _tile(seq_len, 512, 8) if seq_len >= 512 else _tile(m, 512, seq_len)
    sh, sh_spec = _rowvec(shift, seq_len, tm)
    sc, sc_spec = _rowvec(scale, seq_len, tm)
    const = lambda shape: pl.BlockSpec(shape, lambda i: (0, 0))
    return pl.pallas_call(
        _norm_kernel,
        grid=(m // tm,),
        in_specs=[pl.BlockSpec((tm, d), lambda i: (i, 0)), const((1, d)), sh_spec, sc_spec, const((hp, d)), const((hp, 1))],
        out_specs=[pl.BlockSpec((tm, d), lambda i: (i, 0)), pl.BlockSpec((hp, tm), lambda i: (0, i))],
        out_shape=[jax.ShapeDtypeStruct((m, d), BF16), jax.ShapeDtypeStruct((hp, m), F32)],
        compiler_params=_params("parallel"),
        name="norm_forget",
    )(x, g.reshape(1, d), sh, sc, wf_t, b_f)


def _proj_kernel(a_ref, w_ref, o_ref, *, scale):
    acc = jnp.dot(a_ref[...], w_ref[...], preferred_element_type=F32)
    if scale != 1.0:
        acc = acc * scale
    o_ref[...] = acc.astype(o_ref.dtype)


RESIDENT_WEIGHT_BYTES = 8 * 1024 * 1024


def _proj(a, w, out_dtype, scale=1.0):
    m, k = a.shape
    n = w.shape[1]
    if k * n * w.dtype.itemsize <= RESIDENT_WEIGHT_BYTES:
        tm, tn = _tile(m, 512, BF16_ROWS), n
    else:
        tm, tn = _tile(m, 1024, BF16_ROWS), _tile(n, 512, LANE)
    return pl.pallas_call(
        functools.partial(_proj_kernel, scale=scale),
        grid=(m // tm, n // tn),
        in_specs=[pl.BlockSpec((tm, k), lambda i, j: (i, 0)), pl.BlockSpec((k, tn), lambda i, j: (0, j))],
        out_specs=pl.BlockSpec((tm, tn), lambda i, j: (i, j)),
        out_shape=jax.ShapeDtypeStruct((m, n), out_dtype),
        compiler_params=_params("parallel", "parallel"),
        name="proj",
    )(a, w)


CONV_TAIL_ROWS = 8


def _conv_kernel(*refs, seq_len, has_prev):
    if has_prev:
        h_ref, wb_ref, wc_ref, wv_ref, cw_ref, cb_ref, p1_ref, p2_ref, y_ref, tail_ref = refs
    else:
        h_ref, wb_ref, wc_ref, wv_ref, cw_ref, cb_ref, y_ref, tail_ref = refs
    h = h_ref[...]
    bg = jnp.dot(h, wb_ref[...], preferred_element_type=F32)
    cg = jnp.dot(h, wc_ref[...], preferred_element_type=F32)
    vc = jnp.dot(h, wv_ref[...], preferred_element_type=F32)
    u = cg * vc
    tm = u.shape[0]
    t = lax.broadcasted_iota(I32, (tm, 1), 0)
    if tm > seq_len:
        t = jnp.bitwise_and(t, seq_len - 1) if seq_len & (seq_len - 1) == 0 else lax.rem(t, seq_len)
    u1 = jnp.where(t >= 1, pltpu.roll(u, 1, 0), p1_ref[...] if has_prev else 0.0)
    u2 = jnp.where(t >= 2, pltpu.roll(u, 2, 0), p2_ref[...] if has_prev else 0.0)
    cw = cw_ref[...]
    conv = cb_ref[...] + ((cw[0:1] * u2 + cw[1:2] * u1) + cw[2:3] * u)
    y_ref[...] = (bg * conv).astype(y_ref.dtype)
    for s in range(tm // seq_len):
        tail_ref[s] = u[(s + 1) * seq_len - CONV_TAIL_ROWS:(s + 1) * seq_len, :]


def _conv_branch(h, w_in_bf, conv_w, conv_b, prev, seq_len, d_conv):
    m, d = h.shape
    assert conv_w.shape[0] == 3 and seq_len >= CONV_TAIL_ROWS
    tm = seq_len if seq_len >= 512 else m
    assert m % tm == 0 and tm % seq_len == 0
    tc = _tile(d_conv, 256, LANE)
    nc = d_conv // tc
    n_seq = m // seq_len
    seq_per_tile = tm // seq_len
    in_specs = [
        pl.BlockSpec((tm, d), lambda i, j: (i, 0)),
        pl.BlockSpec((d, tc), lambda i, j: (0, j)),
        pl.BlockSpec((d, tc), lambda i, j: (0, nc + j)),
        pl.BlockSpec((d, tc), lambda i, j: (0, 2 * nc + j)),
        pl.BlockSpec((3, tc), lambda i, j: (0, j)),
        pl.BlockSpec((1, tc), lambda i, j: (0, j)),
    ]
    args = [h, w_in_bf, w_in_bf, w_in_bf, conv_w, conv_b.reshape(1, d_conv)]
    if prev is not None:
        zeros = jnp.zeros((n_seq, seq_len, d_conv), F32)
        p1 = zeros.at[:, 0].set(prev[:, 1]).reshape(m, d_conv)
        p2 = zeros.at[:, 0].set(prev[:, 0]).at[:, 1].set(prev[:, 1]).reshape(m, d_conv)
        in_specs += [pl.BlockSpec((tm, tc), lambda i, j: (i, j))] * 2
        args += [p1, p2]
    return pl.pallas_call(
        functools.partial(_conv_kernel, seq_len=seq_len, has_prev=prev is not None),
        grid=(m // tm, nc),
        in_specs=in_specs,
        out_specs=[
            pl.BlockSpec((tm, tc), lambda i, j: (i, j)),
            pl.BlockSpec((seq_per_tile, CONV_TAIL_ROWS, tc), lambda i, j: (i, 0, j)),
        ],
        out_shape=[
            jax.ShapeDtypeStruct((m, d_conv), BF16),
            jax.ShapeDtypeStruct((n_seq, CONV_TAIL_ROWS, d_conv), F32),
        ],
        compiler_params=_params("parallel", "parallel"),
        name="conv_branch",
    )(*args)


CUMSUM_CHUNK = LANE


def _cumsum_kernel(x_ref, o_ref):
    r, n = x_ref.shape
    c = CUMSUM_CHUNK
    tri = (lax.broadcasted_iota(I32, (c, c), 0) <= lax.broadcasted_iota(I32, (c, c), 1)).astype(BF16)
    carry = jnp.zeros((r, 1), F32)
    for j in range(n // c):
        x = x_ref[:, j * c:(j + 1) * c]
        cs = carry
        for _ in range(3):
            piece = x.astype(BF16)
            cs = cs + jnp.dot(piece, tri, preferred_element_type=F32)
            x = x - piece.astype(F32)
        o_ref[:, j * c:(j + 1) * c] = cs
        carry = cs[:, c - 1:c]


def _cumsum_lanes(x):
    r, n = x.shape
    tr = _tile(r, 128, 8)
    return pl.pallas_call(
        _cumsum_kernel,
        grid=(r // tr,),
        in_specs=[pl.BlockSpec((tr, n), lambda i: (i, 0))],
        out_specs=pl.BlockSpec((tr, n), lambda i: (i, 0)),
        out_shape=jax.ShapeDtypeStruct((r, n), F32),
        compiler_params=_params("parallel"),
        name="cumsum",
    )(x)


def _attn_prompt_kernel(q_ref, k_ref, v_ref, f_ref, o_ref, *, tq, dh):
    seq_len = q_ref.shape[0]
    heads = q_ref.shape[1] // dh
    ones_col = (lax.broadcasted_iota(I32, (seq_len, dh), 1) == 0).astype(BF16)
    kb, vb, f = [], [], []
    for h in range(heads):
        cols = slice(h * dh, (h + 1) * dh)
        kb.append(k_ref[:, cols].astype(BF16))
        vb.append(jnp.concatenate([v_ref[:, cols].astype(BF16), ones_col], axis=1))
        f.append(f_ref[h] * LOG2_E)
    causal = lax.broadcasted_iota(I32, (tq, tq), 1) <= lax.broadcasted_iota(I32, (tq, tq), 0)
    for qi in range(seq_len // tq):
        lo = qi * tq
        for h in range(heads):
            cols = slice(h * dh, (h + 1) * dh)
            q = q_ref[lo:lo + tq, cols]
            sd = lax.dot_general(q, kb[h][lo:lo + tq], NT_DIMS, preferred_element_type=F32) - f[h][:, lo:lo + tq]
            sd = jnp.where(causal, sd, NEG_INF)
            m = jnp.max(sd, axis=-1, keepdims=True)
            if qi > 0:
                sp = lax.dot_general(q, kb[h][:lo], NT_DIMS, preferred_element_type=F32) - f[h][:, :lo]
                m = jnp.maximum(m, jnp.max(sp, axis=-1, keepdims=True))
                ol = jnp.dot(jnp.exp2(sp - m).astype(BF16), vb[h][:lo], preferred_element_type=F32)
            old = jnp.dot(jnp.exp2(sd - m).astype(BF16), vb[h][lo:lo + tq], preferred_element_type=F32)
            ol = ol + old if qi > 0 else old
            o_ref[lo:lo + tq, cols] = (ol[:, :dh] / ol[:, dh:dh + 1]).astype(o_ref.dtype)


ATTN_HEADS_PER_STEP = 4


def _attn_prompt(q, k, v, f, n_seq, seq_len, n_heads, dh):
    m = n_seq * seq_len
    tq = _tile(seq_len, 256, LANE)
    hps = ATTN_HEADS_PER_STEP if n_heads % ATTN_HEADS_PER_STEP == 0 else 1
    groups = n_heads // hps
    blk = pl.BlockSpec((seq_len, hps * dh), lambda b, g: (b, g))
    return pl.pallas_call(
        functools.partial(_attn_prompt_kernel, tq=tq, dh=dh),
        grid=(n_seq, groups),
        in_specs=[blk, blk, blk, pl.BlockSpec((hps, 1, seq_len), lambda b, g: (b * groups + g, 0, 0))],
        out_specs=blk,
        out_shape=jax.ShapeDtypeStruct((m, n_heads * dh), BF16),
        compiler_params=_params("parallel", "parallel"),
        name="attn_prompt",
    )(q, k, v, f)


def _attn_sample_kernel(q_ref, kc_ref, vc_ref, fp_ref, kn_ref, vn_ref, fn_ref, o_ref, m_sc, l_sc, acc_sc, *, n_heads, dh):
    kt = pl.program_id(1)
    lq = q_ref.shape[0]

    @pl.when(kt == 0)
    def _():
        m_sc[...] = jnp.full(m_sc.shape, NEG_INF, F32)
        l_sc[...] = jnp.zeros(l_sc.shape, F32)
        acc_sc[...] = jnp.zeros(acc_sc.shape, F32)

    q = q_ref[...]

    def update(k_heads, v_heads, fk, mask):
        s_heads = []
        for h in range(n_heads):
            s = lax.dot_general(q[:, h * dh:(h + 1) * dh], k_heads[h], NT_DIMS, preferred_element_type=F32)
            s = s - fk[h:h + 1, :]
            if mask is not None:
                s = jnp.where(mask, s, NEG_INF)
            s_heads.append(s)
        s = jnp.concatenate(s_heads, axis=0)
        m_prev = m_sc[...]
        m_new = jnp.maximum(m_prev, jnp.max(s, axis=-1, keepdims=True))
        alpha = jnp.exp2(m_prev - m_new)
        p = jnp.exp2(s - m_new)
        l_sc[...] = alpha * l_sc[...] + jnp.sum(p, axis=-1, keepdims=True)
        pb = p.astype(BF16)
        o_heads = [jnp.dot(pb[h * lq:(h + 1) * lq], v_heads[h], preferred_element_type=F32) for h in range(n_heads)]
        acc_sc[...] = alpha * acc_sc[...] + jnp.concatenate(o_heads, axis=0)
        m_sc[...] = m_new

    def cached_heads(ref):
        x = jnp.swapaxes(ref[0, 0].astype(BF16), 0, 1)
        return [x[h] for h in range(n_heads)]

    update(cached_heads(kc_ref), cached_heads(vc_ref), fp_ref[0] * LOG2_E, None)

    @pl.when(kt == pl.num_programs(1) - 1)
    def _():
        pad = jnp.zeros((LANE - lq, n_heads * dh), BF16)
        kn = jnp.concatenate([kn_ref[...].astype(BF16), pad], axis=0)
        vn = jnp.concatenate([vn_ref[...].astype(BF16), pad], axis=0)
        mask = lax.broadcasted_iota(I32, (lq, LANE), 1) <= lax.broadcasted_iota(I32, (lq, LANE), 0)
        heads = lambda x: [x[:, h * dh:(h + 1) * dh] for h in range(n_heads)]
        update(heads(kn), heads(vn), fn_ref[0] * LOG2_E, mask)
        out = acc_sc[...] / l_sc[...]
        o_ref[...] = jnp.concatenate([out[h * lq:(h + 1) * lq] for h in range(n_heads)], axis=1).astype(o_ref.dtype)


def _attn_sample(q, cache_k, cache_v, layer, k_new, v_new, f_all, n_seq, lq, past, n_heads, dh):
    da = n_heads * dh
    tk = _tile(past, 512, LANE)
    assert lq <= LANE and lq % BF16_ROWS == 0 and past % LANE == 0
    qblk = pl.BlockSpec((lq, da), lambda b, t: (b, 0))
    cblk = pl.BlockSpec((1, 1, tk, n_heads, dh), lambda b, t: (layer, b, t, 0, 0))
    return pl.pallas_call(
        functools.partial(_attn_sample_kernel, n_heads=n_heads, dh=dh),
        grid=(n_seq, past // tk),
        in_specs=[
            qblk,
            cblk,
            cblk,
            pl.BlockSpec((1, n_heads, tk), lambda b, t: (b, 0, t)),
            qblk,
            qblk,
            pl.BlockSpec((1, n_heads, LANE), lambda b, t: (b, 0, past // LANE)),
        ],
        out_specs=qblk,
        out_shape=jax.ShapeDtypeStruct((n_seq * lq, da), BF16),
        scratch_shapes=[
            pltpu.VMEM((n_heads * lq, 1), F32),
            pltpu.VMEM((n_heads * lq, 1), F32),
            pltpu.VMEM((n_heads * lq, dh), F32),
        ],
        compiler_params=_params("parallel", "arbitrary"),
        name="attn_sample",
    )(q, cache_k, cache_v, f_all, k_new, v_new, f_all)


def _merge_kernel(yc_ref, ya_ref, h_ref, wc_ref, wa_ref, wgc_ref, wga_ref, bgc_ref, bga_ref, o_ref):
    h = h_ref[...]
    a = jnp.dot(yc_ref[...], wc_ref[...], preferred_element_type=F32)
    b = jnp.dot(ya_ref[...], wa_ref[...], preferred_element_type=F32)
    gc = jax.nn.sigmoid(jnp.dot(h, wgc_ref[...], preferred_element_type=F32) + bgc_ref[...])
    ga = jax.nn.sigmoid(jnp.dot(h, wga_ref[...], preferred_element_type=F32) + bga_ref[...])
    o_ref[...] = (gc * a + ga * b).astype(o_ref.dtype)


def _merge(yc, ya, h, w_brc, w_bra, w_gate, b_gate):
    m, d = h.shape
    dc, da = yc.shape[1], ya.shape[1]
    tm = _tile(m, 1024, BF16_ROWS)
    tn = _tile(d, 512, LANE)
    nj = d // tn
    return pl.pallas_call(
        _merge_kernel,
        grid=(m // tm, nj),
        in_specs=[
            pl.BlockSpec((tm, dc), lambda i, j: (i, 0)),
            pl.BlockSpec((tm, da), lambda i, j: (i, 0)),
            pl.BlockSpec((tm, d), lambda i, j: (i, 0)),
            pl.BlockSpec((dc, tn), lambda i, j: (0, j)),
            pl.BlockSpec((da, tn), lambda i, j: (0, j)),
            pl.BlockSpec((d, tn), lambda i, j: (0, j)),
            pl.BlockSpec((d, tn), lambda i, j: (0, nj + j)),
            pl.BlockSpec((1, tn), lambda i, j: (0, j)),
            pl.BlockSpec((1, tn), lambda i, j: (0, nj + j)),
        ],
        out_specs=pl.BlockSpec((tm, tn), lambda i, j: (i, j)),
        out_shape=jax.ShapeDtypeStruct((m, d), BF16),
        compiler_params=_params("parallel", "parallel"),
        name="merge",
    )(yc, ya, h, w_brc, w_bra, w_gate, w_gate, b_gate, b_gate)


ROUTE_ID0, ROUTE_ID1, ROUTE_W0, ROUTE_W1 = 0, 1, 2, 3


def _wo_kernel(*refs, n_tiles_a, n_groups, per_group):
    a_refs, b_refs, (wo_ref, ng_ref, wr_ref, br_ref), outs = refs[0:5], refs[5:10], refs[10:14], refs[14:17]
    i = pl.program_id(0)

    @pl.when(i < n_tiles_a)
    def _():
        _wo_tile(*a_refs, wo_ref, ng_ref, wr_ref, br_ref, *outs, n_groups=n_groups, per_group=per_group)

    @pl.when(i >= n_tiles_a)
    def _():
        _wo_tile(*b_refs, wo_ref, ng_ref, wr_ref, br_ref, *outs, n_groups=n_groups, per_group=per_group)


WO_SUBTILES = 2


def _wo_tile(mg_ref, x_ref, g1_ref, sh_ref, sc_ref, wo_ref, ng_ref, wr_ref, br_ref, x1_ref, h2_ref, rt_ref, *, n_groups, per_group):
    tm = x_ref.shape[0]
    sub = tm // WO_SUBTILES if tm % (WO_SUBTILES * BF16_ROWS) == 0 else tm
    for r0 in range(0, tm, sub):
        rows = slice(r0, r0 + sub)
        vec = lambda ref: _bc(ref, tm) if ref.shape[0] == 1 else _bc(ref, tm)[rows, :]
        _wo_rows(mg_ref[rows, :], x_ref[rows, :], vec(g1_ref), vec(sh_ref), vec(sc_ref), wo_ref, ng_ref, wr_ref, br_ref,
                 x1_ref.at[rows, :], h2_ref.at[rows, :], rt_ref.at[rows, :], n_groups=n_groups, per_group=per_group)


def _wo_rows(mg, x, g1, sh, sc, wo_ref, ng_ref, wr_ref, br_ref, x1_ref, h2_ref, rt_ref, *, n_groups, per_group):
    o = jnp.dot(mg, wo_ref[...], preferred_element_type=F32)
    x1 = x + g1 * o
    x1_ref[...] = x1
    h2 = _rms(x1, ng_ref[...]) * (1.0 + sc) + sh
    h2_ref[...] = h2
    wr = wr_ref[...]
    hi = h2.astype(BF16)
    lo = (h2 - hi.astype(F32)).astype(BF16)
    r_hi = jnp.dot(hi, wr, preferred_element_type=F32)
    r_lo = jnp.dot(lo, wr[:, :LANE], preferred_element_type=F32)
    lg = r_hi[:, :LANE] + (r_hi[:, LANE:] + r_lo) + br_ref[...]
    col = lax.broadcasted_iota(I32, lg.shape, 1).astype(F32)
    big = float(LANE)
    gl = jnp.where(col < n_groups, lg, NEG_INF)
    gmax = jnp.max(gl, axis=-1, keepdims=True)
    g_val = 1.0 / jnp.sum(jnp.exp(gl - gmax), axis=-1, keepdims=True)
    g_idx = jnp.min(jnp.where(gl == gmax, col, big), axis=-1, keepdims=True)
    lo = n_groups + g_idx * per_group
    el = jnp.where((col >= lo) & (col < lo + per_group), lg, NEG_INF)
    m1 = jnp.max(el, axis=-1, keepdims=True)
    i1 = jnp.min(jnp.where(el == m1, col, big), axis=-1, keepdims=True)
    el2 = jnp.where(col == i1, NEG_INF, el)
    m2 = jnp.max(el2, axis=-1, keepdims=True)
    i2 = jnp.min(jnp.where(el2 == m2, col, big), axis=-1, keepdims=True)
    r = jnp.exp(m2 - m1)
    w0 = g_val / (1.0 + r)
    w1 = g_val * r / (1.0 + r)
    rt = jnp.where(col == ROUTE_ID0, i1 - n_groups, 0.0)
    rt = jnp.where(col == ROUTE_ID1, i2 - n_groups, rt)
    rt = jnp.where(col == ROUTE_W0, w0, rt)
    rt = jnp.where(col == ROUTE_W1, w1, rt)
    rt_ref[...] = rt


def _wo(path_a, path_b, w_o, norm_g, w_r, b_r, n_groups, per_group):
    d = w_o.shape[0]
    ma, mb = path_a[1].shape[0], path_b[1].shape[0]
    tm = _tile(min(ma, mb), 512, BF16_ROWS)
    assert ma % tm == 0 and mb % tm == 0
    na, nb = ma // tm, mb // tm
    tiles = (lambda i: jnp.minimum(i, na - 1), lambda i: jnp.maximum(i - na, 0))
    in_specs, args = [], []
    for (merged, x, gate1, shift2, scale2, seq_len), tile in zip((path_a, path_b), tiles):
        row = pl.BlockSpec((tm, d), lambda i, tile=tile: (tile(i), 0))
        in_specs += [row, row]
        args += [merged, x]
        for vec in (gate1, shift2, scale2):
            arr, spec = _rowvec(vec, seq_len, tm, tile)
            in_specs.append(spec)
            args.append(arr)
    const = lambda shape: pl.BlockSpec(shape, lambda i: (0, 0))
    in_specs += [pl.BlockSpec((d, d), lambda i: (0, 0), pipeline_mode=pl.Buffered(1)),
                 const((1, d)), const((d, 2 * LANE)), const((1, LANE))]
    args += [w_o, norm_g.reshape(1, d), w_r, b_r]
    row = pl.BlockSpec((tm, d), lambda i: (i, 0))
    return pl.pallas_call(
        functools.partial(_wo_kernel, n_tiles_a=na, n_groups=n_groups, per_group=per_group),
        grid=(na + nb,),
        in_specs=in_specs,
        out_specs=[row, row, pl.BlockSpec((tm, LANE), lambda i: (i, 0))],
        out_shape=[
            jax.ShapeDtypeStruct((ma + mb, d), F32),
            jax.ShapeDtypeStruct((ma + mb, d), F32),
            jax.ShapeDtypeStruct((ma + mb, LANE), F32),
        ],
        compiler_params=_params("arbitrary"),
        name="wo_norm_router",
    )(*args)


RING_SLOTS = 3
DMA_QUEUES = 2


def _gather_start(idx_ref, src_hbm, dst, slot, sem, n_rows, first=0):
    for r in range(first, first + n_rows):
        pltpu.make_async_copy(src_hbm.at[pl.ds(idx_ref[0, 0, r], 1)], dst.at[slot, pl.ds(r, 1)], sem.at[slot]).start(
            priority=r % DMA_QUEUES)


def _gather_wait(src_hbm, dst, slot, sem, n_rows):
    pltpu.make_async_copy(src_hbm.at[pl.ds(0, n_rows)], dst.at[slot], sem.at[slot]).wait()


def _expert_kernel(te_ref, tv_ref, tok0_ref, tok1_ref, tokn_ref, h2_hbm, w1_ref, w3_ref, w2_ref, o_ref, xbuf, w1b, w3b, w2b, sem):
    i = pl.program_id(0)
    tm = xbuf.shape[1]
    slot = lax.rem(i, RING_SLOTS)
    valid = tv_ref[i] == 1
    started = jnp.where(i < 2, tv_ref[0], tv_ref[jnp.maximum(i - 2, 0)]) == 1

    @pl.when(jnp.logical_and(i == 0, valid))
    def _():
        _gather_start(tok0_ref, h2_hbm, xbuf, 0, sem, tm)
        _gather_start(tok1_ref, h2_hbm, xbuf, 1, sem, tm)

    @pl.when(valid)
    def _():
        _gather_wait(h2_hbm, xbuf, slot, sem, tm)

        @pl.when(jnp.logical_or(i == 0, te_ref[i] != te_ref[jnp.maximum(i - 1, 0)]))
        def _():
            w1b[...] = w1_ref[0].astype(BF16)
            w3b[...] = w3_ref[0].astype(BF16)
            w2b[...] = w2_ref[0].astype(BF16)

        nslot = lax.rem(i + 2, RING_SLOTS)
        x = xbuf[slot].astype(BF16)
        a = jnp.dot(x, w1b[...], preferred_element_type=F32)
        _gather_start(tokn_ref, h2_hbm, xbuf, nslot, sem, tm // 2)
        b = jnp.dot(x, w3b[...], preferred_element_type=F32)
        _gather_start(tokn_ref, h2_hbm, xbuf, nslot, sem, tm // 2, first=tm // 2)
        hm = (jax.nn.silu(a) * b).astype(BF16)
        o_ref[...] = jnp.dot(hm, w2b[...], preferred_element_type=F32)

    @pl.when(jnp.logical_not(valid))
    def _():
        @pl.when(started)
        def _():
            _gather_wait(h2_hbm, xbuf, slot, sem, tm)

        o_ref[...] = jnp.zeros(o_ref.shape, F32)


def _experts(h2_all, tile_expert, tile_valid, tok_sorted, w1, w3, w2, tm):
    n_tiles = tile_expert.shape[0]
    _, d, de = w1.shape
    tok3 = tok_sorted.reshape(n_tiles, 1, tm)
    assert n_tiles >= RING_SLOTS
    idx_spec = lambda tile: pl.BlockSpec((1, 1, tm), lambda i, te, tv: (tile(i), 0, 0), memory_space=pltpu.SMEM)
    grid_spec = pltpu.PrefetchScalarGridSpec(
        num_scalar_prefetch=2,
        grid=(n_tiles,),
        in_specs=[
            idx_spec(lambda i: 0),
            idx_spec(lambda i: 1),
            idx_spec(lambda i: jnp.minimum(i + 2, n_tiles - 1)),
            pl.BlockSpec(memory_space=pl.ANY),
            pl.BlockSpec((1, d, de), lambda i, te, tv: (te[i], 0, 0)),
            pl.BlockSpec((1, d, de), lambda i, te, tv: (te[i], 0, 0)),
            pl.BlockSpec((1, de, d), lambda i, te, tv: (te[i], 0, 0)),
        ],
        out_specs=pl.BlockSpec((tm, d), lambda i, te, tv: (i, 0)),
        scratch_shapes=[
            pltpu.VMEM((RING_SLOTS, tm, d), F32),
            pltpu.VMEM((d, de), BF16),
            pltpu.VMEM((d, de), BF16),
            pltpu.VMEM((de, d), BF16),
            pltpu.SemaphoreType.DMA((RING_SLOTS,)),
        ],
    )
    return pl.pallas_call(
        _expert_kernel,
        grid_spec=grid_spec,
        out_shape=jax.ShapeDtypeStruct((n_tiles * tm, d), F32),
        compiler_params=_params("arbitrary"),
        name="experts",
    )(tile_expert, tile_valid, tok3, tok3, tok3, h2_all, w1, w3, w2)


COMBINE_CHUNKS = 4


def _combine_kernel(d_ref, dn_ref, ys_hbm, x1_ref, rt_ref, g2_ref, ng_ref, y_ref, ybuf, sem):
    i = pl.program_id(0)
    tm = x1_ref.shape[0]
    slot = lax.rem(i, 2)

    @pl.when(i == 0)
    def _():
        _gather_start(d_ref, ys_hbm, ybuf, 0, sem, 2 * tm)

    _gather_wait(ys_hbm, ybuf, slot, sem, 2 * tm)
    g2 = _bc(g2_ref, tm)
    ng = ng_ref[...]
    ck = tm // COMBINE_CHUNKS
    for c in range(COMBINE_CHUNKS):
        _gather_start(dn_ref, ys_hbm, ybuf, 1 - slot, sem, 2 * ck, first=2 * ck * c)
        rows = slice(c * ck, (c + 1) * ck)
        rt = rt_ref[rows, :]
        y0 = ybuf[slot, c * ck:(c + 1) * ck]
        y1 = ybuf[slot, tm + c * ck:tm + (c + 1) * ck]
        moe = rt[:, ROUTE_W0:ROUTE_W0 + 1] * y0 + rt[:, ROUTE_W1:ROUTE_W1 + 1] * y1
        g2c = g2 if g2.shape[0] == 1 else g2[rows]
        y_ref[rows, :] = _rms(x1_ref[rows, :] + g2c * moe, ng)

    @pl.when(i == pl.num_programs(0) - 1)
    def _():
        _gather_wait(ys_hbm, ybuf, 1 - slot, sem, 2 * tm)


def _combine(ys, dest, x1_all, route_all, row_off, m, gate2, normf_g, seq_len):
    d = x1_all.shape[1]
    tm = _tile(seq_len, 256, 8) if seq_len >= 256 else _tile(m, 256, seq_len)
    assert row_off % tm == 0 and tm % COMBINE_CHUNKS == 0
    nb, off = m // tm, row_off // tm
    d3 = dest.reshape(2, nb, tm).transpose(1, 0, 2).reshape(nb, 1, 2 * tm)
    d3 = jnp.concatenate([d3, jnp.zeros((1, 1, 2 * tm), I32)], axis=0)
    g2, g2_spec = _rowvec(gate2, seq_len, tm)
    return pl.pallas_call(
        _combine_kernel,
        grid=(nb,),
        in_specs=[
            pl.BlockSpec((1, 1, 2 * tm), lambda i: (i, 0, 0), memory_space=pltpu.SMEM),
            pl.BlockSpec((1, 1, 2 * tm), lambda i: (i + 1, 0, 0), memory_space=pltpu.SMEM),
            pl.BlockSpec(memory_space=pl.ANY),
            pl.BlockSpec((tm, d), lambda i: (off + i, 0)),
            pl.BlockSpec((tm, LANE), lambda i: (off + i, 0)),
            g2_spec,
            pl.BlockSpec((1, d), lambda i: (0, 0)),
        ],
        out_specs=pl.BlockSpec((tm, d), lambda i: (i, 0)),
        out_shape=jax.ShapeDtypeStruct((m, d), F32),
        scratch_shapes=[pltpu.VMEM((2, 2 * tm, d), F32), pltpu.SemaphoreType.DMA((2,))],
        compiler_params=_params("arbitrary"),
        name="combine_final_norm",
    )(d3, d3, ys, x1_all, route_all, g2, normf_g.reshape(1, d))


def _route_plan(route, n_experts, tm):
    t = route.shape[0]
    s = TOP_K * t
    e_flat = jnp.concatenate([route[:, ROUTE_ID0], route[:, ROUTE_ID1]]).astype(I32)
    onehot = (e_flat[:, None] == jnp.arange(n_experts, dtype=I32)[None, :]).astype(I32)
    csum = jnp.cumsum(onehot, axis=0)
    counts = csum[-1]
    padded = (counts + tm - 1) // tm * tm
    ends = jnp.cumsum(padded)
    dest = jnp.sum(onehot * (csum - 1 + (ends - padded)[None, :]), axis=1)
    n_tiles = pl.cdiv(s, tm) + n_experts + 2
    tile_start = jnp.arange(n_tiles, dtype=I32) * tm
    tile_valid = (tile_start < ends[-1]).astype(I32)
    te = jnp.minimum(jnp.sum((ends[None, :] <= tile_start[:, None]).astype(I32), axis=1), n_experts - 1)
    last_used = jnp.maximum(ends[-1] // tm - 1, 0)
    tile_expert = jnp.where(tile_valid == 1, te, te[last_used])
    tok_sorted = jnp.zeros((n_tiles * tm,), I32).at[dest].set(
        jnp.arange(s, dtype=I32) % t, unique_indices=True, mode="promise_in_bounds")
    return dest.reshape(TOP_K, t), tok_sorted, tile_expert, tile_valid


def kernel(x_prompt, x_sample, cache_k, cache_v, cache_logf, state_conv, c_prompt, c_sample, w_ada, b_ada, norm1_g, norm2_g, normf_g, w_in, conv_w, conv_b, b_forget, w_br_conv, w_br_attn, w_gate, b_gate, w_o, w_rg, b_rg, w_re, b_re, w1, w3, w2):
    depth = w_ada.shape[0]
    assert depth == 1, "single-layer step"
    l = 0
    bp, sp, d = x_prompt.shape
    bs, ss, _ = x_sample.shape
    _, _, past, n_heads, dh = cache_k.shape
    d_conv = state_conv.shape[-1]
    da = n_heads * dh
    n_groups = w_rg.shape[-1]
    n_experts = w_re.shape[-1]
    per_group = n_experts // n_groups
    assert n_groups + n_experts <= LANE
    scale = LOG2_E * dh ** -0.5

    w_in_bf = w_in[l].astype(BF16)
    q_off = 3 * d_conv
    w_q, w_k, w_v = (w_in_bf[:, q_off + j * da:q_off + (j + 1) * da] for j in range(3))
    hp = _round_up(n_heads, BF16_ROWS)
    wf_t = jnp.zeros((hp, d), BF16).at[:n_heads].set(w_in[l][:, 3 * d_conv + 3 * da:].T.astype(BF16))
    b_f = jnp.zeros((hp, 1), F32).at[:n_heads, 0].set(b_forget[l])
    w_brc_bf = w_br_conv[l].astype(BF16)
    w_bra_bf = w_br_attn[l].astype(BF16)
    w_gate_bf = w_gate[l].astype(BF16)
    b_gate2 = b_gate[l].reshape(1, 2 * d)
    w_o_bf = w_o[l].astype(BF16)
    w_r32 = jnp.pad(jnp.concatenate([w_rg[l], w_re[l]], axis=1), ((0, 0), (0, LANE - n_groups - n_experts)))
    w_r_hi = w_r32.astype(BF16)
    w_r = jnp.concatenate([w_r_hi, (w_r32 - w_r_hi.astype(F32)).astype(BF16)], axis=1)
    b_r = jnp.zeros((1, LANE), F32).at[0, :n_groups].set(b_rg[l]).at[0, n_groups:n_groups + n_experts].set(b_re[l])

    mod = _ada(jnp.concatenate([c_prompt, c_sample], axis=0), w_ada[l], b_ada[l])
    mods = {"p": jnp.split(mod[:bp], 6, axis=-1), "s": jnp.split(mod[bp:], 6, axis=-1)}

    def mixer_inputs(x, mod6, seq_len, prev):
        shift1, scale1 = mod6[0], mod6[1]
        h, lf_t = _norm(x, norm1_g[l], shift1, scale1, wf_t, b_f, seq_len)
        y_conv, tail = _conv_branch(h, w_in_bf, conv_w[l], conv_b[l], prev, seq_len, d_conv)
        q = _proj(h, w_q, BF16, scale)
        k = _proj(h, w_k, F32)
        v = _proj(h, w_v, F32)
        return h, y_conv, tail, q, k, v, lf_t[:n_heads]

    xp = x_prompt.reshape(bp * sp, d)
    xs = x_sample.reshape(bs * ss, d)
    hp_, ycp, tailp, qp, kp, vp, lftp = mixer_inputs(xp, mods["p"], sp, None)
    hs_, ycs, tails, qs, ks, vs, lfts = mixer_inputs(xs, mods["s"], ss, state_conv[l])

    lf_p = lftp.reshape(n_heads, bp, sp).transpose(1, 0, 2).reshape(bp * n_heads, sp)
    f_p = _cumsum_lanes(lf_p).reshape(bp * n_heads, 1, sp)
    yap = _attn_prompt(qp, kp, vp, f_p, bp, sp, n_heads, dh)

    lf_new = lfts.reshape(n_heads, bs, ss).transpose(1, 0, 2)
    lf_past = cache_logf[l].astype(F32).transpose(0, 2, 1)
    tot = _round_up(past + LANE, CUMSUM_CHUNK)
    lf_all = jnp.concatenate([lf_past, lf_new, jnp.zeros((bs, n_heads, tot - past - ss), F32)], axis=-1)
    f_s = _cumsum_lanes(lf_all.reshape(bs * n_heads, tot)).reshape(bs, n_heads, tot)
    yas = _attn_sample(qs, cache_k, cache_v, l, ks, vs, f_s, bs, ss, past, n_heads, dh)

    def wo_path(x, mod6, seq_len, h, yc, ya):
        gate1, shift2, scale2 = mod6[2], mod6[3], mod6[4]
        merged = _merge(yc, ya, h, w_brc_bf, w_bra_bf, w_gate_bf, b_gate2)
        return merged, x, gate1, shift2, scale2, seq_len

    x1_all, h2_all, route_all = _wo(
        wo_path(xp, mods["p"], sp, hp_, ycp, yap), wo_path(xs, mods["s"], ss, hs_, ycs, yas),
        w_o_bf, norm2_g[l], w_r, b_r, n_groups, per_group)
    tm_e = 256 if TOP_K * h2_all.shape[0] >= 8192 else 64
    dest, tok_sorted, tile_expert, tile_valid = _route_plan(route_all, n_experts, tm_e)
    ys = _experts(h2_all, tile_expert, tile_valid, tok_sorted, w1[l], w3[l], w2[l], tm_e)

    mp, ms = bp * sp, bs * ss
    y_p = _combine(ys, dest[:, :mp], x1_all, route_all, 0, mp, mods["p"][5], normf_g, sp)
    y_s = _combine(ys, dest[:, mp:], x1_all, route_all, mp, ms, mods["s"][5], normf_g, ss)

    def cache_outs(k, v, lf_t, tail, n_seq, seq_len):
        return (
            k.reshape(1, n_seq, seq_len, n_heads, dh),
            v.reshape(1, n_seq, seq_len, n_heads, dh),
            lf_t.T.reshape(1, n_seq, seq_len, n_heads),
            tail[:, CONV_TAIL_ROWS - 2:, :].reshape(1, n_seq, 2, d_conv),
        )

    return (y_p.reshape(bp, sp, d), y_s.reshape(bs, ss, d)) + cache_outs(kp, vp, lftp, tailp, bp, sp) + cache_outs(ks, vs, lfts, tails, bs, ss)
```

```python
import functools

import jax
import jax.numpy as jnp
from jax import lax
from jax.experimental import pallas as pl
from jax.experimental.pallas import tpu as pltpu

F32 = jnp.float32
BF16 = jnp.bfloat16
I32 = jnp.int32

EPS = 1e-6
NEG_INF = -1e30
LOG2_E = 1.4426950408889634
TOP_K = 2
LANE = 128
BF16_ROWS = 16
VMEM_LIMIT_BYTES = 56 * 1024 * 1024
NT_DIMS = (((1,), (1,)), ((), ()))


def _tile(n, pref, align):
    if n <= pref:
        return n
    t = (pref // align) * align
    while t >= align:
        if n % t == 0:
            return t
        t -= align
    return n


def _round_up(n, m):
    return (n + m - 1) // m * m


def _params(*sem):
    return pltpu.CompilerParams(dimension_semantics=sem, vmem_limit_bytes=VMEM_LIMIT_BYTES)


def _rowvec(vec, seq_len, tm, tile=lambda i: i):
    n_seq, d = vec.shape
    vec = vec.reshape(n_seq, 1, d)
    if tm <= seq_len:
        per = seq_len // tm
        return vec, pl.BlockSpec((1, 1, d), lambda i, *_: (tile(i) // per, 0, 0))
    return vec, pl.BlockSpec((tm // seq_len, 1, d), lambda i, *_: (tile(i), 0, 0))


def _bc(ref, rows):
    v = ref[...]
    k, _, d = v.shape
    if k == 1:
        return v.reshape(1, d)
    return jnp.broadcast_to(v, (k, rows // k, d)).reshape(rows, d)


def _rms(x, g):
    return x * lax.rsqrt(jnp.mean(x * x, axis=-1, keepdims=True) + EPS) * g


def _ada_kernel(c_ref, w_ref, b_ref, o_ref):
    o_ref[...] = jnp.dot(c_ref[...].astype(BF16), w_ref[...].astype(BF16), preferred_element_type=F32) + b_ref[...]


def _ada(c_all, w, b):
    r, d = c_all.shape
    n = w.shape[1]
    tn = _tile(n, 1024, LANE)
    return pl.pallas_call(
        _ada_kernel,
        grid=(n // tn,),
        in_specs=[
            pl.BlockSpec((r, d), lambda j: (0, 0)),
            pl.BlockSpec((d, tn), lambda j: (0, j)),
            pl.BlockSpec((1, tn), lambda j: (0, j)),
        ],
        out_specs=pl.BlockSpec((r, tn), lambda j: (0, j)),
        out_shape=jax.ShapeDtypeStruct((r, n), F32),
        compiler_params=_params("parallel"),
        name="ada",
    )(c_all, w, b.reshape(1, n))


def _norm_kernel(x_ref, g_ref, sh_ref, sc_ref, wf_ref, bf_ref, h_ref, lf_ref):
    tm = x_ref.shape[0]
    h = _rms(x_ref[...], g_ref[...]) * (1.0 + _bc(sc_ref, tm)) + _bc(sh_ref, tm)
    hb = h.astype(h_ref.dtype)
    h_ref[...] = hb
    fl = lax.dot_general(wf_ref[...], hb, NT_DIMS, preferred_element_type=F32)
    lf_ref[...] = jax.nn.log_sigmoid(fl + bf_ref[...])


def _norm(x, g, shift, scale, wf_t, b_f, seq_len):
    m, d = x.shape
    hp = wf_t.shape[0]
    tm = _tile(seq_len, 512, 8) if seq_len >= 512 else _tile(m, 512, seq_len)
    sh, sh_spec = _rowvec(shift, seq_len, tm)
    sc, sc_spec = _rowvec(scale, seq_len, tm)
    const = lambda shape: pl.BlockSpec(shape, lambda i: (0, 0))
    return pl.pallas_call(
        _norm_kernel,
        grid=(m // tm,),
        in_specs=[pl.BlockSpec((tm, d), lambda i: (i, 0)), const((1, d)), sh_spec, sc_spec, const((hp, d)), const((hp, 1))],
        out_specs=[pl.BlockSpec((tm, d), lambda i: (i, 0)), pl.BlockSpec((hp, tm), lambda i: (0, i))],
        out_shape=[jax.ShapeDtypeStruct((m, d), BF16), jax.ShapeDtypeStruct((hp, m), F32)],
        compiler_params=_params("parallel"),
        name="norm_forget",
    )(x, g.reshape(1, d), sh, sc, wf_t, b_f)


def _proj_tile(a_ref, w_ref, o_ref, scale):
    acc = jnp.dot(a_ref[...], w_ref[...], preferred_element_type=F32)
    if scale != 1.0:
        acc = acc * scale
    o_ref[...] = acc.astype(o_ref.dtype)


def _proj_kernel(a_ref, b_ref, w_ref, oa_ref, ob_ref, *, scale, n_tiles_a):
    i = pl.program_id(0)

    @pl.when(i < n_tiles_a)
    def _():
        _proj_tile(a_ref, w_ref, oa_ref, scale)

    @pl.when(i >= n_tiles_a)
    def _():
        _proj_tile(b_ref, w_ref, ob_ref, scale)


def _proj(a, b, w, out_dtype, scale=1.0):
    (ma, k), mb, n = a.shape, b.shape[0], w.shape[1]
    tm = _tile(min(ma, mb), 512, BF16_ROWS)
    assert ma % tm == 0 and mb % tm == 0
    na, nb = ma // tm, mb // tm
    spec_a = lambda cols: pl.BlockSpec((tm, cols), lambda i: (jnp.minimum(i, na - 1), 0))
    spec_b = lambda cols: pl.BlockSpec((tm, cols), lambda i: (jnp.maximum(i - na, 0), 0))
    return pl.pallas_call(
        functools.partial(_proj_kernel, scale=scale, n_tiles_a=na),
        grid=(na + nb,),
        in_specs=[spec_a(k), spec_b(k), pl.BlockSpec((k, n), lambda i: (0, 0))],
        out_specs=[spec_a(n), spec_b(n)],
        out_shape=[jax.ShapeDtypeStruct((ma, n), out_dtype), jax.ShapeDtypeStruct((mb, n), out_dtype)],
        compiler_params=_params("arbitrary"),
        name="proj",
    )(a, b, w)


CONV_TAIL_ROWS = 8


def _conv_kernel(*refs, seq_len, has_prev):
    if has_prev:
        h_ref, wb_ref, wc_ref, wv_ref, cw_ref, cb_ref, p1_ref, p2_ref, y_ref, tail_ref = refs
    else:
        h_ref, wb_ref, wc_ref, wv_ref, cw_ref, cb_ref, y_ref, tail_ref = refs
    h = h_ref[...]
    bg = jnp.dot(h, wb_ref[...], preferred_element_type=F32)
    cg = jnp.dot(h, wc_ref[...], preferred_element_type=F32)
    vc = jnp.dot(h, wv_ref[...], preferred_element_type=F32)
    u = cg * vc
    tm = u.shape[0]
    t = lax.broadcasted_iota(I32, (tm, 1), 0)
    if tm > seq_len:
        t = jnp.bitwise_and(t, seq_len - 1) if seq_len & (seq_len - 1) == 0 else lax.rem(t, seq_len)
    u1 = jnp.where(t >= 1, pltpu.roll(u, 1, 0), p1_ref[...] if has_prev else 0.0)
    u2 = jnp.where(t >= 2, pltpu.roll(u, 2, 0), p2_ref[...] if has_prev else 0.0)
    cw = cw_ref[...]
    conv = cb_ref[...] + ((cw[0:1] * u2 + cw[1:2] * u1) + cw[2:3] * u)
    y_ref[...] = (bg * conv).astype(y_ref.dtype)
    for s in range(tm // seq_len):
        tail_ref[s] = u[(s + 1) * seq_len - CONV_TAIL_ROWS:(s + 1) * seq_len, :]


def _conv_branch(h, w_in_bf, conv_w, conv_b, prev, seq_len, d_conv):
    m, d = h.shape
    assert conv_w.shape[0] == 3 and seq_len >= CONV_TAIL_ROWS
    tm = seq_len if seq_len >= 512 else m
    assert m % tm == 0 and tm % seq_len == 0
    tc = _tile(d_conv, 256, LANE)
    nc = d_conv // tc
    n_seq = m // seq_len
    seq_per_tile = tm // seq_len
    in_specs = [
        pl.BlockSpec((tm, d), lambda i, j: (i, 0)),
        pl.BlockSpec((d, tc), lambda i, j: (0, j)),
        pl.BlockSpec((d, tc), lambda i, j: (0, nc + j)),
        pl.BlockSpec((d, tc), lambda i, j: (0, 2 * nc + j)),
        pl.BlockSpec((3, tc), lambda i, j: (0, j)),
        pl.BlockSpec((1, tc), lambda i, j: (0, j)),
    ]
    args = [h, w_in_bf, w_in_bf, w_in_bf, conv_w, conv_b.reshape(1, d_conv)]
    if prev is not None:
        zeros = jnp.zeros((n_seq, seq_len, d_conv), F32)
        p1 = zeros.at[:, 0].set(prev[:, 1]).reshape(m, d_conv)
        p2 = zeros.at[:, 0].set(prev[:, 0]).at[:, 1].set(prev[:, 1]).reshape(m, d_conv)
        in_specs += [pl.BlockSpec((tm, tc), lambda i, j: (i, j))] * 2
        args += [p1, p2]
    return pl.pallas_call(
        functools.partial(_conv_kernel, seq_len=seq_len, has_prev=prev is not None),
        grid=(m // tm, nc),
        in_specs=in_specs,
        out_specs=[
            pl.BlockSpec((tm, tc), lambda i, j: (i, j)),
            pl.BlockSpec((seq_per_tile, CONV_TAIL_ROWS, tc), lambda i, j: (i, 0, j)),
        ],
        out_shape=[
            jax.ShapeDtypeStruct((m, d_conv), BF16),
            jax.ShapeDtypeStruct((n_seq, CONV_TAIL_ROWS, d_conv), F32),
        ],
        compiler_params=_params("parallel", "parallel"),
        name="conv_branch",
    )(*args)


CUMSUM_CHUNK = LANE


def _cumsum_kernel(x_ref, o_ref):
    r, n = x_ref.shape
    c = CUMSUM_CHUNK
    tri = (lax.broadcasted_iota(I32, (c, c), 0) <= lax.broadcasted_iota(I32, (c, c), 1)).astype(BF16)
    carry = jnp.zeros((r, 1), F32)
    for j in range(n // c):
        x = x_ref[:, j * c:(j + 1) * c]
        cs = carry
        for _ in range(3):
            piece = x.astype(BF16)
            cs = cs + jnp.dot(piece, tri, preferred_element_type=F32)
            x = x - piece.astype(F32)
        o_ref[:, j * c:(j + 1) * c] = cs
        carry = cs[:, c - 1:c]


def _cumsum_lanes(x):
    r, n = x.shape
    tr = _tile(r, 128, 8)
    return pl.pallas_call(
        _cumsum_kernel,
        grid=(r // tr,),
        in_specs=[pl.BlockSpec((tr, n), lambda i: (i, 0))],
        out_specs=pl.BlockSpec((tr, n), lambda i: (i, 0)),
        out_shape=jax.ShapeDtypeStruct((r, n), F32),
        compiler_params=_params("parallel"),
        name="cumsum",
    )(x)


def _attn_prompt_kernel(q_ref, k_ref, v_ref, f_ref, o_ref, *, tq, dh):
    seq_len = q_ref.shape[0]
    heads = q_ref.shape[1] // dh
    ones_col = (lax.broadcasted_iota(I32, (seq_len, dh), 1) == 0).astype(BF16)
    kb, vb, f = [], [], []
    for h in range(heads):
        cols = slice(h * dh, (h + 1) * dh)
        kb.append(k_ref[:, cols].astype(BF16))
        vb.append(jnp.concatenate([v_ref[:, cols].astype(BF16), ones_col], axis=1))
        f.append(f_ref[h] * LOG2_E)
    causal = lax.broadcasted_iota(I32, (tq, tq), 1) <= lax.broadcasted_iota(I32, (tq, tq), 0)
    for qi in range(seq_len // tq):
        lo = qi * tq
        for h in range(heads):
            cols = slice(h * dh, (h + 1) * dh)
            q = q_ref[lo:lo + tq, cols]
            sd = lax.dot_general(q, kb[h][lo:lo + tq], NT_DIMS, preferred_element_type=F32) - f[h][:, lo:lo + tq]
            sd = jnp.where(causal, sd, NEG_INF)
            m = jnp.max(sd, axis=-1, keepdims=True)
            if qi > 0:
                sp = lax.dot_general(q, kb[h][:lo], NT_DIMS, preferred_element_type=F32) - f[h][:, :lo]
                m = jnp.maximum(m, jnp.max(sp, axis=-1, keepdims=True))
                ol = jnp.dot(jnp.exp2(sp - m).astype(BF16), vb[h][:lo], preferred_element_type=F32)
            old = jnp.dot(jnp.exp2(sd - m).astype(BF16), vb[h][lo:lo + tq], preferred_element_type=F32)
            ol = ol + old if qi > 0 else old
            o_ref[lo:lo + tq, cols] = (ol[:, :dh] / ol[:, dh:dh + 1]).astype(o_ref.dtype)


ATTN_HEADS_PER_STEP = 4


def _attn_prompt(q, k, v, f, n_seq, seq_len, n_heads, dh):
    m = n_seq * seq_len
    tq = _tile(seq_len, 256, LANE)
    hps = ATTN_HEADS_PER_STEP if n_heads % ATTN_HEADS_PER_STEP == 0 else 1
    groups = n_heads // hps
    blk = pl.BlockSpec((seq_len, hps * dh), lambda b, g: (b, g))
    return pl.pallas_call(
        functools.partial(_attn_prompt_kernel, tq=tq, dh=dh),
        grid=(n_seq, groups),
        in_specs=[blk, blk, blk, pl.BlockSpec((hps, 1, seq_len), lambda b, g: (b * groups + g, 0, 0))],
        out_specs=blk,
        out_shape=jax.ShapeDtypeStruct((m, n_heads * dh), BF16),
        compiler_params=_params("parallel", "parallel"),
        name="attn_prompt",
    )(q, k, v, f)


def _attn_sample_kernel(q_ref, kc_ref, vc_ref, fp_ref, kn_ref, vn_ref, fn_ref, o_ref, m_sc, l_sc, acc_sc, *, n_heads, dh):
    kt = pl.program_id(1)
    lq = q_ref.shape[0]

    @pl.when(kt == 0)
    def _():
        m_sc[...] = jnp.full(m_sc.shape, NEG_INF, F32)
        l_sc[...] = jnp.zeros(l_sc.shape, F32)
        acc_sc[...] = jnp.zeros(acc_sc.shape, F32)

    q = q_ref[...]

    def update(k_heads, v_heads, fk, mask):
        s_heads = []
        for h in range(n_heads):
            s = lax.dot_general(q[:, h * dh:(h + 1) * dh], k_heads[h], NT_DIMS, preferred_element_type=F32)
            s = s - fk[h:h + 1, :]
            if mask is not None:
                s = jnp.where(mask, s, NEG_INF)
            s_heads.append(s)
        s = jnp.concatenate(s_heads, axis=0)
        m_prev = m_sc[...]
        m_new = jnp.maximum(m_prev, jnp.max(s, axis=-1, keepdims=True))
        alpha = jnp.exp2(m_prev - m_new)
        p = jnp.exp2(s - m_new)
        l_sc[...] = alpha * l_sc[...] + jnp.sum(p, axis=-1, keepdims=True)
        pb = p.astype(BF16)
        o_heads = [jnp.dot(pb[h * lq:(h + 1) * lq], v_heads[h], preferred_element_type=F32) for h in range(n_heads)]
        acc_sc[...] = alpha * acc_sc[...] + jnp.concatenate(o_heads, axis=0)
        m_sc[...] = m_new

    def cached_heads(ref):
        x = jnp.swapaxes(ref[0, 0].astype(BF16), 0, 1)
        return [x[h] for h in range(n_heads)]

    update(cached_heads(kc_ref), cached_heads(vc_ref), fp_ref[0] * LOG2_E, None)

    @pl.when(kt == pl.num_programs(1) - 1)
    def _():
        pad = jnp.zeros((LANE - lq, n_heads * dh), BF16)
        kn = jnp.concatenate([kn_ref[...].astype(BF16), pad], axis=0)
        vn = jnp.concatenate([vn_ref[...].astype(BF16), pad], axis=0)
        mask = lax.broadcasted_iota(I32, (lq, LANE), 1) <= lax.broadcasted_iota(I32, (lq, LANE), 0)
        heads = lambda x: [x[:, h * dh:(h + 1) * dh] for h in range(n_heads)]
        update(heads(kn), heads(vn), fn_ref[0] * LOG2_E, mask)
        out = acc_sc[...] / l_sc[...]
        o_ref[...] = jnp.concatenate([out[h * lq:(h + 1) * lq] for h in range(n_heads)], axis=1).astype(o_ref.dtype)


def _attn_sample(q, cache_k, cache_v, layer, k_new, v_new, f_all, n_seq, lq, past, n_heads, dh):
    da = n_heads * dh
    tk = _tile(past, 512, LANE)
    assert lq <= LANE and lq % BF16_ROWS == 0 and past % LANE == 0
    qblk = pl.BlockSpec((lq, da), lambda b, t: (b, 0))
    cblk = pl.BlockSpec((1, 1, tk, n_heads, dh), lambda b, t: (layer, b, t, 0, 0))
    return pl.pallas_call(
        functools.partial(_attn_sample_kernel, n_heads=n_heads, dh=dh),
        grid=(n_seq, past // tk),
        in_specs=[
            qblk,
            cblk,
            cblk,
            pl.BlockSpec((1, n_heads, tk), lambda b, t: (b, 0, t)),
            qblk,
            qblk,
            pl.BlockSpec((1, n_heads, LANE), lambda b, t: (b, 0, past // LANE)),
        ],
        out_specs=qblk,
        out_shape=jax.ShapeDtypeStruct((n_seq * lq, da), BF16),
        scratch_shapes=[
            pltpu.VMEM((n_heads * lq, 1), F32),
            pltpu.VMEM((n_heads * lq, 1), F32),
            pltpu.VMEM((n_heads * lq, dh), F32),
        ],
        compiler_params=_params("parallel", "arbitrary"),
        name="attn_sample",
    )(q, cache_k, cache_v, f_all, k_new, v_new, f_all)


def _merge_kernel(yc_ref, ya_ref, h_ref, wc_ref, wa_ref, wgc_ref, wga_ref, bgc_ref, bga_ref, o_ref):
    h = h_ref[...]
    a = jnp.dot(yc_ref[...], wc_ref[...], preferred_element_type=F32)
    b = jnp.dot(ya_ref[...], wa_ref[...], preferred_element_type=F32)
    gc = jax.nn.sigmoid(jnp.dot(h, wgc_ref[...], preferred_element_type=F32) + bgc_ref[...])
    ga = jax.nn.sigmoid(jnp.dot(h, wga_ref[...], preferred_element_type=F32) + bga_ref[...])
    o_ref[...] = (gc * a + ga * b).astype(o_ref.dtype)


def _merge(yc, ya, h, w_brc, w_bra, w_gate, b_gate):
    m, d = h.shape
    dc, da = yc.shape[1], ya.shape[1]
    tm = _tile(m, 1024, BF16_ROWS)
    tn = _tile(d, 512, LANE)
    nj = d // tn
    return pl.pallas_call(
        _merge_kernel,
        grid=(m // tm, nj),
        in_specs=[
            pl.BlockSpec((tm, dc), lambda i, j: (i, 0)),
            pl.BlockSpec((tm, da), lambda i, j: (i, 0)),
            pl.BlockSpec((tm, d), lambda i, j: (i, 0)),
            pl.BlockSpec((dc, tn), lambda i, j: (0, j)),
            pl.BlockSpec((da, tn), lambda i, j: (0, j)),
            pl.BlockSpec((d, tn), lambda i, j: (0, j)),
            pl.BlockSpec((d, tn), lambda i, j: (0, nj + j)),
            pl.BlockSpec((1, tn), lambda i, j: (0, j)),
            pl.BlockSpec((1, tn), lambda i, j: (0, nj + j)),
        ],
        out_specs=pl.BlockSpec((tm, tn), lambda i, j: (i, j)),
        out_shape=jax.ShapeDtypeStruct((m, d), BF16),
        compiler_params=_params("parallel", "parallel"),
        name="merge",
    )(yc, ya, h, w_brc, w_bra, w_gate, w_gate, b_gate, b_gate)


ROUTE_ID0, ROUTE_ID1, ROUTE_W0, ROUTE_W1 = 0, 1, 2, 3


def _wo_kernel(*refs, n_tiles_a, n_groups, per_group):
    a_refs, b_refs, (wo_ref, ng_ref, wr_ref, br_ref), outs = refs[0:5], refs[5:10], refs[10:14], refs[14:17]
    i = pl.program_id(0)

    @pl.when(i < n_tiles_a)
    def _():
        _wo_tile(*a_refs, wo_ref, ng_ref, wr_ref, br_ref, *outs, n_groups=n_groups, per_group=per_group)

    @pl.when(i >= n_tiles_a)
    def _():
        _wo_tile(*b_refs, wo_ref, ng_ref, wr_ref, br_ref, *outs, n_groups=n_groups, per_group=per_group)


WO_SUBTILES = 2


def _wo_tile(mg_ref, x_ref, g1_ref, sh_ref, sc_ref, wo_ref, ng_ref, wr_ref, br_ref, x1_ref, h2_ref, rt_ref, *, n_groups, per_group):
    tm = x_ref.shape[0]
    sub = tm // WO_SUBTILES if tm % (WO_SUBTILES * BF16_ROWS) == 0 else tm
    for r0 in range(0, tm, sub):
        rows = slice(r0, r0 + sub)
        vec = lambda ref: _bc(ref, tm) if ref.shape[0] == 1 else _bc(ref, tm)[rows, :]
        _wo_rows(mg_ref[rows, :], x_ref[rows, :], vec(g1_ref), vec(sh_ref), vec(sc_ref), wo_ref, ng_ref, wr_ref, br_ref,
                 x1_ref.at[rows, :], h2_ref.at[rows, :], rt_ref.at[rows, :], n_groups=n_groups, per_group=per_group)


def _wo_rows(mg, x, g1, sh, sc, wo_ref, ng_ref, wr_ref, br_ref, x1_ref, h2_ref, rt_ref, *, n_groups, per_group):
    o = jnp.dot(mg, wo_ref[...], preferred_element_type=F32)
    x1 = x + g1 * o
    x1_ref[...] = x1
    h2 = _rms(x1, ng_ref[...]) * (1.0 + sc) + sh
    h2_ref[...] = h2
    wr = wr_ref[...]
    hi = h2.astype(BF16)
    lo = (h2 - hi.astype(F32)).astype(BF16)
    r_hi = jnp.dot(hi, wr, preferred_element_type=F32)
    r_lo = jnp.dot(lo, wr[:, :LANE], preferred_element_type=F32)
    lg = r_hi[:, :LANE] + (r_hi[:, LANE:] + r_lo) + br_ref[...]
    col = lax.broadcasted_iota(I32, lg.shape, 1).astype(F32)
    big = float(LANE)
    gl = jnp.where(col < n_groups, lg, NEG_INF)
    gmax = jnp.max(gl, axis=-1, keepdims=True)
    g_val = 1.0 / jnp.sum(jnp.exp(gl - gmax), axis=-1, keepdims=True)
    g_idx = jnp.min(jnp.where(gl == gmax, col, big), axis=-1, keepdims=True)
    lo = n_groups + g_idx * per_group
    el = jnp.where((col >= lo) & (col < lo + per_group), lg, NEG_INF)
    m1 = jnp.max(el, axis=-1, keepdims=True)
    i1 = jnp.min(jnp.where(el == m1, col, big), axis=-1, keepdims=True)
    el2 = jnp.where(col == i1, NEG_INF, el)
    m2 = jnp.max(el2, axis=-1, keepdims=True)
    i2 = jnp.min(jnp.where(el2 == m2, col, big), axis=-1, keepdims=True)
    r = jnp.exp(m2 - m1)
    w0 = g_val / (1.0 + r)
    w1 = g_val * r / (1.0 + r)
    rt = jnp.where(col == ROUTE_ID0, i1 - n_groups, 0.0)
    rt = jnp.where(col == ROUTE_ID1, i2 - n_groups, rt)
    rt = jnp.where(col == ROUTE_W0, w0, rt)
    rt = jnp.where(col == ROUTE_W1, w1, rt)
    rt_ref[...] = rt


def _wo(path_a, path_b, w_o, norm_g, w_r, b_r, n_groups, per_group):
    d = w_o.shape[0]
    ma, mb = path_a[1].shape[0], path_b[1].shape[0]
    tm = _tile(min(ma, mb), 512, BF16_ROWS)
    assert ma % tm == 0 and mb % tm == 0
    na, nb = ma // tm, mb // tm
    tiles = (lambda i: jnp.minimum(i, na - 1), lambda i: jnp.maximum(i - na, 0))
    in_specs, args = [], []
    for (merged, x, gate1, shift2, scale2, seq_len), tile in zip((path_a, path_b), tiles):
        row = pl.BlockSpec((tm, d), lambda i, tile=tile: (tile(i), 0))
        in_specs += [row, row]
        args += [merged, x]
        for vec in (gate1, shift2, scale2):
            arr, spec = _rowvec(vec, seq_len, tm, tile)
            in_specs.append(spec)
            args.append(arr)
    const = lambda shape: pl.BlockSpec(shape, lambda i: (0, 0))
    in_specs += [pl.BlockSpec((d, d), lambda i: (0, 0), pipeline_mode=pl.Buffered(1)),
                 const((1, d)), const((d, 2 * LANE)), const((1, LANE))]
    args += [w_o, norm_g.reshape(1, d), w_r, b_r]
    row = pl.BlockSpec((tm, d), lambda i: (i, 0))
    return pl.pallas_call(
        functools.partial(_wo_kernel, n_tiles_a=na, n_groups=n_groups, per_group=per_group),
        grid=(na + nb,),
        in_specs=in_specs,
        out_specs=[row, row, pl.BlockSpec((tm, LANE), lambda i: (i, 0))],
        out_shape=[
            jax.ShapeDtypeStruct((ma + mb, d), F32),
            jax.ShapeDtypeStruct((ma + mb, d), F32),
            jax.ShapeDtypeStruct((ma + mb, LANE), F32),
        ],
        compiler_params=_params("arbitrary"),
        name="wo_norm_router",
    )(*args)


RING_SLOTS = 3
DMA_QUEUES = 2


def _gather_start(idx_ref, src_hbm, dst, slot, sem, n_rows, first=0):
    for r in range(first, first + n_rows):
        pltpu.make_async_copy(src_hbm.at[pl.ds(idx_ref[0, 0, r], 1)], dst.at[slot, pl.ds(r, 1)], sem.at[slot]).start(
            priority=r % DMA_QUEUES)


def _gather_wait(src_hbm, dst, slot, sem, n_rows):
    pltpu.make_async_copy(src_hbm.at[pl.ds(0, n_rows)], dst.at[slot], sem.at[slot]).wait()


def _expert_kernel(te_ref, tv_ref, tok0_ref, tok1_ref, tokn_ref, h2_hbm, w1_ref, w3_ref, w2_ref, o_ref, xbuf, w1b, w3b, w2b, sem):
    i = pl.program_id(0)
    tm = xbuf.shape[1]
    slot = lax.rem(i, RING_SLOTS)
    valid = tv_ref[i] == 1
    started = jnp.where(i < 2, tv_ref[0], tv_ref[jnp.maximum(i - 2, 0)]) == 1

    @pl.when(jnp.logical_and(i == 0, valid))
    def _():
        _gather_start(tok0_ref, h2_hbm, xbuf, 0, sem, tm)
        _gather_start(tok1_ref, h2_hbm, xbuf, 1, sem, tm)

    @pl.when(valid)
    def _():
        _gather_wait(h2_hbm, xbuf, slot, sem, tm)

        @pl.when(jnp.logical_or(i == 0, te_ref[i] != te_ref[jnp.maximum(i - 1, 0)]))
        def _():
            w1b[...] = w1_ref[0].astype(BF16)
            w3b[...] = w3_ref[0].astype(BF16)
            w2b[...] = w2_ref[0].astype(BF16)

        nslot = lax.rem(i + 2, RING_SLOTS)
        x = xbuf[slot].astype(BF16)
        a = jnp.dot(x, w1b[...], preferred_element_type=F32)
        _gather_start(tokn_ref, h2_hbm, xbuf, nslot, sem, tm // 2)
        b = jnp.dot(x, w3b[...], preferred_element_type=F32)
        _gather_start(tokn_ref, h2_hbm, xbuf, nslot, sem, tm // 2, first=tm // 2)
        hm = (jax.nn.silu(a) * b).astype(BF16)
        o_ref[...] = jnp.dot(hm, w2b[...], preferred_element_type=F32)

    @pl.when(jnp.logical_not(valid))
    def _():
        @pl.when(started)
        def _():
            _gather_wait(h2_hbm, xbuf, slot, sem, tm)

        o_ref[...] = jnp.zeros(o_ref.shape, F32)


def _experts(h2_all, tile_expert, tile_valid, tok_sorted, w1, w3, w2, tm):
    n_tiles = tile_expert.shape[0]
    _, d, de = w1.shape
    tok3 = tok_sorted.reshape(n_tiles, 1, tm)
    assert n_tiles >= RING_SLOTS
    idx_spec = lambda tile: pl.BlockSpec((1, 1, tm), lambda i, te, tv: (tile(i), 0, 0), memory_space=pltpu.SMEM)
    grid_spec = pltpu.PrefetchScalarGridSpec(
        num_scalar_prefetch=2,
        grid=(n_tiles,),
        in_specs=[
            idx_spec(lambda i: 0),
            idx_spec(lambda i: 1),
            idx_spec(lambda i: jnp.minimum(i + 2, n_tiles - 1)),
            pl.BlockSpec(memory_space=pl.ANY),
            pl.BlockSpec((1, d, de), lambda i, te, tv: (te[i], 0, 0)),
            pl.BlockSpec((1, d, de), lambda i, te, tv: (te[i], 0, 0)),
            pl.BlockSpec((1, de, d), lambda i, te, tv: (te[i], 0, 0)),
        ],
        out_specs=pl.BlockSpec((tm, d), lambda i, te, tv: (i, 0)),
        scratch_shapes=[
            pltpu.VMEM((RING_SLOTS, tm, d), F32),
            pltpu.VMEM((d, de), BF16),
            pltpu.VMEM((d, de), BF16),
            pltpu.VMEM((de, d), BF16),
            pltpu.SemaphoreType.DMA((RING_SLOTS,)),
        ],
    )
    return pl.pallas_call(
        _expert_kernel,
        grid_spec=grid_spec,
        out_shape=jax.ShapeDtypeStruct((n_tiles * tm, d), F32),
        compiler_params=_params("arbitrary"),
        name="experts",
    )(tile_expert, tile_valid, tok3, tok3, tok3, h2_all, w1, w3, w2)


COMBINE_CHUNKS = 4


def _combine_kernel(d_ref, dn_ref, ys_hbm, x1_ref, rt_ref, g2_ref, ng_ref, y_ref, ybuf, sem):
    i = pl.program_id(0)
    tm = x1_ref.shape[0]
    slot = lax.rem(i, 2)

    @pl.when(i == 0)
    def _():
        _gather_start(d_ref, ys_hbm, ybuf, 0, sem, 2 * tm)

    _gather_wait(ys_hbm, ybuf, slot, sem, 2 * tm)
    g2 = _bc(g2_ref, tm)
    ng = ng_ref[...]
    ck = tm // COMBINE_CHUNKS
    for c in range(COMBINE_CHUNKS):
        _gather_start(dn_ref, ys_hbm, ybuf, 1 - slot, sem, 2 * ck, first=2 * ck * c)
        rows = slice(c * ck, (c + 1) * ck)
        rt = rt_ref[rows, :]
        y0 = ybuf[slot, c * ck:(c + 1) * ck]
        y1 = ybuf[slot, tm + c * ck:tm + (c + 1) * ck]
        moe = rt[:, ROUTE_W0:ROUTE_W0 + 1] * y0 + rt[:, ROUTE_W1:ROUTE_W1 + 1] * y1
        g2c = g2 if g2.shape[0] == 1 else g2[rows]
        y_ref[rows, :] = _rms(x1_ref[rows, :] + g2c * moe, ng)

    @pl.when(i == pl.num_programs(0) - 1)
    def _():
        _gather_wait(ys_hbm, ybuf, 1 - slot, sem, 2 * tm)


def _combine(ys, dest, x1_all, route_all, row_off, m, gate2, normf_g, seq_len):
    d = x1_all.shape[1]
    tm = _tile(seq_len, 256, 8) if seq_len >= 256 else _tile(m, 256, seq_len)
    assert row_off % tm == 0 and tm % COMBINE_CHUNKS == 0
    nb, off = m // tm, row_off // tm
    d3 = dest.reshape(2, nb, tm).transpose(1, 0, 2).reshape(nb, 1, 2 * tm)
    d3 = jnp.concatenate([d3, jnp.zeros((1, 1, 2 * tm), I32)], axis=0)
    g2, g2_spec = _rowvec(gate2, seq_len, tm)
    return pl.pallas_call(
        _combine_kernel,
        grid=(nb,),
        in_specs=[
            pl.BlockSpec((1, 1, 2 * tm), lambda i: (i, 0, 0), memory_space=pltpu.SMEM),
            pl.BlockSpec((1, 1, 2 * tm), lambda i: (i + 1, 0, 0), memory_space=pltpu.SMEM),
            pl.BlockSpec(memory_space=pl.ANY),
            pl.BlockSpec((tm, d), lambda i: (off + i, 0)),
            pl.BlockSpec((tm, LANE), lambda i: (off + i, 0)),
            g2_spec,
            pl.BlockSpec((1, d), lambda i: (0, 0)),
        ],
        out_specs=pl.BlockSpec((tm, d), lambda i: (i, 0)),
        out_shape=jax.ShapeDtypeStruct((m, d), F32),
        scratch_shapes=[pltpu.VMEM((2, 2 * tm, d), F32), pltpu.SemaphoreType.DMA((2,))],
        compiler_params=_params("arbitrary"),
        name="combine_final_norm",
    )(d3, d3, ys, x1_all, route_all, g2, normf_g.reshape(1, d))


def _route_plan(route, n_experts, tm):
    t = route.shape[0]
    s = TOP_K * t
    e_flat = jnp.concatenate([route[:, ROUTE_ID0], route[:, ROUTE_ID1]]).astype(I32)
    onehot = (e_flat[:, None] == jnp.arange(n_experts, dtype=I32)[None, :]).astype(I32)
    csum = jnp.cumsum(onehot, axis=0)
    counts = csum[-1]
    padded = (counts + tm - 1) // tm * tm
    ends = jnp.cumsum(padded)
    dest = jnp.sum(onehot * (csum - 1 + (ends - padded)[None, :]), axis=1)
    n_tiles = pl.cdiv(s, tm) + n_experts + 2
    tile_start = jnp.arange(n_tiles, dtype=I32) * tm
    tile_valid = (tile_start < ends[-1]).astype(I32)
    te = jnp.minimum(jnp.sum((ends[None, :] <= tile_start[:, None]).astype(I32), axis=1), n_experts - 1)
    last_used = jnp.maximum(ends[-1] // tm - 1, 0)
    tile_expert = jnp.where(tile_valid == 1, te, te[last_used])
    tok_sorted = jnp.zeros((n_tiles * tm,), I32).at[dest].set(
        jnp.arange(s, dtype=I32) % t, unique_indices=True, mode="promise_in_bounds")
    return dest.reshape(TOP_K, t), tok_sorted, tile_expert, tile_valid


def kernel(x_prompt, x_sample, cache_k, cache_v, cache_logf, state_conv, c_prompt, c_sample, w_ada, b_ada, norm1_g, norm2_g, normf_g, w_in, conv_w, conv_b, b_forget, w_br_conv, w_br_attn, w_gate, b_gate, w_o, w_rg, b_rg, w_re, b_re, w1, w3, w2):
    depth = w_ada.shape[0]
    assert depth == 1, "single-layer step"
    l = 0
    bp, sp, d = x_prompt.shape
    bs, ss, _ = x_sample.shape
    _, _, past, n_heads, dh = cache_k.shape
    d_conv = state_conv.shape[-1]
    da = n_heads * dh
    n_groups = w_rg.shape[-1]
    n_experts = w_re.shape[-1]
    per_group = n_experts // n_groups
    assert n_groups + n_experts <= LANE
    scale = LOG2_E * dh ** -0.5

    w_in_bf = w_in[l].astype(BF16)
    q_off = 3 * d_conv
    w_q, w_k, w_v = (w_in_bf[:, q_off + j * da:q_off + (j + 1) * da] for j in range(3))
    hp = _round_up(n_heads, BF16_ROWS)
    wf_t = jnp.zeros((hp, d), BF16).at[:n_heads].set(w_in[l][:, 3 * d_conv + 3 * da:].T.astype(BF16))
    b_f = jnp.zeros((hp, 1), F32).at[:n_heads, 0].set(b_forget[l])
    w_brc_bf = w_br_conv[l].astype(BF16)
    w_bra_bf = w_br_attn[l].astype(BF16)
    w_gate_bf = w_gate[l].astype(BF16)
    b_gate2 = b_gate[l].reshape(1, 2 * d)
    w_o_bf = w_o[l].astype(BF16)
    w_r32 = jnp.pad(jnp.concatenate([w_rg[l], w_re[l]], axis=1), ((0, 0), (0, LANE - n_groups - n_experts)))
    w_r_hi = w_r32.astype(BF16)
    w_r = jnp.concatenate([w_r_hi, (w_r32 - w_r_hi.astype(F32)).astype(BF16)], axis=1)
    b_r = jnp.zeros((1, LANE), F32).at[0, :n_groups].set(b_rg[l]).at[0, n_groups:n_groups + n_experts].set(b_re[l])

    mod = _ada(jnp.concatenate([c_prompt, c_sample], axis=0), w_ada[l], b_ada[l])
    mods = {"p": jnp.split(mod[:bp], 6, axis=-1), "s": jnp.split(mod[bp:], 6, axis=-1)}

    def mixer_inputs(x, mod6, seq_len, prev):
        shift1, scale1 = mod6[0], mod6[1]
        h, lf_t = _norm(x, norm1_g[l], shift1, scale1, wf_t, b_f, seq_len)
        y_conv, tail = _conv_branch(h, w_in_bf, conv_w[l], conv_b[l], prev, seq_len, d_conv)
        return h, y_conv, tail, lf_t[:n_heads]

    xp = x_prompt.reshape(bp * sp, d)
    xs = x_sample.reshape(bs * ss, d)
    hp_, ycp, tailp, lftp = mixer_inputs(xp, mods["p"], sp, None)
    hs_, ycs, tails, lfts = mixer_inputs(xs, mods["s"], ss, state_conv[l])
    qp, qs = _proj(hp_, hs_, w_q, BF16, scale)
    kp, ks = _proj(hp_, hs_, w_k, F32)
    vp, vs = _proj(hp_, hs_, w_v, F32)

    lf_p = lftp.reshape(n_heads, bp, sp).transpose(1, 0, 2).reshape(bp * n_heads, sp)
    f_p = _cumsum_lanes(lf_p).reshape(bp * n_heads, 1, sp)
    yap = _attn_prompt(qp, kp, vp, f_p, bp, sp, n_heads, dh)

    lf_new = lfts.reshape(n_heads, bs, ss).transpose(1, 0, 2)
    lf_past = cache_logf[l].astype(F32).transpose(0, 2, 1)
    tot = _round_up(past + LANE, CUMSUM_CHUNK)
    lf_all = jnp.concatenate([lf_past, lf_new, jnp.zeros((bs, n_heads, tot - past - ss), F32)], axis=-1)
    f_s = _cumsum_lanes(lf_all.reshape(bs * n_heads, tot)).reshape(bs, n_heads, tot)
    yas = _attn_sample(qs, cache_k, cache_v, l, ks, vs, f_s, bs, ss, past, n_heads, dh)

    def wo_path(x, mod6, seq_len, h, yc, ya):
        gate1, shift2, scale2 = mod6[2], mod6[3], mod6[4]
        merged = _merge(yc, ya, h, w_brc_bf, w_bra_bf, w_gate_bf, b_gate2)
        return merged, x, gate1, shift2, scale2, seq_len

    x1_all, h2_all, route_all = _wo(
        wo_path(xp, mods["p"], sp, hp_, ycp, yap), wo_path(xs, mods["s"], ss, hs_, ycs, yas),
        w_o_bf, norm2_g[l], w_r, b_r, n_groups, per_group)
    tm_e = 256 if TOP_K * h2_all.shape[0] >= 8192 else 64
    dest, tok_sorted, tile_expert, tile_valid = _route_plan(route_all, n_experts, tm_e)
    ys = _experts(h2_all, tile_expert, tile_valid, tok_sorted, w1[l], w3[l], w2[l], tm_e)

    mp, ms = bp * sp, bs * ss
    y_p = _combine(ys, dest[:, :mp], x1_all, route_all, 0, mp, mods["p"][5], normf_g, sp)
    y_s = _combine(ys, dest[:, mp:], x1_all, route_all, mp, ms, mods["s"][5], normf_g, ss)

    def cache_outs(k, v, lf_t, tail, n_seq, seq_len):
        return (
            k.reshape(1, n_seq, seq_len, n_heads, dh),
            v.reshape(1, n_seq, seq_len, n_heads, dh),
            lf_t.T.reshape(1, n_seq, seq_len, n_heads),
            tail[:, CONV_TAIL_ROWS - 2:, :].reshape(1, n_seq, 2, d_conv),
        )

    return (y_p.reshape(bp, sp, d), y_s.reshape(bs, ss, d)) + cache_outs(kp, vp, lftp, tailp, bp, sp) + cache_outs(ks, vs, lfts, tails, bs, ss)
```

```python
import functools

import jax
import jax.numpy as jnp
from jax import lax
from jax.experimental import pallas as pl
from jax.experimental.pallas import tpu as pltpu

F32 = jnp.float32
BF16 = jnp.bfloat16
I32 = jnp.int32

EPS = 1e-6
NEG_INF = -1e30
LOG2_E = 1.4426950408889634
TOP_K = 2
LANE = 128
BF16_ROWS = 16
VMEM_LIMIT_BYTES = 56 * 1024 * 1024
NT_DIMS = (((1,), (1,)), ((), ()))


def _tile(n, pref, align):
    if n <= pref:
        return n
    t = (pref // align) * align
    while t >= align:
        if n % t == 0:
            return t
        t -= align
    return n


def _round_up(n, m):
    return (n + m - 1) // m * m


def _params(*sem):
    return pltpu.CompilerParams(dimension_semantics=sem, vmem_limit_bytes=VMEM_LIMIT_BYTES)


def _rowvec(vec, seq_len, tm, tile=lambda i: i):
    n_seq, d = vec.shape
    vec = vec.reshape(n_seq, 1, d)
    if tm <= seq_len:
        per = seq_len // tm
        return vec, pl.BlockSpec((1, 1, d), lambda i, *_: (tile(i) // per, 0, 0))
    return vec, pl.BlockSpec((tm // seq_len, 1, d), lambda i, *_: (tile(i), 0, 0))


def _bc(ref, rows):
    v = ref[...]
    k, _, d = v.shape
    if k == 1:
        return v.reshape(1, d)
    return jnp.broadcast_to(v, (k, rows // k, d)).reshape(rows, d)


def _rms(x, g):
    return x * lax.rsqrt(jnp.mean(x * x, axis=-1, keepdims=True) + EPS) * g


def _ada_kernel(c_ref, w_ref, b_ref, o_ref):
    o_ref[...] = jnp.dot(c_ref[...].astype(BF16), w_ref[...].astype(BF16), preferred_element_type=F32) + b_ref[...]


def _ada(c_all, w, b):
    r, d = c_all.shape
    n = w.shape[1]
    tn = _tile(n, 1024, LANE)
    return pl.pallas_call(
        _ada_kernel,
        grid=(n // tn,),
        in_specs=[
            pl.BlockSpec((r, d), lambda j: (0, 0)),
            pl.BlockSpec((d, tn), lambda j: (0, j)),
            pl.BlockSpec((1, tn), lambda j: (0, j)),
        ],
        out_specs=pl.BlockSpec((r, tn), lambda j: (0, j)),
        out_shape=jax.ShapeDtypeStruct((r, n), F32),
        compiler_params=_params("parallel"),
        name="ada",
    )(c_all, w, b.reshape(1, n))


def _norm_kernel(x_ref, g_ref, sh_ref, sc_ref, wf_ref, bf_ref, h_ref, lf_ref):
    tm = x_ref.shape[0]
    h = _rms(x_ref[...], g_ref[...]) * (1.0 + _bc(sc_ref, tm)) + _bc(sh_ref, tm)
    hb = h.astype(h_ref.dtype)
    h_ref[...] = hb
    fl = lax.dot_general(wf_ref[...], hb, NT_DIMS, preferred_element_type=F32)
    lf_ref[...] = jax.nn.log_sigmoid(fl + bf_ref[...])


def _norm(x, g, shift, scale, wf_t, b_f, seq_len):
    m, d = x.shape
    hp = wf_t.shape[0]
    tm = _tile(seq_len, 512, 8) if seq_len >= 512 else _tile(m, 512, seq_len)
    sh, sh_spec = _rowvec(shift, seq_len, tm)
    sc, sc_spec = _rowvec(scale, seq_len, tm)
    const = lambda shape: pl.BlockSpec(shape, lambda i: (0, 0))
    return pl.pallas_call(
        _norm_kernel,
        grid=(m // tm,),
        in_specs=[pl.BlockSpec((tm, d), lambda i: (i, 0)), const((1, d)), sh_spec, sc_spec, const((hp, d)), const((hp, 1))],
        out_specs=[pl.BlockSpec((tm, d), lambda i: (i, 0)), pl.BlockSpec((hp, tm), lambda i: (0, i))],
        out_shape=[jax.ShapeDtypeStruct((m, d), BF16), jax.ShapeDtypeStruct((hp, m), F32)],
        compiler_params=_params("parallel"),
        name="norm_forget",
    )(x, g.reshape(1, d), sh, sc, wf_t, b_f)


def _proj_tile(a_ref, w_ref, o_ref, scale):
    acc = jnp.dot(a_ref[...], w_ref[...], preferred_element_type=F32)
    if scale != 1.0:
        acc = acc * scale
    o_ref[...] = acc.astype(o_ref.dtype)


def _proj_kernel(a_ref, b_ref, w_ref, oa_ref, ob_ref, *, scale, n_tiles_a):
    i = pl.program_id(0)

    @pl.when(i < n_tiles_a)
    def _():
        _proj_tile(a_ref, w_ref, oa_ref, scale)

    @pl.when(i >= n_tiles_a)
    def _():
        _proj_tile(b_ref, w_ref, ob_ref, scale)


def _proj(a, b, w, out_dtype, scale=1.0):
    (ma, k), mb, n = a.shape, b.shape[0], w.shape[1]
    tm = _tile(min(ma, mb), 512, BF16_ROWS)
    assert ma % tm == 0 and mb % tm == 0
    na, nb = ma // tm, mb // tm
    spec_a = lambda cols: pl.BlockSpec((tm, cols), lambda i: (jnp.minimum(i, na - 1), 0))
    spec_b = lambda cols: pl.BlockSpec((tm, cols), lambda i: (jnp.maximum(i - na, 0), 0))
    return pl.pallas_call(
        functools.partial(_proj_kernel, scale=scale, n_tiles_a=na),
        grid=(na + nb,),
        in_specs=[spec_a(k), spec_b(k), pl.BlockSpec((k, n), lambda i: (0, 0))],
        out_specs=[spec_a(n), spec_b(n)],
        out_shape=[jax.ShapeDtypeStruct((ma, n), out_dtype), jax.ShapeDtypeStruct((mb, n), out_dtype)],
        compiler_params=_params("arbitrary"),
        name="proj",
    )(a, b, w)


CONV_TAIL_ROWS = 8


def _conv_kernel(*refs, seq_len, has_prev):
    if has_prev:
        h_ref, wb_ref, wc_ref, wv_ref, cw_ref, cb_ref, p1_ref, p2_ref, y_ref, tail_ref = refs
    else:
        h_ref, wb_ref, wc_ref, wv_ref, cw_ref, cb_ref, y_ref, tail_ref = refs
    h = h_ref[...]
    bg = jnp.dot(h, wb_ref[...], preferred_element_type=F32)
    cg = jnp.dot(h, wc_ref[...], preferred_element_type=F32)
    vc = jnp.dot(h, wv_ref[...], preferred_element_type=F32)
    u = cg * vc
    tm = u.shape[0]
    t = lax.broadcasted_iota(I32, (tm, 1), 0)
    if tm > seq_len:
        t = jnp.bitwise_and(t, seq_len - 1) if seq_len & (seq_len - 1) == 0 else lax.rem(t, seq_len)
    u1 = jnp.where(t >= 1, pltpu.roll(u, 1, 0), p1_ref[...] if has_prev else 0.0)
    u2 = jnp.where(t >= 2, pltpu.roll(u, 2, 0), p2_ref[...] if has_prev else 0.0)
    cw = cw_ref[...]
    conv = cb_ref[...] + ((cw[0:1] * u2 + cw[1:2] * u1) + cw[2:3] * u)
    y_ref[...] = (bg * conv).astype(y_ref.dtype)
    for s in range(tm // seq_len):
        tail_ref[s] = u[(s + 1) * seq_len - CONV_TAIL_ROWS:(s + 1) * seq_len, :]


def _conv_branch(h, w_in_bf, conv_w, conv_b, prev, seq_len, d_conv):
    m, d = h.shape
    assert conv_w.shape[0] == 3 and seq_len >= CONV_TAIL_ROWS
    tm = seq_len if seq_len >= 512 else m
    assert m % tm == 0 and tm % seq_len == 0
    tc = _tile(d_conv, 256, LANE)
    nc = d_conv // tc
    n_seq = m // seq_len
    seq_per_tile = tm // seq_len
    in_specs = [
        pl.BlockSpec((tm, d), lambda i, j: (i, 0)),
        pl.BlockSpec((d, tc), lambda i, j: (0, j)),
        pl.BlockSpec((d, tc), lambda i, j: (0, nc + j)),
        pl.BlockSpec((d, tc), lambda i, j: (0, 2 * nc + j)),
        pl.BlockSpec((3, tc), lambda i, j: (0, j)),
        pl.BlockSpec((1, tc), lambda i, j: (0, j)),
    ]
    args = [h, w_in_bf, w_in_bf, w_in_bf, conv_w, conv_b.reshape(1, d_conv)]
    if prev is not None:
        zeros = jnp.zeros((n_seq, seq_len, d_conv), F32)
        p1 = zeros.at[:, 0].set(prev[:, 1]).reshape(m, d_conv)
        p2 = zeros.at[:, 0].set(prev[:, 0]).at[:, 1].set(prev[:, 1]).reshape(m, d_conv)
        in_specs += [pl.BlockSpec((tm, tc), lambda i, j: (i, j))] * 2
        args += [p1, p2]
    return pl.pallas_call(
        functools.partial(_conv_kernel, seq_len=seq_len, has_prev=prev is not None),
        grid=(m // tm, nc),
        in_specs=in_specs,
        out_specs=[
            pl.BlockSpec((tm, tc), lambda i, j: (i, j)),
            pl.BlockSpec((seq_per_tile, CONV_TAIL_ROWS, tc), lambda i, j: (i, 0, j)),
        ],
        out_shape=[
            jax.ShapeDtypeStruct((m, d_conv), BF16),
            jax.ShapeDtypeStruct((n_seq, CONV_TAIL_ROWS, d_conv), F32),
        ],
        compiler_params=_params("parallel", "parallel"),
        name="conv_branch",
    )(*args)


CUMSUM_CHUNK = LANE


def _cumsum_kernel(x_ref, o_ref):
    r, n = x_ref.shape
    c = CUMSUM_CHUNK
    tri = (lax.broadcasted_iota(I32, (c, c), 0) <= lax.broadcasted_iota(I32, (c, c), 1)).astype(BF16)
    carry = jnp.zeros((r, 1), F32)
    for j in range(n // c):
        x = x_ref[:, j * c:(j + 1) * c]
        cs = carry
        for _ in range(3):
            piece = x.astype(BF16)
            cs = cs + jnp.dot(piece, tri, preferred_element_type=F32)
            x = x - piece.astype(F32)
        o_ref[:, j * c:(j + 1) * c] = cs
        carry = cs[:, c - 1:c]


def _cumsum_lanes(x):
    r, n = x.shape
    tr = _tile(r, 128, 8)
    return pl.pallas_call(
        _cumsum_kernel,
        grid=(r // tr,),
        in_specs=[pl.BlockSpec((tr, n), lambda i: (i, 0))],
        out_specs=pl.BlockSpec((tr, n), lambda i: (i, 0)),
        out_shape=jax.ShapeDtypeStruct((r, n), F32),
        compiler_params=_params("parallel"),
        name="cumsum",
    )(x)


def _attn_prompt_kernel(q_ref, k_ref, v_ref, f_ref, o_ref, *, tq, dh):
    seq_len = q_ref.shape[0]
    heads = q_ref.shape[1] // dh
    ones_col = (lax.broadcasted_iota(I32, (seq_len, dh), 1) == 0).astype(BF16)
    kb, vb, f = [], [], []
    for h in range(heads):
        cols = slice(h * dh, (h + 1) * dh)
        kb.append(k_ref[:, cols].astype(BF16))
        vb.append(jnp.concatenate([v_ref[:, cols].astype(BF16), ones_col], axis=1))
        f.append(f_ref[h] * LOG2_E)
    causal = lax.broadcasted_iota(I32, (tq, tq), 1) <= lax.broadcasted_iota(I32, (tq, tq), 0)
    for qi in range(seq_len // tq):
        lo = qi * tq
        for h in range(heads):
            cols = slice(h * dh, (h + 1) * dh)
            q = q_ref[lo:lo + tq, cols]
            sd = lax.dot_general(q, kb[h][lo:lo + tq], NT_DIMS, preferred_element_type=F32) - f[h][:, lo:lo + tq]
            sd = jnp.where(causal, sd, NEG_INF)
            m = jnp.max(sd, axis=-1, keepdims=True)
            if qi > 0:
                sp = lax.dot_general(q, kb[h][:lo], NT_DIMS, preferred_element_type=F32) - f[h][:, :lo]
                m = jnp.maximum(m, jnp.max(sp, axis=-1, keepdims=True))
                ol = jnp.dot(jnp.exp2(sp - m).astype(BF16), vb[h][:lo], preferred_element_type=F32)
            old = jnp.dot(jnp.exp2(sd - m).astype(BF16), vb[h][lo:lo + tq], preferred_element_type=F32)
            ol = ol + old if qi > 0 else old
            o_ref[lo:lo + tq, cols] = (ol[:, :dh] / ol[:, dh:dh + 1]).astype(o_ref.dtype)


ATTN_HEADS_PER_STEP = 4


def _attn_prompt(q, k, v, f, n_seq, seq_len, n_heads, dh):
    m = n_seq * seq_len
    tq = _tile(seq_len, 256, LANE)
    hps = ATTN_HEADS_PER_STEP if n_heads % ATTN_HEADS_PER_STEP == 0 else 1
    groups = n_heads // hps
    blk = pl.BlockSpec((seq_len, hps * dh), lambda b, g: (b, g))
    return pl.pallas_call(
        functools.partial(_attn_prompt_kernel, tq=tq, dh=dh),
        grid=(n_seq, groups),
        in_specs=[blk, blk, blk, pl.BlockSpec((hps, 1, seq_len), lambda b, g: (b * groups + g, 0, 0))],
        out_specs=blk,
        out_shape=jax.ShapeDtypeStruct((m, n_heads * dh), BF16),
        compiler_params=_params("parallel", "parallel"),
        name="attn_prompt",
    )(q, k, v, f)


def _attn_sample_kernel(q_ref, kc_ref, vc_ref, fp_ref, kn_ref, vn_ref, fn_ref, o_ref, m_sc, l_sc, acc_sc, *, n_heads, dh):
    kt = pl.program_id(1)
    lq = q_ref.shape[0]

    @pl.when(kt == 0)
    def _():
        m_sc[...] = jnp.full(m_sc.shape, NEG_INF, F32)
        l_sc[...] = jnp.zeros(l_sc.shape, F32)
        acc_sc[...] = jnp.zeros(acc_sc.shape, F32)

    q = q_ref[...]

    def update(k_heads, v_heads, fk, mask):
        s_heads = []
        for h in range(n_heads):
            s = lax.dot_general(q[:, h * dh:(h + 1) * dh], k_heads[h], NT_DIMS, preferred_element_type=F32)
            s = s - fk[h:h + 1, :]
            if mask is not None:
                s = jnp.where(mask, s, NEG_INF)
            s_heads.append(s)
        s = jnp.concatenate(s_heads, axis=0)
        m_prev = m_sc[...]
        m_new = jnp.maximum(m_prev, jnp.max(s, axis=-1, keepdims=True))
        alpha = jnp.exp2(m_prev - m_new)
        p = jnp.exp2(s - m_new)
        l_sc[...] = alpha * l_sc[...] + jnp.sum(p, axis=-1, keepdims=True)
        pb = p.astype(BF16)
        o_heads = [jnp.dot(pb[h * lq:(h + 1) * lq], v_heads[h], preferred_element_type=F32) for h in range(n_heads)]
        acc_sc[...] = alpha * acc_sc[...] + jnp.concatenate(o_heads, axis=0)
        m_sc[...] = m_new

    def cached_heads(ref):
        x = jnp.swapaxes(ref[0, 0].astype(BF16), 0, 1)
        return [x[h] for h in range(n_heads)]

    update(cached_heads(kc_ref), cached_heads(vc_ref), fp_ref[0] * LOG2_E, None)

    @pl.when(kt == pl.num_programs(1) - 1)
    def _():
        pad = jnp.zeros((LANE - lq, n_heads * dh), BF16)
        kn = jnp.concatenate([kn_ref[...].astype(BF16), pad], axis=0)
        vn = jnp.concatenate([vn_ref[...].astype(BF16), pad], axis=0)
        mask = lax.broadcasted_iota(I32, (lq, LANE), 1) <= lax.broadcasted_iota(I32, (lq, LANE), 0)
        heads = lambda x: [x[:, h * dh:(h + 1) * dh] for h in range(n_heads)]
        update(heads(kn), heads(vn), fn_ref[0] * LOG2_E, mask)
        out = acc_sc[...] / l_sc[...]
        o_ref[...] = jnp.concatenate([out[h * lq:(h + 1) * lq] for h in range(n_heads)], axis=1).astype(o_ref.dtype)


def _attn_sample(q, cache_k, cache_v, layer, k_new, v_new, f_all, n_seq, lq, past, n_heads, dh):
    da = n_heads * dh
    tk = _tile(past, 1024, LANE)
    assert lq <= LANE and lq % BF16_ROWS == 0 and past % LANE == 0
    qblk = pl.BlockSpec((lq, da), lambda b, t: (b, 0))
    cblk = pl.BlockSpec((1, 1, tk, n_heads, dh), lambda b, t: (layer, b, t, 0, 0))
    return pl.pallas_call(
        functools.partial(_attn_sample_kernel, n_heads=n_heads, dh=dh),
        grid=(n_seq, past // tk),
        in_specs=[
            qblk,
            cblk,
            cblk,
            pl.BlockSpec((1, n_heads, tk), lambda b, t: (b, 0, t)),
            qblk,
            qblk,
            pl.BlockSpec((1, n_heads, LANE), lambda b, t: (b, 0, past // LANE)),
        ],
        out_specs=qblk,
        out_shape=jax.ShapeDtypeStruct((n_seq * lq, da), BF16),
        scratch_shapes=[
            pltpu.VMEM((n_heads * lq, 1), F32),
            pltpu.VMEM((n_heads * lq, 1), F32),
            pltpu.VMEM((n_heads * lq, dh), F32),
        ],
        compiler_params=_params("parallel", "arbitrary"),
        name="attn_sample",
    )(q, cache_k, cache_v, f_all, k_new, v_new, f_all)


def _merge_kernel(yc_ref, ya_ref, h_ref, wc_ref, wa_ref, wgc_ref, wga_ref, bgc_ref, bga_ref, o_ref):
    h = h_ref[...]
    a = jnp.dot(yc_ref[...], wc_ref[...], preferred_element_type=F32)
    b = jnp.dot(ya_ref[...], wa_ref[...], preferred_element_type=F32)
    gc = jax.nn.sigmoid(jnp.dot(h, wgc_ref[...], preferred_element_type=F32) + bgc_ref[...])
    ga = jax.nn.sigmoid(jnp.dot(h, wga_ref[...], preferred_element_type=F32) + bga_ref[...])
    o_ref[...] = (gc * a + ga * b).astype(o_ref.dtype)


def _merge(yc, ya, h, w_brc, w_bra, w_gate, b_gate):
    m, d = h.shape
    dc, da = yc.shape[1], ya.shape[1]
    tm = _tile(m, 1024, BF16_ROWS)
    tn = _tile(d, 512, LANE)
    nj = d // tn
    return pl.pallas_call(
        _merge_kernel,
        grid=(m // tm, nj),
        in_specs=[
            pl.BlockSpec((tm, dc), lambda i, j: (i, 0)),
            pl.BlockSpec((tm, da), lambda i, j: (i, 0)),
            pl.BlockSpec((tm, d), lambda i, j: (i, 0)),
            pl.BlockSpec((dc, tn), lambda i, j: (0, j)),
            pl.BlockSpec((da, tn), lambda i, j: (0, j)),
            pl.BlockSpec((d, tn), lambda i, j: (0, j)),
            pl.BlockSpec((d, tn), lambda i, j: (0, nj + j)),
            pl.BlockSpec((1, tn), lambda i, j: (0, j)),
            pl.BlockSpec((1, tn), lambda i, j: (0, nj + j)),
        ],
        out_specs=pl.BlockSpec((tm, tn), lambda i, j: (i, j)),
        out_shape=jax.ShapeDtypeStruct((m, d), BF16),
        compiler_params=_params("parallel", "parallel"),
        name="merge",
    )(yc, ya, h, w_brc, w_bra, w_gate, w_gate, b_gate, b_gate)


ROUTE_ID0, ROUTE_ID1, ROUTE_W0, ROUTE_W1 = 0, 1, 2, 3


def _wo_kernel(*refs, n_tiles_a, n_groups, per_group):
    a_refs, b_refs, (wo_ref, ng_ref, wr_ref, br_ref), outs = refs[0:5], refs[5:10], refs[10:14], refs[14:17]
    i = pl.program_id(0)

    @pl.when(i < n_tiles_a)
    def _():
        _wo_tile(*a_refs, wo_ref, ng_ref, wr_ref, br_ref, *outs, n_groups=n_groups, per_group=per_group)

    @pl.when(i >= n_tiles_a)
    def _():
        _wo_tile(*b_refs, wo_ref, ng_ref, wr_ref, br_ref, *outs, n_groups=n_groups, per_group=per_group)


WO_SUBTILES = 2


def _wo_tile(mg_ref, x_ref, g1_ref, sh_ref, sc_ref, wo_ref, ng_ref, wr_ref, br_ref, x1_ref, h2_ref, rt_ref, *, n_groups, per_group):
    tm = x_ref.shape[0]
    sub = tm // WO_SUBTILES if tm % (WO_SUBTILES * BF16_ROWS) == 0 else tm
    for r0 in range(0, tm, sub):
        rows = slice(r0, r0 + sub)
        vec = lambda ref: _bc(ref, tm) if ref.shape[0] == 1 else _bc(ref, tm)[rows, :]
        _wo_rows(mg_ref[rows, :], x_ref[rows, :], vec(g1_ref), vec(sh_ref), vec(sc_ref), wo_ref, ng_ref, wr_ref, br_ref,
                 x1_ref.at[rows, :], h2_ref.at[rows, :], rt_ref.at[rows, :], n_groups=n_groups, per_group=per_group)


def _wo_rows(mg, x, g1, sh, sc, wo_ref, ng_ref, wr_ref, br_ref, x1_ref, h2_ref, rt_ref, *, n_groups, per_group):
    o = jnp.dot(mg, wo_ref[...], preferred_element_type=F32)
    x1 = x + g1 * o
    x1_ref[...] = x1
    h2 = _rms(x1, ng_ref[...]) * (1.0 + sc) + sh
    h2_ref[...] = h2
    wr = wr_ref[...]
    hi = h2.astype(BF16)
    lo = (h2 - hi.astype(F32)).astype(BF16)
    r_hi = jnp.dot(hi, wr, preferred_element_type=F32)
    r_lo = jnp.dot(lo, wr[:, :LANE], preferred_element_type=F32)
    lg = r_hi[:, :LANE] + (r_hi[:, LANE:] + r_lo) + br_ref[...]
    col = lax.broadcasted_iota(I32, lg.shape, 1).astype(F32)
    big = float(LANE)
    gl = jnp.where(col < n_groups, lg, NEG_INF)
    gmax = jnp.max(gl, axis=-1, keepdims=True)
    g_val = 1.0 / jnp.sum(jnp.exp(gl - gmax), axis=-1, keepdims=True)
    g_idx = jnp.min(jnp.where(gl == gmax, col, big), axis=-1, keepdims=True)
    lo = n_groups + g_idx * per_group
    el = jnp.where((col >= lo) & (col < lo + per_group), lg, NEG_INF)
    m1 = jnp.max(el, axis=-1, keepdims=True)
    i1 = jnp.min(jnp.where(el == m1, col, big), axis=-1, keepdims=True)
    el2 = jnp.where(col == i1, NEG_INF, el)
    m2 = jnp.max(el2, axis=-1, keepdims=True)
    i2 = jnp.min(jnp.where(el2 == m2, col, big), axis=-1, keepdims=True)
    r = jnp.exp(m2 - m1)
    w0 = g_val / (1.0 + r)
    w1 = g_val * r / (1.0 + r)
    rt = jnp.where(col == ROUTE_ID0, i1 - n_groups, 0.0)
    rt = jnp.where(col == ROUTE_ID1, i2 - n_groups, rt)
    rt = jnp.where(col == ROUTE_W0, w0, rt)
    rt = jnp.where(col == ROUTE_W1, w1, rt)
    rt_ref[...] = rt


def _wo(path_a, path_b, w_o, norm_g, w_r, b_r, n_groups, per_group):
    d = w_o.shape[0]
    ma, mb = path_a[1].shape[0], path_b[1].shape[0]
    tm = _tile(min(ma, mb), 512, BF16_ROWS)
    assert ma % tm == 0 and mb % tm == 0
    na, nb = ma // tm, mb // tm
    tiles = (lambda i: jnp.minimum(i, na - 1), lambda i: jnp.maximum(i - na, 0))
    in_specs, args = [], []
    for (merged, x, gate1, shift2, scale2, seq_len), tile in zip((path_a, path_b), tiles):
        row = pl.BlockSpec((tm, d), lambda i, tile=tile: (tile(i), 0))
        in_specs += [row, row]
        args += [merged, x]
        for vec in (gate1, shift2, scale2):
            arr, spec = _rowvec(vec, seq_len, tm, tile)
            in_specs.append(spec)
            args.append(arr)
    const = lambda shape: pl.BlockSpec(shape, lambda i: (0, 0))
    in_specs += [pl.BlockSpec((d, d), lambda i: (0, 0), pipeline_mode=pl.Buffered(1)),
                 const((1, d)), const((d, 2 * LANE)), const((1, LANE))]
    args += [w_o, norm_g.reshape(1, d), w_r, b_r]
    row = pl.BlockSpec((tm, d), lambda i: (i, 0))
    return pl.pallas_call(
        functools.partial(_wo_kernel, n_tiles_a=na, n_groups=n_groups, per_group=per_group),
        grid=(na + nb,),
        in_specs=in_specs,
        out_specs=[row, row, pl.BlockSpec((tm, LANE), lambda i: (i, 0))],
        out_shape=[
            jax.ShapeDtypeStruct((ma + mb, d), F32),
            jax.ShapeDtypeStruct((ma + mb, d), F32),
            jax.ShapeDtypeStruct((ma + mb, LANE), F32),
        ],
        compiler_params=_params("arbitrary"),
        name="wo_norm_router",
    )(*args)


RING_SLOTS = 3
DMA_QUEUES = 2


def _gather_start(idx_ref, src_hbm, dst, slot, sem, n_rows, first=0):
    for r in range(first, first + n_rows):
        pltpu.make_async_copy(src_hbm.at[pl.ds(idx_ref[0, 0, r], 1)], dst.at[slot, pl.ds(r, 1)], sem.at[slot]).start(
            priority=r % DMA_QUEUES)


def _gather_wait(src_hbm, dst, slot, sem, n_rows):
    pltpu.make_async_copy(src_hbm.at[pl.ds(0, n_rows)], dst.at[slot], sem.at[slot]).wait()


def _expert_kernel(te_ref, tv_ref, tok0_ref, tok1_ref, tokn_ref, h2_hbm, w1_ref, w3_ref, w2_ref, o_ref, xbuf, w1b, w3b, w2b, sem):
    i = pl.program_id(0)
    tm = xbuf.shape[1]
    slot = lax.rem(i, RING_SLOTS)
    valid = tv_ref[i] == 1
    started = jnp.where(i < 2, tv_ref[0], tv_ref[jnp.maximum(i - 2, 0)]) == 1

    @pl.when(jnp.logical_and(i == 0, valid))
    def _():
        _gather_start(tok0_ref, h2_hbm, xbuf, 0, sem, tm)
        _gather_start(tok1_ref, h2_hbm, xbuf, 1, sem, tm)

    @pl.when(valid)
    def _():
        _gather_wait(h2_hbm, xbuf, slot, sem, tm)

        @pl.when(jnp.logical_or(i == 0, te_ref[i] != te_ref[jnp.maximum(i - 1, 0)]))
        def _():
            w1b[...] = w1_ref[0].astype(BF16)
            w3b[...] = w3_ref[0].astype(BF16)
            w2b[...] = w2_ref[0].astype(BF16)

        nslot = lax.rem(i + 2, RING_SLOTS)
        x = xbuf[slot].astype(BF16)
        a = jnp.dot(x, w1b[...], preferred_element_type=F32)
        _gather_start(tokn_ref, h2_hbm, xbuf, nslot, sem, tm // 2)
        b = jnp.dot(x, w3b[...], preferred_element_type=F32)
        _gather_start(tokn_ref, h2_hbm, xbuf, nslot, sem, tm // 2, first=tm // 2)
        hm = (jax.nn.silu(a) * b).astype(BF16)
        o_ref[...] = jnp.dot(hm, w2b[...], preferred_element_type=F32)

    @pl.when(jnp.logical_not(valid))
    def _():
        @pl.when(started)
        def _():
            _gather_wait(h2_hbm, xbuf, slot, sem, tm)

        o_ref[...] = jnp.zeros(o_ref.shape, F32)


def _experts(h2_all, tile_expert, tile_valid, tok_sorted, w1, w3, w2, tm):
    n_tiles = tile_expert.shape[0]
    _, d, de = w1.shape
    tok3 = tok_sorted.reshape(n_tiles, 1, tm)
    assert n_tiles >= RING_SLOTS
    idx_spec = lambda tile: pl.BlockSpec((1, 1, tm), lambda i, te, tv: (tile(i), 0, 0), memory_space=pltpu.SMEM)
    grid_spec = pltpu.PrefetchScalarGridSpec(
        num_scalar_prefetch=2,
        grid=(n_tiles,),
        in_specs=[
            idx_spec(lambda i: 0),
            idx_spec(lambda i: 1),
            idx_spec(lambda i: jnp.minimum(i + 2, n_tiles - 1)),
            pl.BlockSpec(memory_space=pl.ANY),
            pl.BlockSpec((1, d, de), lambda i, te, tv: (te[i], 0, 0)),
            pl.BlockSpec((1, d, de), lambda i, te, tv: (te[i], 0, 0)),
            pl.BlockSpec((1, de, d), lambda i, te, tv: (te[i], 0, 0)),
        ],
        out_specs=pl.BlockSpec((tm, d), lambda i, te, tv: (i, 0)),
        scratch_shapes=[
            pltpu.VMEM((RING_SLOTS, tm, d), F32),
            pltpu.VMEM((d, de), BF16),
            pltpu.VMEM((d, de), BF16),
            pltpu.VMEM((de, d), BF16),
            pltpu.SemaphoreType.DMA((RING_SLOTS,)),
        ],
    )
    return pl.pallas_call(
        _expert_kernel,
        grid_spec=grid_spec,
        out_shape=jax.ShapeDtypeStruct((n_tiles * tm, d), F32),
        compiler_params=_params("arbitrary"),
        name="experts",
    )(tile_expert, tile_valid, tok3, tok3, tok3, h2_all, w1, w3, w2)


COMBINE_CHUNKS = 4


def _combine_kernel(d_ref, dn_ref, ys_hbm, x1_ref, rt_ref, g2a_ref, g2b_ref, ng_ref, ya_ref, yb_ref, ybuf, sem, *, n_tiles_a):
    i = pl.program_id(0)
    tm = x1_ref.shape[0]
    slot = lax.rem(i, 2)

    @pl.when(i == 0)
    def _():
        _gather_start(d_ref, ys_hbm, ybuf, 0, sem, 2 * tm)

    _gather_wait(ys_hbm, ybuf, slot, sem, 2 * tm)
    ng = ng_ref[...]
    ck = tm // COMBINE_CHUNKS

    def tile(g2_ref, y_ref):
        g2 = _bc(g2_ref, tm)
        for c in range(COMBINE_CHUNKS):
            _gather_start(dn_ref, ys_hbm, ybuf, 1 - slot, sem, 2 * ck, first=2 * ck * c)
            rows = slice(c * ck, (c + 1) * ck)
            rt = rt_ref[rows, :]
            y0 = ybuf[slot, c * ck:(c + 1) * ck]
            y1 = ybuf[slot, tm + c * ck:tm + (c + 1) * ck]
            moe = rt[:, ROUTE_W0:ROUTE_W0 + 1] * y0 + rt[:, ROUTE_W1:ROUTE_W1 + 1] * y1
            g2c = g2 if g2.shape[0] == 1 else g2[rows]
            y_ref[rows, :] = _rms(x1_ref[rows, :] + g2c * moe, ng)

    @pl.when(i < n_tiles_a)
    def _():
        tile(g2a_ref, ya_ref)

    @pl.when(i >= n_tiles_a)
    def _():
        tile(g2b_ref, yb_ref)

    @pl.when(i == pl.num_programs(0) - 1)
    def _():
        _gather_wait(ys_hbm, ybuf, 1 - slot, sem, 2 * tm)


def _combine(ys, dest, x1_all, route_all, path_a, path_b, normf_g):
    d = x1_all.shape[1]
    (g2a, seq_a, ma), (g2b, seq_b, mb) = path_a, path_b
    tm = _tile(min(ma, mb, seq_a), 256, 8)
    assert ma % tm == 0 and mb % tm == 0 and tm % COMBINE_CHUNKS == 0 and (tm <= seq_b or tm % seq_b == 0)
    na, nb = ma // tm, mb // tm
    d3 = dest.reshape(2, na + nb, tm).transpose(1, 0, 2).reshape(na + nb, 1, 2 * tm)
    d3 = jnp.concatenate([d3, jnp.zeros((1, 1, 2 * tm), I32)], axis=0)
    tile_a, tile_b = (lambda i: jnp.minimum(i, na - 1)), (lambda i: jnp.maximum(i - na, 0))
    g2a, g2a_spec = _rowvec(g2a, seq_a, tm, tile_a)
    g2b, g2b_spec = _rowvec(g2b, seq_b, tm, tile_b)
    return pl.pallas_call(
        functools.partial(_combine_kernel, n_tiles_a=na),
        grid=(na + nb,),
        in_specs=[
            pl.BlockSpec((1, 1, 2 * tm), lambda i: (i, 0, 0), memory_space=pltpu.SMEM),
            pl.BlockSpec((1, 1, 2 * tm), lambda i: (i + 1, 0, 0), memory_space=pltpu.SMEM),
            pl.BlockSpec(memory_space=pl.ANY),
            pl.BlockSpec((tm, d), lambda i: (i, 0)),
            pl.BlockSpec((tm, LANE), lambda i: (i, 0)),
            g2a_spec,
            g2b_spec,
            pl.BlockSpec((1, d), lambda i: (0, 0)),
        ],
        out_specs=[pl.BlockSpec((tm, d), lambda i: (tile_a(i), 0)), pl.BlockSpec((tm, d), lambda i: (tile_b(i), 0))],
        out_shape=[jax.ShapeDtypeStruct((ma, d), F32), jax.ShapeDtypeStruct((mb, d), F32)],
        scratch_shapes=[pltpu.VMEM((2, 2 * tm, d), F32), pltpu.SemaphoreType.DMA((2,))],
        compiler_params=_params("arbitrary"),
        name="combine_final_norm",
    )(d3, d3, ys, x1_all, route_all, g2a, g2b, normf_g.reshape(1, d))


def _route_plan(route, n_experts, tm):
    t = route.shape[0]
    s = TOP_K * t
    e_flat = jnp.concatenate([route[:, ROUTE_ID0], route[:, ROUTE_ID1]]).astype(I32)
    onehot = (e_flat[:, None] == jnp.arange(n_experts, dtype=I32)[None, :]).astype(I32)
    csum = jnp.cumsum(onehot, axis=0)
    counts = csum[-1]
    padded = (counts + tm - 1) // tm * tm
    ends = jnp.cumsum(padded)
    dest = jnp.sum(onehot * (csum - 1 + (ends - padded)[None, :]), axis=1)
    n_tiles = pl.cdiv(s, tm) + n_experts + 2
    tile_start = jnp.arange(n_tiles, dtype=I32) * tm
    tile_valid = (tile_start < ends[-1]).astype(I32)
    te = jnp.minimum(jnp.sum((ends[None, :] <= tile_start[:, None]).astype(I32), axis=1), n_experts - 1)
    last_used = jnp.maximum(ends[-1] // tm - 1, 0)
    tile_expert = jnp.where(tile_valid == 1, te, te[last_used])
    tok_sorted = jnp.zeros((n_tiles * tm,), I32).at[dest].set(
        jnp.arange(s, dtype=I32) % t, unique_indices=True, mode="promise_in_bounds")
    return dest.reshape(TOP_K, t), tok_sorted, tile_expert, tile_valid


def kernel(x_prompt, x_sample, cache_k, cache_v, cache_logf, state_conv, c_prompt, c_sample, w_ada, b_ada, norm1_g, norm2_g, normf_g, w_in, conv_w, conv_b, b_forget, w_br_conv, w_br_attn, w_gate, b_gate, w_o, w_rg, b_rg, w_re, b_re, w1, w3, w2):
    depth = w_ada.shape[0]
    assert depth == 1, "single-layer step"
    l = 0
    bp, sp, d = x_prompt.shape
    bs, ss, _ = x_sample.shape
    _, _, past, n_heads, dh = cache_k.shape
    d_conv = state_conv.shape[-1]
    da = n_heads * dh
    n_groups = w_rg.shape[-1]
    n_experts = w_re.shape[-1]
    per_group = n_experts // n_groups
    assert n_groups + n_experts <= LANE
    scale = LOG2_E * dh ** -0.5

    w_in_bf = w_in[l].astype(BF16)
    q_off = 3 * d_conv
    w_q, w_k, w_v = (w_in_bf[:, q_off + j * da:q_off + (j + 1) * da] for j in range(3))
    hp = _round_up(n_heads, BF16_ROWS)
    wf_t = jnp.zeros((hp, d), BF16).at[:n_heads].set(w_in[l][:, 3 * d_conv + 3 * da:].T.astype(BF16))
    b_f = jnp.zeros((hp, 1), F32).at[:n_heads, 0].set(b_forget[l])
    w_brc_bf = w_br_conv[l].astype(BF16)
    w_bra_bf = w_br_attn[l].astype(BF16)
    w_gate_bf = w_gate[l].astype(BF16)
    b_gate2 = b_gate[l].reshape(1, 2 * d)
    w_o_bf = w_o[l].astype(BF16)
    w_r32 = jnp.pad(jnp.concatenate([w_rg[l], w_re[l]], axis=1), ((0, 0), (0, LANE - n_groups - n_experts)))
    w_r_hi = w_r32.astype(BF16)
    w_r = jnp.concatenate([w_r_hi, (w_r32 - w_r_hi.astype(F32)).astype(BF16)], axis=1)
    b_r = jnp.zeros((1, LANE), F32).at[0, :n_groups].set(b_rg[l]).at[0, n_groups:n_groups + n_experts].set(b_re[l])

    mod = _ada(jnp.concatenate([c_prompt, c_sample], axis=0), w_ada[l], b_ada[l])
    mods = {"p": jnp.split(mod[:bp], 6, axis=-1), "s": jnp.split(mod[bp:], 6, axis=-1)}

    def mixer_inputs(x, mod6, seq_len, prev):
        shift1, scale1 = mod6[0], mod6[1]
        h, lf_t = _norm(x, norm1_g[l], shift1, scale1, wf_t, b_f, seq_len)
        y_conv, tail = _conv_branch(h, w_in_bf, conv_w[l], conv_b[l], prev, seq_len, d_conv)
        return h, y_conv, tail, lf_t[:n_heads]

    xp = x_prompt.reshape(bp * sp, d)
    xs = x_sample.reshape(bs * ss, d)
    hp_, ycp, tailp, lftp = mixer_inputs(xp, mods["p"], sp, None)
    hs_, ycs, tails, lfts = mixer_inputs(xs, mods["s"], ss, state_conv[l])
    qp, qs = _proj(hp_, hs_, w_q, BF16, scale)
    kp, ks = _proj(hp_, hs_, w_k, F32)
    vp, vs = _proj(hp_, hs_, w_v, F32)

    lf_p = lftp.reshape(n_heads, bp, sp).transpose(1, 0, 2).reshape(bp * n_heads, sp)
    f_p = _cumsum_lanes(lf_p).reshape(bp * n_heads, 1, sp)
    yap = _attn_prompt(qp, kp, vp, f_p, bp, sp, n_heads, dh)

    lf_new = lfts.reshape(n_heads, bs, ss).transpose(1, 0, 2)
    lf_past = cache_logf[l].astype(F32).transpose(0, 2, 1)
    tot = _round_up(past + LANE, CUMSUM_CHUNK)
    lf_all = jnp.concatenate([lf_past, lf_new, jnp.zeros((bs, n_heads, tot - past - ss), F32)], axis=-1)
    f_s = _cumsum_lanes(lf_all.reshape(bs * n_heads, tot)).reshape(bs, n_heads, tot)
    yas = _attn_sample(qs, cache_k, cache_v, l, ks, vs, f_s, bs, ss, past, n_heads, dh)

    def wo_path(x, mod6, seq_len, h, yc, ya):
        gate1, shift2, scale2 = mod6[2], mod6[3], mod6[4]
        merged = _merge(yc, ya, h, w_brc_bf, w_bra_bf, w_gate_bf, b_gate2)
        return merged, x, gate1, shift2, scale2, seq_len

    x1_all, h2_all, route_all = _wo(
        wo_path(xp, mods["p"], sp, hp_, ycp, yap), wo_path(xs, mods["s"], ss, hs_, ycs, yas),
        w_o_bf, norm2_g[l], w_r, b_r, n_groups, per_group)
    tm_e = 256 if TOP_K * h2_all.shape[0] >= 8192 else 64
    dest, tok_sorted, tile_expert, tile_valid = _route_plan(route_all, n_experts, tm_e)
    ys = _experts(h2_all, tile_expert, tile_valid, tok_sorted, w1[l], w3[l], w2[l], tm_e)

    mp, ms = bp * sp, bs * ss
    y_p, y_s = _combine(ys, dest, x1_all, route_all, (mods["p"][5], sp, mp), (mods["s"][5], ss, ms), normf_g)

    def cache_outs(k, v, lf_t, tail, n_seq, seq_len):
        return (
            k.reshape(1, n_seq, seq_len, n_heads, dh),
            v.reshape(1, n_seq, seq_len, n_heads, dh),
            lf_t.T.reshape(1, n_seq, seq_len, n_heads),
            tail[:, CONV_TAIL_ROWS - 2:, :].reshape(1, n_seq, 2, d_conv),
        )

    return (y_p.reshape(bp, sp, d), y_s.reshape(bs, ss, d)) + cache_outs(kp, vp, lftp, tailp, bp, sp) + cache_outs(ks, vs, lfts, tails, bs, ss)
```

```python
import functools

import jax
import jax.numpy as jnp
from jax import lax
from jax.experimental import pallas as pl
from jax.experimental.pallas import tpu as pltpu

F32 = jnp.float32
BF16 = jnp.bfloat16
I32 = jnp.int32

EPS = 1e-6
NEG_INF = -1e30
LOG2_E = 1.4426950408889634
TOP_K = 2
LANE = 128
BF16_ROWS = 16
VMEM_LIMIT_BYTES = 56 * 1024 * 1024
NT_DIMS = (((1,), (1,)), ((), ()))


def _tile(n, pref, align):
    if n <= pref:
        return n
    t = (pref // align) * align
    while t >= align:
        if n % t == 0:
            return t
        t -= align
    return n


def _round_up(n, m):
    return (n + m - 1) // m * m


def _params(*sem):
    return pltpu.CompilerParams(dimension_semantics=sem, vmem_limit_bytes=VMEM_LIMIT_BYTES)


def _rowvec(vec, seq_len, tm, tile=lambda i: i):
    n_seq, d = vec.shape
    vec = vec.reshape(n_seq, 1, d)
    if tm <= seq_len:
        per = seq_len // tm
        return vec, pl.BlockSpec((1, 1, d), lambda i, *_: (tile(i) // per, 0, 0))
    return vec, pl.BlockSpec((tm // seq_len, 1, d), lambda i, *_: (tile(i), 0, 0))


def _bc(ref, rows):
    v = ref[...]
    k, _, d = v.shape
    if k == 1:
        return v.reshape(1, d)
    return jnp.broadcast_to(v, (k, rows // k, d)).reshape(rows, d)


def _rms(x, g):
    return x * lax.rsqrt(jnp.mean(x * x, axis=-1, keepdims=True) + EPS) * g


def _ada_kernel(c_ref, w_ref, b_ref, o_ref):
    o_ref[...] = jnp.dot(c_ref[...].astype(BF16), w_ref[...].astype(BF16), preferred_element_type=F32) + b_ref[...]


def _ada(c_all, w, b):
    r, d = c_all.shape
    n = w.shape[1]
    tn = _tile(n, 1024, LANE)
    return pl.pallas_call(
        _ada_kernel,
        grid=(n // tn,),
        in_specs=[
            pl.BlockSpec((r, d), lambda j: (0, 0)),
            pl.BlockSpec((d, tn), lambda j: (0, j)),
            pl.BlockSpec((1, tn), lambda j: (0, j)),
        ],
        out_specs=pl.BlockSpec((r, tn), lambda j: (0, j)),
        out_shape=jax.ShapeDtypeStruct((r, n), F32),
        compiler_params=_params("parallel"),
        name="ada",
    )(c_all, w, b.reshape(1, n))


def _norm_kernel(x_ref, g_ref, sh_ref, sc_ref, wf_ref, bf_ref, h_ref, lf_ref):
    tm = x_ref.shape[0]
    h = _rms(x_ref[...], g_ref[...]) * (1.0 + _bc(sc_ref, tm)) + _bc(sh_ref, tm)
    hb = h.astype(h_ref.dtype)
    h_ref[...] = hb
    fl = lax.dot_general(wf_ref[...], hb, NT_DIMS, preferred_element_type=F32)
    lf_ref[...] = jax.nn.log_sigmoid(fl + bf_ref[...])


def _norm(x, g, shift, scale, wf_t, b_f, seq_len):
    m, d = x.shape
    hp = wf_t.shape[0]
    tm = _tile(seq_len, 512, 8) if seq_len >= 512 else _tile(m, 512, seq_len)
    sh, sh_spec = _rowvec(shift, seq_len, tm)
    sc, sc_spec = _rowvec(scale, seq_len, tm)
    const = lambda shape: pl.BlockSpec(shape, lambda i: (0, 0))
    return pl.pallas_call(
        _norm_kernel,
        grid=(m // tm,),
        in_specs=[pl.BlockSpec((tm, d), lambda i: (i, 0)), const((1, d)), sh_spec, sc_spec, const((hp, d)), const((hp, 1))],
        out_specs=[pl.BlockSpec((tm, d), lambda i: (i, 0)), pl.BlockSpec((hp, tm), lambda i: (0, i))],
        out_shape=[jax.ShapeDtypeStruct((m, d), BF16), jax.ShapeDtypeStruct((hp, m), F32)],
        compiler_params=_params("parallel"),
        name="norm_forget",
    )(x, g.reshape(1, d), sh, sc, wf_t, b_f)


def _proj_tile(a_ref, w_ref, o_ref, scale):
    acc = jnp.dot(a_ref[...], w_ref[...], preferred_element_type=F32)
    if scale != 1.0:
        acc = acc * scale
    o_ref[...] = acc.astype(o_ref.dtype)


def _proj_kernel(a_ref, b_ref, w_ref, oa_ref, ob_ref, *, scale, n_tiles_a):
    i = pl.program_id(0)

    @pl.when(i < n_tiles_a)
    def _():
        _proj_tile(a_ref, w_ref, oa_ref, scale)

    @pl.when(i >= n_tiles_a)
    def _():
        _proj_tile(b_ref, w_ref, ob_ref, scale)


def _proj(a, b, w, out_dtype, scale=1.0):
    (ma, k), mb, n = a.shape, b.shape[0], w.shape[1]
    tm = _tile(min(ma, mb), 512, BF16_ROWS)
    assert ma % tm == 0 and mb % tm == 0
    na, nb = ma // tm, mb // tm
    spec_a = lambda cols: pl.BlockSpec((tm, cols), lambda i: (jnp.minimum(i, na - 1), 0))
    spec_b = lambda cols: pl.BlockSpec((tm, cols), lambda i: (jnp.maximum(i - na, 0), 0))
    return pl.pallas_call(
        functools.partial(_proj_kernel, scale=scale, n_tiles_a=na),
        grid=(na + nb,),
        in_specs=[spec_a(k), spec_b(k), pl.BlockSpec((k, n), lambda i: (0, 0))],
        out_specs=[spec_a(n), spec_b(n)],
        out_shape=[jax.ShapeDtypeStruct((ma, n), out_dtype), jax.ShapeDtypeStruct((mb, n), out_dtype)],
        compiler_params=_params("arbitrary"),
        name="proj",
    )(a, b, w)


CONV_TAIL_ROWS = 8


def _conv_kernel(*refs, seq_len, has_prev):
    if has_prev:
        h_ref, wb_ref, wc_ref, wv_ref, cw_ref, cb_ref, p1_ref, p2_ref, y_ref, tail_ref = refs
    else:
        h_ref, wb_ref, wc_ref, wv_ref, cw_ref, cb_ref, y_ref, tail_ref = refs
    h = h_ref[...]
    bg = jnp.dot(h, wb_ref[...], preferred_element_type=F32)
    cg = jnp.dot(h, wc_ref[...], preferred_element_type=F32)
    vc = jnp.dot(h, wv_ref[...], preferred_element_type=F32)
    u = cg * vc
    tm = u.shape[0]
    t = lax.broadcasted_iota(I32, (tm, 1), 0)
    if tm > seq_len:
        t = jnp.bitwise_and(t, seq_len - 1) if seq_len & (seq_len - 1) == 0 else lax.rem(t, seq_len)
    u1 = jnp.where(t >= 1, pltpu.roll(u, 1, 0), p1_ref[...] if has_prev else 0.0)
    u2 = jnp.where(t >= 2, pltpu.roll(u, 2, 0), p2_ref[...] if has_prev else 0.0)
    cw = cw_ref[...]
    conv = cb_ref[...] + ((cw[0:1] * u2 + cw[1:2] * u1) + cw[2:3] * u)
    y_ref[...] = (bg * conv).astype(y_ref.dtype)
    for s in range(tm // seq_len):
        tail_ref[s] = u[(s + 1) * seq_len - CONV_TAIL_ROWS:(s + 1) * seq_len, :]


def _conv_branch(h, w_in_bf, conv_w, conv_b, prev, seq_len, d_conv):
    m, d = h.shape
    assert conv_w.shape[0] == 3 and seq_len >= CONV_TAIL_ROWS
    tm = seq_len if seq_len >= 512 else m
    assert m % tm == 0 and tm % seq_len == 0
    tc = _tile(d_conv, 256, LANE)
    nc = d_conv // tc
    n_seq = m // seq_len
    seq_per_tile = tm // seq_len
    in_specs = [
        pl.BlockSpec((tm, d), lambda i, j: (i, 0)),
        pl.BlockSpec((d, tc), lambda i, j: (0, j)),
        pl.BlockSpec((d, tc), lambda i, j: (0, nc + j)),
        pl.BlockSpec((d, tc), lambda i, j: (0, 2 * nc + j)),
        pl.BlockSpec((3, tc), lambda i, j: (0, j)),
        pl.BlockSpec((1, tc), lambda i, j: (0, j)),
    ]
    args = [h, w_in_bf, w_in_bf, w_in_bf, conv_w, conv_b.reshape(1, d_conv)]
    if prev is not None:
        zeros = jnp.zeros((n_seq, seq_len, d_conv), F32)
        p1 = zeros.at[:, 0].set(prev[:, 1]).reshape(m, d_conv)
        p2 = zeros.at[:, 0].set(prev[:, 0]).at[:, 1].set(prev[:, 1]).reshape(m, d_conv)
        in_specs += [pl.BlockSpec((tm, tc), lambda i, j: (i, j))] * 2
        args += [p1, p2]
    return pl.pallas_call(
        functools.partial(_conv_kernel, seq_len=seq_len, has_prev=prev is not None),
        grid=(m // tm, nc),
        in_specs=in_specs,
        out_specs=[
            pl.BlockSpec((tm, tc), lambda i, j: (i, j)),
            pl.BlockSpec((seq_per_tile, CONV_TAIL_ROWS, tc), lambda i, j: (i, 0, j)),
        ],
        out_shape=[
            jax.ShapeDtypeStruct((m, d_conv), BF16),
            jax.ShapeDtypeStruct((n_seq, CONV_TAIL_ROWS, d_conv), F32),
        ],
        compiler_params=_params("parallel", "parallel"),
        name="conv_branch",
    )(*args)


CUMSUM_CHUNK = LANE


def _cumsum_kernel(x_ref, o_ref):
    r, n = x_ref.shape
    c = CUMSUM_CHUNK
    tri = (lax.broadcasted_iota(I32, (c, c), 0) <= lax.broadcasted_iota(I32, (c, c), 1)).astype(BF16)
    carry = jnp.zeros((r, 1), F32)
    for j in range(n // c):
        x = x_ref[:, j * c:(j + 1) * c]
        cs = carry
        for _ in range(3):
            piece = x.astype(BF16)
            cs = cs + jnp.dot(piece, tri, preferred_element_type=F32)
            x = x - piece.astype(F32)
        o_ref[:, j * c:(j + 1) * c] = cs
        carry = cs[:, c - 1:c]


def _cumsum_lanes(x):
    r, n = x.shape
    tr = _tile(r, 128, 8)
    return pl.pallas_call(
        _cumsum_kernel,
        grid=(r // tr,),
        in_specs=[pl.BlockSpec((tr, n), lambda i: (i, 0))],
        out_specs=pl.BlockSpec((tr, n), lambda i: (i, 0)),
        out_shape=jax.ShapeDtypeStruct((r, n), F32),
        compiler_params=_params("parallel"),
        name="cumsum",
    )(x)


def _attn_prompt_kernel(q_ref, k_ref, v_ref, f_ref, o_ref, *, tq, dh):
    seq_len = q_ref.shape[0]
    heads = q_ref.shape[1] // dh
    ones_col = (lax.broadcasted_iota(I32, (seq_len, dh), 1) == 0).astype(BF16)
    kb, vb, f = [], [], []
    for h in range(heads):
        cols = slice(h * dh, (h + 1) * dh)
        kb.append(k_ref[:, cols].astype(BF16))
        vb.append(jnp.concatenate([v_ref[:, cols].astype(BF16), ones_col], axis=1))
        f.append(f_ref[h] * LOG2_E)
    causal = lax.broadcasted_iota(I32, (tq, tq), 1) <= lax.broadcasted_iota(I32, (tq, tq), 0)
    for qi in range(seq_len // tq):
        lo = qi * tq
        for h in range(heads):
            cols = slice(h * dh, (h + 1) * dh)
            q = q_ref[lo:lo + tq, cols]
            sd = lax.dot_general(q, kb[h][lo:lo + tq], NT_DIMS, preferred_element_type=F32) - f[h][:, lo:lo + tq]
            sd = jnp.where(causal, sd, NEG_INF)
            m = jnp.max(sd, axis=-1, keepdims=True)
            if qi > 0:
                sp = lax.dot_general(q, kb[h][:lo], NT_DIMS, preferred_element_type=F32) - f[h][:, :lo]
                m = jnp.maximum(m, jnp.max(sp, axis=-1, keepdims=True))
                ol = jnp.dot(jnp.exp2(sp - m).astype(BF16), vb[h][:lo], preferred_element_type=F32)
            old = jnp.dot(jnp.exp2(sd - m).astype(BF16), vb[h][lo:lo + tq], preferred_element_type=F32)
            ol = ol + old if qi > 0 else old
            o_ref[lo:lo + tq, cols] = (ol[:, :dh] / ol[:, dh:dh + 1]).astype(o_ref.dtype)


ATTN_HEADS_PER_STEP = 4


def _attn_prompt(q, k, v, f, n_seq, seq_len, n_heads, dh):
    m = n_seq * seq_len
    tq = _tile(seq_len, 256, LANE)
    hps = ATTN_HEADS_PER_STEP if n_heads % ATTN_HEADS_PER_STEP == 0 else 1
    groups = n_heads // hps
    blk = pl.BlockSpec((seq_len, hps * dh), lambda b, g: (b, g))
    return pl.pallas_call(
        functools.partial(_attn_prompt_kernel, tq=tq, dh=dh),
        grid=(n_seq, groups),
        in_specs=[blk, blk, blk, pl.BlockSpec((hps, 1, seq_len), lambda b, g: (b * groups + g, 0, 0))],
        out_specs=blk,
        out_shape=jax.ShapeDtypeStruct((m, n_heads * dh), BF16),
        compiler_params=_params("parallel", "parallel"),
        name="attn_prompt",
    )(q, k, v, f)


def _attn_sample_kernel(q_ref, kc_ref, vc_ref, fp_ref, kn_ref, vn_ref, fn_ref, o_ref, m_sc, l_sc, acc_sc, *, n_heads, dh):
    kt = pl.program_id(1)
    lq = q_ref.shape[0]

    @pl.when(kt == 0)
    def _():
        m_sc[...] = jnp.full(m_sc.shape, NEG_INF, F32)
        l_sc[...] = jnp.zeros(l_sc.shape, F32)
        acc_sc[...] = jnp.zeros(acc_sc.shape, F32)

    q = q_ref[...]

    def update(k_heads, v_heads, fk, mask):
        s_heads = []
        for h in range(n_heads):
            s = lax.dot_general(q[:, h * dh:(h + 1) * dh], k_heads[h], NT_DIMS, preferred_element_type=F32)
            s = s - fk[h:h + 1, :]
            if mask is not None:
                s = jnp.where(mask, s, NEG_INF)
            s_heads.append(s)
        s = jnp.concatenate(s_heads, axis=0)
        m_prev = m_sc[...]
        m_new = jnp.maximum(m_prev, jnp.max(s, axis=-1, keepdims=True))
        alpha = jnp.exp2(m_prev - m_new)
        p = jnp.exp2(s - m_new)
        l_sc[...] = alpha * l_sc[...] + jnp.sum(p, axis=-1, keepdims=True)
        pb = p.astype(BF16)
        o_heads = [jnp.dot(pb[h * lq:(h + 1) * lq], v_heads[h], preferred_element_type=F32) for h in range(n_heads)]
        acc_sc[...] = alpha * acc_sc[...] + jnp.concatenate(o_heads, axis=0)
        m_sc[...] = m_new

    def cached_heads(ref):
        x = jnp.swapaxes(ref[0, 0].astype(BF16), 0, 1)
        return [x[h] for h in range(n_heads)]

    update(cached_heads(kc_ref), cached_heads(vc_ref), fp_ref[0] * LOG2_E, None)

    @pl.when(kt == pl.num_programs(1) - 1)
    def _():
        pad = jnp.zeros((LANE - lq, n_heads * dh), BF16)
        kn = jnp.concatenate([kn_ref[...].astype(BF16), pad], axis=0)
        vn = jnp.concatenate([vn_ref[...].astype(BF16), pad], axis=0)
        mask = lax.broadcasted_iota(I32, (lq, LANE), 1) <= lax.broadcasted_iota(I32, (lq, LANE), 0)
        heads = lambda x: [x[:, h * dh:(h + 1) * dh] for h in range(n_heads)]
        update(heads(kn), heads(vn), fn_ref[0] * LOG2_E, mask)
        out = acc_sc[...] / l_sc[...]
        o_ref[...] = jnp.concatenate([out[h * lq:(h + 1) * lq] for h in range(n_heads)], axis=1).astype(o_ref.dtype)


def _attn_sample(q, cache_k, cache_v, layer, k_new, v_new, f_all, n_seq, lq, past, n_heads, dh):
    da = n_heads * dh
    tk = _tile(past, 1024, LANE)
    assert lq <= LANE and lq % BF16_ROWS == 0 and past % LANE == 0
    qblk = pl.BlockSpec((lq, da), lambda b, t: (b, 0))
    cblk = pl.BlockSpec((1, 1, tk, n_heads, dh), lambda b, t: (layer, b, t, 0, 0))
    return pl.pallas_call(
        functools.partial(_attn_sample_kernel, n_heads=n_heads, dh=dh),
        grid=(n_seq, past // tk),
        in_specs=[
            qblk,
            cblk,
            cblk,
            pl.BlockSpec((1, n_heads, tk), lambda b, t: (b, 0, t)),
            qblk,
            qblk,
            pl.BlockSpec((1, n_heads, LANE), lambda b, t: (b, 0, past // LANE)),
        ],
        out_specs=qblk,
        out_shape=jax.ShapeDtypeStruct((n_seq * lq, da), BF16),
        scratch_shapes=[
            pltpu.VMEM((n_heads * lq, 1), F32),
            pltpu.VMEM((n_heads * lq, 1), F32),
            pltpu.VMEM((n_heads * lq, dh), F32),
        ],
        compiler_params=_params("parallel", "arbitrary"),
        name="attn_sample",
    )(q, cache_k, cache_v, f_all, k_new, v_new, f_all)


def _merge_kernel(yc_ref, ya_ref, h_ref, wc_ref, wa_ref, wgc_ref, wga_ref, bgc_ref, bga_ref, o_ref):
    h = h_ref[...]
    a = jnp.dot(yc_ref[...], wc_ref[...], preferred_element_type=F32)
    b = jnp.dot(ya_ref[...], wa_ref[...], preferred_element_type=F32)
    gc = jax.nn.sigmoid(jnp.dot(h, wgc_ref[...], preferred_element_type=F32) + bgc_ref[...])
    ga = jax.nn.sigmoid(jnp.dot(h, wga_ref[...], preferred_element_type=F32) + bga_ref[...])
    o_ref[...] = (gc * a + ga * b).astype(o_ref.dtype)


def _merge(yc, ya, h, w_brc, w_bra, w_gate, b_gate):
    m, d = h.shape
    dc, da = yc.shape[1], ya.shape[1]
    tm = _tile(m, 1024, BF16_ROWS)
    tn = _tile(d, 512, LANE)
    nj = d // tn
    return pl.pallas_call(
        _merge_kernel,
        grid=(m // tm, nj),
        in_specs=[
            pl.BlockSpec((tm, dc), lambda i, j: (i, 0)),
            pl.BlockSpec((tm, da), lambda i, j: (i, 0)),
            pl.BlockSpec((tm, d), lambda i, j: (i, 0)),
            pl.BlockSpec((dc, tn), lambda i, j: (0, j)),
            pl.BlockSpec((da, tn), lambda i, j: (0, j)),
            pl.BlockSpec((d, tn), lambda i, j: (0, j)),
            pl.BlockSpec((d, tn), lambda i, j: (0, nj + j)),
            pl.BlockSpec((1, tn), lambda i, j: (0, j)),
            pl.BlockSpec((1, tn), lambda i, j: (0, nj + j)),
        ],
        out_specs=pl.BlockSpec((tm, tn), lambda i, j: (i, j)),
        out_shape=jax.ShapeDtypeStruct((m, d), BF16),
        compiler_params=_params("parallel", "parallel"),
        name="merge",
    )(yc, ya, h, w_brc, w_bra, w_gate, w_gate, b_gate, b_gate)


ROUTE_ID0, ROUTE_ID1, ROUTE_W0, ROUTE_W1 = 0, 1, 2, 3


def _wo_kernel(*refs, n_tiles_a, n_groups, per_group):
    a_refs, b_refs, (wo_ref, ng_ref, wr_ref, br_ref), outs = refs[0:5], refs[5:10], refs[10:14], refs[14:17]
    i = pl.program_id(0)

    @pl.when(i < n_tiles_a)
    def _():
        _wo_tile(*a_refs, wo_ref, ng_ref, wr_ref, br_ref, *outs, n_groups=n_groups, per_group=per_group)

    @pl.when(i >= n_tiles_a)
    def _():
        _wo_tile(*b_refs, wo_ref, ng_ref, wr_ref, br_ref, *outs, n_groups=n_groups, per_group=per_group)


WO_SUBTILES = 2


def _wo_tile(mg_ref, x_ref, g1_ref, sh_ref, sc_ref, wo_ref, ng_ref, wr_ref, br_ref, x1_ref, h2_ref, rt_ref, *, n_groups, per_group):
    tm = x_ref.shape[0]
    sub = tm // WO_SUBTILES if tm % (WO_SUBTILES * BF16_ROWS) == 0 else tm
    for r0 in range(0, tm, sub):
        rows = slice(r0, r0 + sub)
        vec = lambda ref: _bc(ref, tm) if ref.shape[0] == 1 else _bc(ref, tm)[rows, :]
        _wo_rows(mg_ref[rows, :], x_ref[rows, :], vec(g1_ref), vec(sh_ref), vec(sc_ref), wo_ref, ng_ref, wr_ref, br_ref,
                 x1_ref.at[rows, :], h2_ref.at[rows, :], rt_ref.at[rows, :], n_groups=n_groups, per_group=per_group)


def _wo_rows(mg, x, g1, sh, sc, wo_ref, ng_ref, wr_ref, br_ref, x1_ref, h2_ref, rt_ref, *, n_groups, per_group):
    o = jnp.dot(mg, wo_ref[...], preferred_element_type=F32)
    x1 = x + g1 * o
    x1_ref[...] = x1
    h2 = _rms(x1, ng_ref[...]) * (1.0 + sc) + sh
    h2_ref[...] = h2
    wr = wr_ref[...]
    hi = h2.astype(BF16)
    lo = (h2 - hi.astype(F32)).astype(BF16)
    r_hi = jnp.dot(hi, wr, preferred_element_type=F32)
    r_lo = jnp.dot(lo, wr[:, :LANE], preferred_element_type=F32)
    lg = r_hi[:, :LANE] + (r_hi[:, LANE:] + r_lo) + br_ref[...]
    col = lax.broadcasted_iota(I32, lg.shape, 1).astype(F32)
    big = float(LANE)
    gl = jnp.where(col < n_groups, lg, NEG_INF)
    gmax = jnp.max(gl, axis=-1, keepdims=True)
    g_val = 1.0 / jnp.sum(jnp.exp(gl - gmax), axis=-1, keepdims=True)
    g_idx = jnp.min(jnp.where(gl == gmax, col, big), axis=-1, keepdims=True)
    lo = n_groups + g_idx * per_group
    el = jnp.where((col >= lo) & (col < lo + per_group), lg, NEG_INF)
    m1 = jnp.max(el, axis=-1, keepdims=True)
    i1 = jnp.min(jnp.where(el == m1, col, big), axis=-1, keepdims=True)
    el2 = jnp.where(col == i1, NEG_INF, el)
    m2 = jnp.max(el2, axis=-1, keepdims=True)
    i2 = jnp.min(jnp.where(el2 == m2, col, big), axis=-1, keepdims=True)
    r = jnp.exp(m2 - m1)
    w0 = g_val / (1.0 + r)
    w1 = g_val * r / (1.0 + r)
    rt = jnp.where(col == ROUTE_ID0, i1 - n_groups, 0.0)
    rt = jnp.where(col == ROUTE_ID1, i2 - n_groups, rt)
    rt = jnp.where(col == ROUTE_W0, w0, rt)
    rt = jnp.where(col == ROUTE_W1, w1, rt)
    rt_ref[...] = rt


def _wo(path_a, path_b, w_o, norm_g, w_r, b_r, n_groups, per_group):
    d = w_o.shape[0]
    ma, mb = path_a[1].shape[0], path_b[1].shape[0]
    tm = _tile(min(ma, mb), 512, BF16_ROWS)
    assert ma % tm == 0 and mb % tm == 0
    na, nb = ma // tm, mb // tm
    tiles = (lambda i: jnp.minimum(i, na - 1), lambda i: jnp.maximum(i - na, 0))
    in_specs, args = [], []
    for (merged, x, gate1, shift2, scale2, seq_len), tile in zip((path_a, path_b), tiles):
        row = pl.BlockSpec((tm, d), lambda i, tile=tile: (tile(i), 0))
        in_specs += [row, row]
        args += [merged, x]
        for vec in (gate1, shift2, scale2):
            arr, spec = _rowvec(vec, seq_len, tm, tile)
            in_specs.append(spec)
            args.append(arr)
    const = lambda shape: pl.BlockSpec(shape, lambda i: (0, 0))
    in_specs += [pl.BlockSpec((d, d), lambda i: (0, 0), pipeline_mode=pl.Buffered(1)),
                 const((1, d)), const((d, 2 * LANE)), const((1, LANE))]
    args += [w_o, norm_g.reshape(1, d), w_r, b_r]
    row = pl.BlockSpec((tm, d), lambda i: (i, 0))
    return pl.pallas_call(
        functools.partial(_wo_kernel, n_tiles_a=na, n_groups=n_groups, per_group=per_group),
        grid=(na + nb,),
        in_specs=in_specs,
        out_specs=[row, row, pl.BlockSpec((tm, LANE), lambda i: (i, 0))],
        out_shape=[
            jax.ShapeDtypeStruct((ma + mb, d), F32),
            jax.ShapeDtypeStruct((ma + mb, d), F32),
            jax.ShapeDtypeStruct((ma + mb, LANE), F32),
        ],
        compiler_params=_params("arbitrary"),
        name="wo_norm_router",
    )(*args)


RING_AHEAD = 3
RING_SLOTS = RING_AHEAD + 1
DMA_QUEUES = 2


def _gather_start(idx_ref, src_hbm, dst, slot, sem, n_rows, first=0):
    for r in range(first, first + n_rows):
        pltpu.make_async_copy(src_hbm.at[pl.ds(idx_ref[0, 0, r], 1)], dst.at[slot, pl.ds(r, 1)], sem.at[slot]).start(
            priority=r % DMA_QUEUES)


def _gather_wait(src_hbm, dst, slot, sem, n_rows):
    pltpu.make_async_copy(src_hbm.at[pl.ds(0, n_rows)], dst.at[slot], sem.at[slot]).wait()


def _expert_kernel(te_ref, tv_ref, *refs):
    tok_first = refs[:RING_AHEAD]
    tokn_ref, h2_hbm, w1_ref, w3_ref, w2_ref, o_ref, xbuf, w1b, w3b, w2b, sem = refs[RING_AHEAD:]
    i = pl.program_id(0)
    tm = xbuf.shape[1]
    slot = lax.rem(i, RING_SLOTS)
    valid = tv_ref[i] == 1
    started = jnp.where(i < RING_AHEAD, tv_ref[0], tv_ref[jnp.maximum(i - RING_AHEAD, 0)]) == 1

    @pl.when(jnp.logical_and(i == 0, valid))
    def _():
        for j, tok_ref in enumerate(tok_first):
            _gather_start(tok_ref, h2_hbm, xbuf, j, sem, tm)

    @pl.when(valid)
    def _():
        _gather_wait(h2_hbm, xbuf, slot, sem, tm)

        @pl.when(jnp.logical_or(i == 0, te_ref[i] != te_ref[jnp.maximum(i - 1, 0)]))
        def _():
            w1b[...] = w1_ref[0].astype(BF16)
            w3b[...] = w3_ref[0].astype(BF16)
            w2b[...] = w2_ref[0].astype(BF16)

        nslot = lax.rem(i + RING_AHEAD, RING_SLOTS)
        x = xbuf[slot].astype(BF16)
        a = jnp.dot(x, w1b[...], preferred_element_type=F32)
        _gather_start(tokn_ref, h2_hbm, xbuf, nslot, sem, tm // 2)
        b = jnp.dot(x, w3b[...], preferred_element_type=F32)
        _gather_start(tokn_ref, h2_hbm, xbuf, nslot, sem, tm // 2, first=tm // 2)
        hm = (jax.nn.silu(a) * b).astype(BF16)
        o_ref[...] = jnp.dot(hm, w2b[...], preferred_element_type=F32)

    @pl.when(jnp.logical_not(valid))
    def _():
        @pl.when(started)
        def _():
            _gather_wait(h2_hbm, xbuf, slot, sem, tm)

        o_ref[...] = jnp.zeros(o_ref.shape, F32)


def _experts(h2_all, tile_expert, tile_valid, tok_sorted, w1, w3, w2, tm):
    n_tiles = tile_expert.shape[0]
    _, d, de = w1.shape
    tok3 = tok_sorted.reshape(n_tiles, 1, tm)
    assert n_tiles >= RING_SLOTS
    idx_spec = lambda tile: pl.BlockSpec((1, 1, tm), lambda i, te, tv: (tile(i), 0, 0), memory_space=pltpu.SMEM)
    grid_spec = pltpu.PrefetchScalarGridSpec(
        num_scalar_prefetch=2,
        grid=(n_tiles,),
        in_specs=[idx_spec(lambda i, j=j: j) for j in range(RING_AHEAD)] + [
            idx_spec(lambda i: jnp.minimum(i + RING_AHEAD, n_tiles - 1)),
            pl.BlockSpec(memory_space=pl.ANY),
            pl.BlockSpec((1, d, de), lambda i, te, tv: (te[i], 0, 0)),
            pl.BlockSpec((1, d, de), lambda i, te, tv: (te[i], 0, 0)),
            pl.BlockSpec((1, de, d), lambda i, te, tv: (te[i], 0, 0)),
        ],
        out_specs=pl.BlockSpec((tm, d), lambda i, te, tv: (i, 0)),
        scratch_shapes=[
            pltpu.VMEM((RING_SLOTS, tm, d), F32),
            pltpu.VMEM((d, de), BF16),
            pltpu.VMEM((d, de), BF16),
            pltpu.VMEM((de, d), BF16),
            pltpu.SemaphoreType.DMA((RING_SLOTS,)),
        ],
    )
    return pl.pallas_call(
        _expert_kernel,
        grid_spec=grid_spec,
        out_shape=jax.ShapeDtypeStruct((n_tiles * tm, d), F32),
        compiler_params=_params("arbitrary"),
        name="experts",
    )(tile_expert, tile_valid, *([tok3] * (RING_AHEAD + 1)), h2_all, w1, w3, w2)


COMBINE_CHUNKS = 4


def _combine_kernel(d_ref, dn_ref, ys_hbm, x1_ref, rt_ref, g2a_ref, g2b_ref, ng_ref, ya_ref, yb_ref, ybuf, sem, *, n_tiles_a):
    i = pl.program_id(0)
    tm = x1_ref.shape[0]
    slot = lax.rem(i, 2)

    @pl.when(i == 0)
    def _():
        _gather_start(d_ref, ys_hbm, ybuf, 0, sem, 2 * tm)

    _gather_wait(ys_hbm, ybuf, slot, sem, 2 * tm)
    ng = ng_ref[...]
    ck = tm // COMBINE_CHUNKS

    def tile(g2_ref, y_ref):
        g2 = _bc(g2_ref, tm)
        for c in range(COMBINE_CHUNKS):
            _gather_start(dn_ref, ys_hbm, ybuf, 1 - slot, sem, 2 * ck, first=2 * ck * c)
            rows = slice(c * ck, (c + 1) * ck)
            rt = rt_ref[rows, :]
            y0 = ybuf[slot, c * ck:(c + 1) * ck]
            y1 = ybuf[slot, tm + c * ck:tm + (c + 1) * ck]
            moe = rt[:, ROUTE_W0:ROUTE_W0 + 1] * y0 + rt[:, ROUTE_W1:ROUTE_W1 + 1] * y1
            g2c = g2 if g2.shape[0] == 1 else g2[rows]
            y_ref[rows, :] = _rms(x1_ref[rows, :] + g2c * moe, ng)

    @pl.when(i < n_tiles_a)
    def _():
        tile(g2a_ref, ya_ref)

    @pl.when(i >= n_tiles_a)
    def _():
        tile(g2b_ref, yb_ref)

    @pl.when(i == pl.num_programs(0) - 1)
    def _():
        _gather_wait(ys_hbm, ybuf, 1 - slot, sem, 2 * tm)


def _combine(ys, dest, x1_all, route_all, path_a, path_b, normf_g):
    d = x1_all.shape[1]
    (g2a, seq_a, ma), (g2b, seq_b, mb) = path_a, path_b
    tm = _tile(min(ma, mb, seq_a), 256, 8)
    assert ma % tm == 0 and mb % tm == 0 and tm % COMBINE_CHUNKS == 0 and (tm <= seq_b or tm % seq_b == 0)
    na, nb = ma // tm, mb // tm
    d3 = dest.reshape(2, na + nb, tm).transpose(1, 0, 2).reshape(na + nb, 1, 2 * tm)
    d3 = jnp.concatenate([d3, jnp.zeros((1, 1, 2 * tm), I32)], axis=0)
    tile_a, tile_b = (lambda i: jnp.minimum(i, na - 1)), (lambda i: jnp.maximum(i - na, 0))
    g2a, g2a_spec = _rowvec(g2a, seq_a, tm, tile_a)
    g2b, g2b_spec = _rowvec(g2b, seq_b, tm, tile_b)
    return pl.pallas_call(
        functools.partial(_combine_kernel, n_tiles_a=na),
        grid=(na + nb,),
        in_specs=[
            pl.BlockSpec((1, 1, 2 * tm), lambda i: (i, 0, 0), memory_space=pltpu.SMEM),
            pl.BlockSpec((1, 1, 2 * tm), lambda i: (i + 1, 0, 0), memory_space=pltpu.SMEM),
            pl.BlockSpec(memory_space=pl.ANY),
            pl.BlockSpec((tm, d), lambda i: (i, 0)),
            pl.BlockSpec((tm, LANE), lambda i: (i, 0)),
            g2a_spec,
            g2b_spec,
            pl.BlockSpec((1, d), lambda i: (0, 0)),
        ],
        out_specs=[pl.BlockSpec((tm, d), lambda i: (tile_a(i), 0)), pl.BlockSpec((tm, d), lambda i: (tile_b(i), 0))],
        out_shape=[jax.ShapeDtypeStruct((ma, d), F32), jax.ShapeDtypeStruct((mb, d), F32)],
        scratch_shapes=[pltpu.VMEM((2, 2 * tm, d), F32), pltpu.SemaphoreType.DMA((2,))],
        compiler_params=_params("arbitrary"),
        name="combine_final_norm",
    )(d3, d3, ys, x1_all, route_all, g2a, g2b, normf_g.reshape(1, d))


def _route_plan(route, n_experts, tm):
    t = route.shape[0]
    s = TOP_K * t
    e_flat = jnp.concatenate([route[:, ROUTE_ID0], route[:, ROUTE_ID1]]).astype(I32)
    onehot = (e_flat[:, None] == jnp.arange(n_experts, dtype=I32)[None, :]).astype(I32)
    csum = jnp.cumsum(onehot, axis=0)
    counts = csum[-1]
    padded = (counts + tm - 1) // tm * tm
    ends = jnp.cumsum(padded)
    dest = jnp.sum(onehot * (csum - 1 + (ends - padded)[None, :]), axis=1)
    n_tiles = pl.cdiv(s, tm) + n_experts + RING_AHEAD
    tile_start = jnp.arange(n_tiles, dtype=I32) * tm
    tile_valid = (tile_start < ends[-1]).astype(I32)
    te = jnp.minimum(jnp.sum((ends[None, :] <= tile_start[:, None]).astype(I32), axis=1), n_experts - 1)
    last_used = jnp.maximum(ends[-1] // tm - 1, 0)
    tile_expert = jnp.where(tile_valid == 1, te, te[last_used])
    tok_sorted = jnp.zeros((n_tiles * tm,), I32).at[dest].set(
        jnp.arange(s, dtype=I32) % t, unique_indices=True, mode="promise_in_bounds")
    return dest.reshape(TOP_K, t), tok_sorted, tile_expert, tile_valid


def kernel(x_prompt, x_sample, cache_k, cache_v, cache_logf, state_conv, c_prompt, c_sample, w_ada, b_ada, norm1_g, norm2_g, normf_g, w_in, conv_w, conv_b, b_forget, w_br_conv, w_br_attn, w_gate, b_gate, w_o, w_rg, b_rg, w_re, b_re, w1, w3, w2):
    depth = w_ada.shape[0]
    assert depth == 1, "single-layer step"
    l = 0
    bp, sp, d = x_prompt.shape
    bs, ss, _ = x_sample.shape
    _, _, past, n_heads, dh = cache_k.shape
    d_conv = state_conv.shape[-1]
    da = n_heads * dh
    n_groups = w_rg.shape[-1]
    n_experts = w_re.shape[-1]
    per_group = n_experts // n_groups
    assert n_groups + n_experts <= LANE
    scale = LOG2_E * dh ** -0.5

    w_in_bf = w_in[l].astype(BF16)
    q_off = 3 * d_conv
    w_q, w_k, w_v = (w_in_bf[:, q_off + j * da:q_off + (j + 1) * da] for j in range(3))
    hp = _round_up(n_heads, BF16_ROWS)
    wf_t = jnp.zeros((hp, d), BF16).at[:n_heads].set(w_in[l][:, 3 * d_conv + 3 * da:].T.astype(BF16))
    b_f = jnp.zeros((hp, 1), F32).at[:n_heads, 0].set(b_forget[l])
    w_brc_bf = w_br_conv[l].astype(BF16)
    w_bra_bf = w_br_attn[l].astype(BF16)
    w_gate_bf = w_gate[l].astype(BF16)
    b_gate2 = b_gate[l].reshape(1, 2 * d)
    w_o_bf = w_o[l].astype(BF16)
    w_r32 = jnp.pad(jnp.concatenate([w_rg[l], w_re[l]], axis=1), ((0, 0), (0, LANE - n_groups - n_experts)))
    w_r_hi = w_r32.astype(BF16)
    w_r = jnp.concatenate([w_r_hi, (w_r32 - w_r_hi.astype(F32)).astype(BF16)], axis=1)
    b_r = jnp.zeros((1, LANE), F32).at[0, :n_groups].set(b_rg[l]).at[0, n_groups:n_groups + n_experts].set(b_re[l])

    mod = _ada(jnp.concatenate([c_prompt, c_sample], axis=0), w_ada[l], b_ada[l])
    mods = {"p": jnp.split(mod[:bp], 6, axis=-1), "s": jnp.split(mod[bp:], 6, axis=-1)}

    def mixer_inputs(x, mod6, seq_len, prev):
        shift1, scale1 = mod6[0], mod6[1]
        h, lf_t = _norm(x, norm1_g[l], shift1, scale1, wf_t, b_f, seq_len)
        y_conv, tail = _conv_branch(h, w_in_bf, conv_w[l], conv_b[l], prev, seq_len, d_conv)
        return h, y_conv, tail, lf_t[:n_heads]

    xp = x_prompt.reshape(bp * sp, d)
    xs = x_sample.reshape(bs * ss, d)
    hp_, ycp, tailp, lftp = mixer_inputs(xp, mods["p"], sp, None)
    hs_, ycs, tails, lfts = mixer_inputs(xs, mods["s"], ss, state_conv[l])
    qp, qs = _proj(hp_, hs_, w_q, BF16, scale)
    kp, ks = _proj(hp_, hs_, w_k, F32)
    vp, vs = _proj(hp_, hs_, w_v, F32)

    lf_p = lftp.reshape(n_heads, bp, sp).transpose(1, 0, 2).reshape(bp * n_heads, sp)
    f_p = _cumsum_lanes(lf_p).reshape(bp * n_heads, 1, sp)
    yap = _attn_prompt(qp, kp, vp, f_p, bp, sp, n_heads, dh)

    lf_new = lfts.reshape(n_heads, bs, ss).transpose(1, 0, 2)
    lf_past = cache_logf[l].astype(F32).transpose(0, 2, 1)
    tot = _round_up(past + LANE, CUMSUM_CHUNK)
    lf_all = jnp.concatenate([lf_past, lf_new, jnp.zeros((bs, n_heads, tot - past - ss), F32)], axis=-1)
    f_s = _cumsum_lanes(lf_all.reshape(bs * n_heads, tot)).reshape(bs, n_heads, tot)
    yas = _attn_sample(qs, cache_k, cache_v, l, ks, vs, f_s, bs, ss, past, n_heads, dh)

    def wo_path(x, mod6, seq_len, h, yc, ya):
        gate1, shift2, scale2 = mod6[2], mod6[3], mod6[4]
        merged = _merge(yc, ya, h, w_brc_bf, w_bra_bf, w_gate_bf, b_gate2)
        return merged, x, gate1, shift2, scale2, seq_len

    x1_all, h2_all, route_all = _wo(
        wo_path(xp, mods["p"], sp, hp_, ycp, yap), wo_path(xs, mods["s"], ss, hs_, ycs, yas),
        w_o_bf, norm2_g[l], w_r, b_r, n_groups, per_group)
    tm_e = 256 if TOP_K * h2_all.shape[0] >= 8192 else 64
    dest, tok_sorted, tile_expert, tile_valid = _route_plan(route_all, n_experts, tm_e)
    ys = _experts(h2_all, tile_expert, tile_valid, tok_sorted, w1[l], w3[l], w2[l], tm_e)

    mp, ms = bp * sp, bs * ss
    y_p, y_s = _combine(ys, dest, x1_all, route_all, (mods["p"][5], sp, mp), (mods["s"][5], ss, ms), normf_g)

    def cache_outs(k, v, lf_t, tail, n_seq, seq_len):
        return (
            k.reshape(1, n_seq, seq_len, n_heads, dh),
            v.reshape(1, n_seq, seq_len, n_heads, dh),
            lf_t.T.reshape(1, n_seq, seq_len, n_heads),
            tail[:, CONV_TAIL_ROWS - 2:, :].reshape(1, n_seq, 2, d_conv),
        )

    return (y_p.reshape(bp, sp, d), y_s.reshape(bs, ss, d)) + cache_outs(kp, vp, lftp, tailp, bp, sp) + cache_outs(ks, vs, lfts, tails, bs, ss)
```

```python
import functools

import jax
import jax.numpy as jnp
from jax import lax
from jax.experimental import pallas as pl
from jax.experimental.pallas import tpu as pltpu

F32 = jnp.float32
BF16 = jnp.bfloat16
I32 = jnp.int32

EPS = 1e-6
NEG_INF = -1e30
LOG2_E = 1.4426950408889634
TOP_K = 2
LANE = 128
BF16_ROWS = 16
VMEM_LIMIT_BYTES = 56 * 1024 * 1024
NT_DIMS = (((1,), (1,)), ((), ()))


def _tile(n, pref, align):
    if n <= pref:
        return n
    t = (pref // align) * align
    while t >= align:
        if n % t == 0:
            return t
        t -= align
    return n


def _round_up(n, m):
    return (n + m - 1) // m * m


def _params(*sem):
    return pltpu.CompilerParams(dimension_semantics=sem, vmem_limit_bytes=VMEM_LIMIT_BYTES)


def _rowvec(vec, seq_len, tm, tile=lambda i: i):
    n_seq, d = vec.shape
    vec = vec.reshape(n_seq, 1, d)
    if tm <= seq_len:
        per = seq_len // tm
        return vec, pl.BlockSpec((1, 1, d), lambda i, *_: (tile(i) // per, 0, 0))
    return vec, pl.BlockSpec((tm // seq_len, 1, d), lambda i, *_: (tile(i), 0, 0))


def _bc(ref, rows):
    v = ref[...]
    k, _, d = v.shape
    if k == 1:
        return v.reshape(1, d)
    return jnp.broadcast_to(v, (k, rows // k, d)).reshape(rows, d)


def _rms(x, g):
    return x * lax.rsqrt(jnp.mean(x * x, axis=-1, keepdims=True) + EPS) * g


def _ada_kernel(c_ref, w_ref, b_ref, o_ref):
    o_ref[...] = jnp.dot(c_ref[...].astype(BF16), w_ref[...].astype(BF16), preferred_element_type=F32) + b_ref[...]


def _ada(c_all, w, b):
    r, d = c_all.shape
    n = w.shape[1]
    tn = _tile(n, 1024, LANE)
    return pl.pallas_call(
        _ada_kernel,
        grid=(n // tn,),
        in_specs=[
            pl.BlockSpec((r, d), lambda j: (0, 0)),
            pl.BlockSpec((d, tn), lambda j: (0, j)),
            pl.BlockSpec((1, tn), lambda j: (0, j)),
        ],
        out_specs=pl.BlockSpec((r, tn), lambda j: (0, j)),
        out_shape=jax.ShapeDtypeStruct((r, n), F32),
        compiler_params=_params("parallel"),
        name="ada",
    )(c_all, w, b.reshape(1, n))


def _norm_kernel(x_ref, g_ref, sh_ref, sc_ref, wf_ref, bf_ref, h_ref, lf_ref):
    tm = x_ref.shape[0]
    h = _rms(x_ref[...], g_ref[...]) * (1.0 + _bc(sc_ref, tm)) + _bc(sh_ref, tm)
    hb = h.astype(h_ref.dtype)
    h_ref[...] = hb
    fl = lax.dot_general(wf_ref[...], hb, NT_DIMS, preferred_element_type=F32)
    lf_ref[...] = jax.nn.log_sigmoid(fl + bf_ref[...])


def _norm(x, g, shift, scale, wf_t, b_f, seq_len):
    m, d = x.shape
    hp = wf_t.shape[0]
    tm = _tile(seq_len, 512, 8) if seq_len >= 512 else _tile(m, 512, seq_len)
    sh, sh_spec = _rowvec(shift, seq_len, tm)
    sc, sc_spec = _rowvec(scale, seq_len, tm)
    const = lambda shape: pl.BlockSpec(shape, lambda i: (0, 0))
    return pl.pallas_call(
        _norm_kernel,
        grid=(m // tm,),
        in_specs=[pl.BlockSpec((tm, d), lambda i: (i, 0)), const((1, d)), sh_spec, sc_spec, const((hp, d)), const((hp, 1))],
        out_specs=[pl.BlockSpec((tm, d), lambda i: (i, 0)), pl.BlockSpec((hp, tm), lambda i: (0, i))],
        out_shape=[jax.ShapeDtypeStruct((m, d), BF16), jax.ShapeDtypeStruct((hp, m), F32)],
        compiler_params=_params("parallel"),
        name="norm_forget",
    )(x, g.reshape(1, d), sh, sc, wf_t, b_f)


def _proj_tile(a_ref, w_ref, o_ref, scale):
    acc = jnp.dot(a_ref[...], w_ref[...], preferred_element_type=F32)
    if scale != 1.0:
        acc = acc * scale
    o_ref[...] = acc.astype(o_ref.dtype)


def _proj_kernel(a_ref, b_ref, w_ref, oa_ref, ob_ref, *, scale, n_tiles_a):
    i = pl.program_id(0)

    @pl.when(i < n_tiles_a)
    def _():
        _proj_tile(a_ref, w_ref, oa_ref, scale)

    @pl.when(i >= n_tiles_a)
    def _():
        _proj_tile(b_ref, w_ref, ob_ref, scale)


def _proj(a, b, w, out_dtype, scale=1.0):
    (ma, k), mb, n = a.shape, b.shape[0], w.shape[1]
    tm = _tile(min(ma, mb), 512, BF16_ROWS)
    assert ma % tm == 0 and mb % tm == 0
    na, nb = ma // tm, mb // tm
    spec_a = lambda cols: pl.BlockSpec((tm, cols), lambda i: (jnp.minimum(i, na - 1), 0))
    spec_b = lambda cols: pl.BlockSpec((tm, cols), lambda i: (jnp.maximum(i - na, 0), 0))
    return pl.pallas_call(
        functools.partial(_proj_kernel, scale=scale, n_tiles_a=na),
        grid=(na + nb,),
        in_specs=[spec_a(k), spec_b(k), pl.BlockSpec((k, n), lambda i: (0, 0))],
        out_specs=[spec_a(n), spec_b(n)],
        out_shape=[jax.ShapeDtypeStruct((ma, n), out_dtype), jax.ShapeDtypeStruct((mb, n), out_dtype)],
        compiler_params=_params("arbitrary"),
        name="proj",
    )(a, b, w)


CONV_TAIL_ROWS = 8


def _conv_kernel(*refs, seq_len, has_prev):
    if has_prev:
        h_ref, wb_ref, wc_ref, wv_ref, cw_ref, cb_ref, p1_ref, p2_ref, y_ref, tail_ref = refs
    else:
        h_ref, wb_ref, wc_ref, wv_ref, cw_ref, cb_ref, y_ref, tail_ref = refs
    h = h_ref[...]
    bg = jnp.dot(h, wb_ref[...], preferred_element_type=F32)
    cg = jnp.dot(h, wc_ref[...], preferred_element_type=F32)
    vc = jnp.dot(h, wv_ref[...], preferred_element_type=F32)
    u = cg * vc
    tm = u.shape[0]
    t = lax.broadcasted_iota(I32, (tm, 1), 0)
    if tm > seq_len:
        t = jnp.bitwise_and(t, seq_len - 1) if seq_len & (seq_len - 1) == 0 else lax.rem(t, seq_len)
    u1 = jnp.where(t >= 1, pltpu.roll(u, 1, 0), p1_ref[...] if has_prev else 0.0)
    u2 = jnp.where(t >= 2, pltpu.roll(u, 2, 0), p2_ref[...] if has_prev else 0.0)
    cw = cw_ref[...]
    conv = cb_ref[...] + ((cw[0:1] * u2 + cw[1:2] * u1) + cw[2:3] * u)
    y_ref[...] = (bg * conv).astype(y_ref.dtype)
    for s in range(tm // seq_len):
        tail_ref[s] = u[(s + 1) * seq_len - CONV_TAIL_ROWS:(s + 1) * seq_len, :]


def _conv_branch(h, w_in_bf, conv_w, conv_b, prev, seq_len, d_conv):
    m, d = h.shape
    assert conv_w.shape[0] == 3 and seq_len >= CONV_TAIL_ROWS
    tm = seq_len if seq_len >= 512 else m
    assert m % tm == 0 and tm % seq_len == 0
    tc = _tile(d_conv, 256, LANE)
    nc = d_conv // tc
    n_seq = m // seq_len
    seq_per_tile = tm // seq_len
    in_specs = [
        pl.BlockSpec((tm, d), lambda i, j: (i, 0)),
        pl.BlockSpec((d, tc), lambda i, j: (0, j)),
        pl.BlockSpec((d, tc), lambda i, j: (0, nc + j)),
        pl.BlockSpec((d, tc), lambda i, j: (0, 2 * nc + j)),
        pl.BlockSpec((3, tc), lambda i, j: (0, j)),
        pl.BlockSpec((1, tc), lambda i, j: (0, j)),
    ]
    args = [h, w_in_bf, w_in_bf, w_in_bf, conv_w, conv_b.reshape(1, d_conv)]
    if prev is not None:
        zeros = jnp.zeros((n_seq, seq_len, d_conv), F32)
        p1 = zeros.at[:, 0].set(prev[:, 1]).reshape(m, d_conv)
        p2 = zeros.at[:, 0].set(prev[:, 0]).at[:, 1].set(prev[:, 1]).reshape(m, d_conv)
        in_specs += [pl.BlockSpec((tm, tc), lambda i, j: (i, j))] * 2
        args += [p1, p2]
    return pl.pallas_call(
        functools.partial(_conv_kernel, seq_len=seq_len, has_prev=prev is not None),
        grid=(m // tm, nc),
        in_specs=in_specs,
        out_specs=[
            pl.BlockSpec((tm, tc), lambda i, j: (i, j)),
            pl.BlockSpec((seq_per_tile, CONV_TAIL_ROWS, tc), lambda i, j: (i, 0, j)),
        ],
        out_shape=[
            jax.ShapeDtypeStruct((m, d_conv), BF16),
            jax.ShapeDtypeStruct((n_seq, CONV_TAIL_ROWS, d_conv), F32),
        ],
        compiler_params=_params("parallel", "parallel"),
        name="conv_branch",
    )(*args)


CUMSUM_CHUNK = LANE


def _cumsum_kernel(x_ref, o_ref):
    r, n = x_ref.shape
    c = CUMSUM_CHUNK
    tri = (lax.broadcasted_iota(I32, (c, c), 0) <= lax.broadcasted_iota(I32, (c, c), 1)).astype(BF16)
    carry = jnp.zeros((r, 1), F32)
    for j in range(n // c):
        x = x_ref[:, j * c:(j + 1) * c]
        cs = carry
        for _ in range(3):
            piece = x.astype(BF16)
            cs = cs + jnp.dot(piece, tri, preferred_element_type=F32)
            x = x - piece.astype(F32)
        o_ref[:, j * c:(j + 1) * c] = cs
        carry = cs[:, c - 1:c]


def _cumsum_lanes(x):
    r, n = x.shape
    tr = _tile(r, 128, 8)
    return pl.pallas_call(
        _cumsum_kernel,
        grid=(r // tr,),
        in_specs=[pl.BlockSpec((tr, n), lambda i: (i, 0))],
        out_specs=pl.BlockSpec((tr, n), lambda i: (i, 0)),
        out_shape=jax.ShapeDtypeStruct((r, n), F32),
        compiler_params=_params("parallel"),
        name="cumsum",
    )(x)


def _attn_prompt_kernel(q_ref, k_ref, v_ref, f_ref, o_ref, *, tq, dh):
    seq_len = q_ref.shape[0]
    heads = q_ref.shape[1] // dh
    ones_col = (lax.broadcasted_iota(I32, (seq_len, dh), 1) == 0).astype(BF16)
    kb, vb, f = [], [], []
    for h in range(heads):
        cols = slice(h * dh, (h + 1) * dh)
        kb.append(k_ref[:, cols].astype(BF16))
        vb.append(jnp.concatenate([v_ref[:, cols].astype(BF16), ones_col], axis=1))
        f.append(f_ref[h] * LOG2_E)
    causal = lax.broadcasted_iota(I32, (tq, tq), 1) <= lax.broadcasted_iota(I32, (tq, tq), 0)
    for qi in range(seq_len // tq):
        lo = qi * tq
        for h in range(heads):
            cols = slice(h * dh, (h + 1) * dh)
            q = q_ref[lo:lo + tq, cols]
            sd = lax.dot_general(q, kb[h][lo:lo + tq], NT_DIMS, preferred_element_type=F32) - f[h][:, lo:lo + tq]
            sd = jnp.where(causal, sd, NEG_INF)
            m = jnp.max(sd, axis=-1, keepdims=True)
            if qi > 0:
                sp = lax.dot_general(q, kb[h][:lo], NT_DIMS, preferred_element_type=F32) - f[h][:, :lo]
                m = jnp.maximum(m, jnp.max(sp, axis=-1, keepdims=True))
                ol = jnp.dot(jnp.exp2(sp - m).astype(BF16), vb[h][:lo], preferred_element_type=F32)
            old = jnp.dot(jnp.exp2(sd - m).astype(BF16), vb[h][lo:lo + tq], preferred_element_type=F32)
            ol = ol + old if qi > 0 else old
            o_ref[lo:lo + tq, cols] = (ol[:, :dh] / ol[:, dh:dh + 1]).astype(o_ref.dtype)


ATTN_HEADS_PER_STEP = 4


def _attn_prompt(q, k, v, f, n_seq, seq_len, n_heads, dh):
    m = n_seq * seq_len
    tq = _tile(seq_len, 256, LANE)
    hps = ATTN_HEADS_PER_STEP if n_heads % ATTN_HEADS_PER_STEP == 0 else 1
    groups = n_heads // hps
    blk = pl.BlockSpec((seq_len, hps * dh), lambda b, g: (b, g))
    return pl.pallas_call(
        functools.partial(_attn_prompt_kernel, tq=tq, dh=dh),
        grid=(n_seq, groups),
        in_specs=[blk, blk, blk, pl.BlockSpec((hps, 1, seq_len), lambda b, g: (b * groups + g, 0, 0))],
        out_specs=blk,
        out_shape=jax.ShapeDtypeStruct((m, n_heads * dh), BF16),
        compiler_params=_params("parallel", "parallel"),
        name="attn_prompt",
    )(q, k, v, f)


def _attn_sample_kernel(q_ref, kc_ref, vc_ref, fp_ref, kn_ref, vn_ref, fn_ref, o_ref, m_sc, l_sc, acc_sc, *, n_heads, dh):
    kt = pl.program_id(1)
    lq = q_ref.shape[0]

    @pl.when(kt == 0)
    def _():
        m_sc[...] = jnp.full(m_sc.shape, NEG_INF, F32)
        l_sc[...] = jnp.zeros(l_sc.shape, F32)
        acc_sc[...] = jnp.zeros(acc_sc.shape, F32)

    q = q_ref[...]

    def update(k_heads, v_heads, fk, mask):
        s_heads = []
        for h in range(n_heads):
            s = lax.dot_general(q[:, h * dh:(h + 1) * dh], k_heads[h], NT_DIMS, preferred_element_type=F32)
            s = s - fk[h:h + 1, :]
            if mask is not None:
                s = jnp.where(mask, s, NEG_INF)
            s_heads.append(s)
        s = jnp.concatenate(s_heads, axis=0)
        m_prev = m_sc[...]
        m_new = jnp.maximum(m_prev, jnp.max(s, axis=-1, keepdims=True))
        alpha = jnp.exp2(m_prev - m_new)
        p = jnp.exp2(s - m_new)
        l_sc[...] = alpha * l_sc[...] + jnp.sum(p, axis=-1, keepdims=True)
        pb = p.astype(BF16)
        o_heads = [jnp.dot(pb[h * lq:(h + 1) * lq], v_heads[h], preferred_element_type=F32) for h in range(n_heads)]
        acc_sc[...] = alpha * acc_sc[...] + jnp.concatenate(o_heads, axis=0)
        m_sc[...] = m_new

    def cached_heads(ref):
        x = jnp.swapaxes(ref[0, 0].astype(BF16), 0, 1)
        return [x[h] for h in range(n_heads)]

    update(cached_heads(kc_ref), cached_heads(vc_ref), fp_ref[0] * LOG2_E, None)

    @pl.when(kt == pl.num_programs(1) - 1)
    def _():
        pad = jnp.zeros((LANE - lq, n_heads * dh), BF16)
        kn = jnp.concatenate([kn_ref[...].astype(BF16), pad], axis=0)
        vn = jnp.concatenate([vn_ref[...].astype(BF16), pad], axis=0)
        mask = lax.broadcasted_iota(I32, (lq, LANE), 1) <= lax.broadcasted_iota(I32, (lq, LANE), 0)
        heads = lambda x: [x[:, h * dh:(h + 1) * dh] for h in range(n_heads)]
        update(heads(kn), heads(vn), fn_ref[0] * LOG2_E, mask)
        out = acc_sc[...] / l_sc[...]
        o_ref[...] = jnp.concatenate([out[h * lq:(h + 1) * lq] for h in range(n_heads)], axis=1).astype(o_ref.dtype)


def _attn_sample(q, cache_k, cache_v, layer, k_new, v_new, f_all, n_seq, lq, past, n_heads, dh):
    da = n_heads * dh
    tk = _tile(past, 1024, LANE)
    assert lq <= LANE and lq % BF16_ROWS == 0 and past % LANE == 0
    qblk = pl.BlockSpec((lq, da), lambda b, t: (b, 0))
    cblk = pl.BlockSpec((1, 1, tk, n_heads, dh), lambda b, t: (layer, b, t, 0, 0))
    return pl.pallas_call(
        functools.partial(_attn_sample_kernel, n_heads=n_heads, dh=dh),
        grid=(n_seq, past // tk),
        in_specs=[
            qblk,
            cblk,
            cblk,
            pl.BlockSpec((1, n_heads, tk), lambda b, t: (b, 0, t)),
            qblk,
            qblk,
            pl.BlockSpec((1, n_heads, LANE), lambda b, t: (b, 0, past // LANE)),
        ],
        out_specs=qblk,
        out_shape=jax.ShapeDtypeStruct((n_seq * lq, da), BF16),
        scratch_shapes=[
            pltpu.VMEM((n_heads * lq, 1), F32),
            pltpu.VMEM((n_heads * lq, 1), F32),
            pltpu.VMEM((n_heads * lq, dh), F32),
        ],
        compiler_params=_params("parallel", "arbitrary"),
        name="attn_sample",
    )(q, cache_k, cache_v, f_all, k_new, v_new, f_all)


def _merge_kernel(yc_ref, ya_ref, h_ref, wc_ref, wa_ref, wgc_ref, wga_ref, bgc_ref, bga_ref, o_ref):
    h = h_ref[...]
    a = jnp.dot(yc_ref[...], wc_ref[...], preferred_element_type=F32)
    b = jnp.dot(ya_ref[...], wa_ref[...], preferred_element_type=F32)
    gc = jax.nn.sigmoid(jnp.dot(h, wgc_ref[...], preferred_element_type=F32) + bgc_ref[...])
    ga = jax.nn.sigmoid(jnp.dot(h, wga_ref[...], preferred_element_type=F32) + bga_ref[...])
    o_ref[...] = (gc * a + ga * b).astype(o_ref.dtype)


def _merge(yc, ya, h, w_brc, w_bra, w_gate, b_gate):
    m, d = h.shape
    dc, da = yc.shape[1], ya.shape[1]
    tm = _tile(m, 1024, BF16_ROWS)
    tn = _tile(d, 512, LANE)
    nj = d // tn
    return pl.pallas_call(
        _merge_kernel,
        grid=(m // tm, nj),
        in_specs=[
            pl.BlockSpec((tm, dc), lambda i, j: (i, 0)),
            pl.BlockSpec((tm, da), lambda i, j: (i, 0)),
            pl.BlockSpec((tm, d), lambda i, j: (i, 0)),
            pl.BlockSpec((dc, tn), lambda i, j: (0, j)),
            pl.BlockSpec((da, tn), lambda i, j: (0, j)),
            pl.BlockSpec((d, tn), lambda i, j: (0, j)),
            pl.BlockSpec((d, tn), lambda i, j: (0, nj + j)),
            pl.BlockSpec((1, tn), lambda i, j: (0, j)),
            pl.BlockSpec((1, tn), lambda i, j: (0, nj + j)),
        ],
        out_specs=pl.BlockSpec((tm, tn), lambda i, j: (i, j)),
        out_shape=jax.ShapeDtypeStruct((m, d), BF16),
        compiler_params=_params("parallel", "parallel"),
        name="merge",
    )(yc, ya, h, w_brc, w_bra, w_gate, w_gate, b_gate, b_gate)


ROUTE_ID0, ROUTE_ID1, ROUTE_W0, ROUTE_W1 = 0, 1, 2, 3


def _wo_kernel(*refs, n_tiles_a, n_groups, per_group):
    a_refs, b_refs, (wo_ref, ng_ref, wr_ref, br_ref), outs = refs[0:5], refs[5:10], refs[10:14], refs[14:17]
    i = pl.program_id(0)

    @pl.when(i < n_tiles_a)
    def _():
        _wo_tile(*a_refs, wo_ref, ng_ref, wr_ref, br_ref, *outs, n_groups=n_groups, per_group=per_group)

    @pl.when(i >= n_tiles_a)
    def _():
        _wo_tile(*b_refs, wo_ref, ng_ref, wr_ref, br_ref, *outs, n_groups=n_groups, per_group=per_group)


WO_SUBTILES = 2


def _wo_tile(mg_ref, x_ref, g1_ref, sh_ref, sc_ref, wo_ref, ng_ref, wr_ref, br_ref, x1_ref, h2_ref, rt_ref, *, n_groups, per_group):
    tm = x_ref.shape[0]
    sub = tm // WO_SUBTILES if tm % (WO_SUBTILES * BF16_ROWS) == 0 else tm
    for r0 in range(0, tm, sub):
        rows = slice(r0, r0 + sub)
        vec = lambda ref: _bc(ref, tm) if ref.shape[0] == 1 else _bc(ref, tm)[rows, :]
        _wo_rows(mg_ref[rows, :], x_ref[rows, :], vec(g1_ref), vec(sh_ref), vec(sc_ref), wo_ref, ng_ref, wr_ref, br_ref,
                 x1_ref.at[rows, :], h2_ref.at[rows, :], rt_ref.at[rows, :], n_groups=n_groups, per_group=per_group)


def _wo_rows(mg, x, g1, sh, sc, wo_ref, ng_ref, wr_ref, br_ref, x1_ref, h2_ref, rt_ref, *, n_groups, per_group):
    o = jnp.dot(mg, wo_ref[...], preferred_element_type=F32)
    x1 = x + g1 * o
    x1_ref[...] = x1
    h2 = _rms(x1, ng_ref[...]) * (1.0 + sc) + sh
    h2_ref[...] = h2
    wr = wr_ref[...]
    hi = h2.astype(BF16)
    lo = (h2 - hi.astype(F32)).astype(BF16)
    r_hi = jnp.dot(hi, wr, preferred_element_type=F32)
    r_lo = jnp.dot(lo, wr[:, :LANE], preferred_element_type=F32)
    lg = r_hi[:, :LANE] + (r_hi[:, LANE:] + r_lo) + br_ref[...]
    col = lax.broadcasted_iota(I32, lg.shape, 1).astype(F32)
    big = float(LANE)
    gl = jnp.where(col < n_groups, lg, NEG_INF)
    gmax = jnp.max(gl, axis=-1, keepdims=True)
    g_val = 1.0 / jnp.sum(jnp.exp(gl - gmax), axis=-1, keepdims=True)
    g_idx = jnp.min(jnp.where(gl == gmax, col, big), axis=-1, keepdims=True)
    lo = n_groups + g_idx * per_group
    el = jnp.where((col >= lo) & (col < lo + per_group), lg, NEG_INF)
    m1 = jnp.max(el, axis=-1, keepdims=True)
    i1 = jnp.min(jnp.where(el == m1, col, big), axis=-1, keepdims=True)
    el2 = jnp.where(col == i1, NEG_INF, el)
    m2 = jnp.max(el2, axis=-1, keepdims=True)
    i2 = jnp.min(jnp.where(el2 == m2, col, big), axis=-1, keepdims=True)
    r = jnp.exp(m2 - m1)
    w0 = g_val / (1.0 + r)
    w1 = g_val * r / (1.0 + r)
    rt = jnp.where(col == ROUTE_ID0, i1 - n_groups, 0.0)
    rt = jnp.where(col == ROUTE_ID1, i2 - n_groups, rt)
    rt = jnp.where(col == ROUTE_W0, w0, rt)
    rt = jnp.where(col == ROUTE_W1, w1, rt)
    rt_ref[...] = rt


def _wo(path_a, path_b, w_o, norm_g, w_r, b_r, n_groups, per_group):
    d = w_o.shape[0]
    ma, mb = path_a[1].shape[0], path_b[1].shape[0]
    tm = _tile(min(ma, mb), 512, BF16_ROWS)
    assert ma % tm == 0 and mb % tm == 0
    na, nb = ma // tm, mb // tm
    tiles = (lambda i: jnp.minimum(i, na - 1), lambda i: jnp.maximum(i - na, 0))
    in_specs, args = [], []
    for (merged, x, gate1, shift2, scale2, seq_len), tile in zip((path_a, path_b), tiles):
        row = pl.BlockSpec((tm, d), lambda i, tile=tile: (tile(i), 0))
        in_specs += [row, row]
        args += [merged, x]
        for vec in (gate1, shift2, scale2):
            arr, spec = _rowvec(vec, seq_len, tm, tile)
            in_specs.append(spec)
            args.append(arr)
    const = lambda shape: pl.BlockSpec(shape, lambda i: (0, 0))
    in_specs += [pl.BlockSpec((d, d), lambda i: (0, 0), pipeline_mode=pl.Buffered(1)),
                 const((1, d)), const((d, 2 * LANE)), const((1, LANE))]
    args += [w_o, norm_g.reshape(1, d), w_r, b_r]
    row = pl.BlockSpec((tm, d), lambda i: (i, 0))
    return pl.pallas_call(
        functools.partial(_wo_kernel, n_tiles_a=na, n_groups=n_groups, per_group=per_group),
        grid=(na + nb,),
        in_specs=in_specs,
        out_specs=[row, row, pl.BlockSpec((tm, LANE), lambda i: (i, 0))],
        out_shape=[
            jax.ShapeDtypeStruct((ma + mb, d), F32),
            jax.ShapeDtypeStruct((ma + mb, d), F32),
            jax.ShapeDtypeStruct((ma + mb, LANE), F32),
        ],
        compiler_params=_params("arbitrary"),
        name="wo_norm_router",
    )(*args)


RING_SLOTS = 3
DMA_QUEUES = 2


def _gather_start(idx_ref, src_hbm, dst, slot, sem, n_rows, first=0):
    for r in range(first, first + n_rows):
        pltpu.make_async_copy(src_hbm.at[pl.ds(idx_ref[0, 0, r], 1)], dst.at[slot, pl.ds(r, 1)], sem.at[slot]).start(
            priority=r % DMA_QUEUES)


def _gather_wait(src_hbm, dst, slot, sem, n_rows):
    pltpu.make_async_copy(src_hbm.at[pl.ds(0, n_rows)], dst.at[slot], sem.at[slot]).wait()


def _expert_kernel(te_ref, tv_ref, tok0_ref, tok1_ref, tokn_ref, h2_hbm, w1_ref, w3_ref, w2_ref, o_ref, xbuf, w1b, w3b, w2b, sem):
    i = pl.program_id(0)
    tm = xbuf.shape[1]
    slot = lax.rem(i, RING_SLOTS)
    valid = tv_ref[i] == 1
    started = jnp.where(i < 2, tv_ref[0], tv_ref[jnp.maximum(i - 2, 0)]) == 1

    @pl.when(jnp.logical_and(i == 0, valid))
    def _():
        _gather_start(tok0_ref, h2_hbm, xbuf, 0, sem, tm)
        _gather_start(tok1_ref, h2_hbm, xbuf, 1, sem, tm)

    @pl.when(valid)
    def _():
        _gather_wait(h2_hbm, xbuf, slot, sem, tm)

        @pl.when(jnp.logical_or(i == 0, te_ref[i] != te_ref[jnp.maximum(i - 1, 0)]))
        def _():
            w1b[...] = w1_ref[0].astype(BF16)
            w3b[...] = w3_ref[0].astype(BF16)
            w2b[...] = w2_ref[0].astype(BF16)

        nslot = lax.rem(i + 2, RING_SLOTS)
        x = xbuf[slot].astype(BF16)
        a = jnp.dot(x, w1b[...], preferred_element_type=F32)
        _gather_start(tokn_ref, h2_hbm, xbuf, nslot, sem, tm // 2)
        b = jnp.dot(x, w3b[...], preferred_element_type=F32)
        _gather_start(tokn_ref, h2_hbm, xbuf, nslot, sem, tm // 2, first=tm // 2)
        hm = (jax.nn.silu(a) * b).astype(BF16)
        o_ref[...] = jnp.dot(hm, w2b[...], preferred_element_type=F32)

    @pl.when(jnp.logical_not(valid))
    def _():
        @pl.when(started)
        def _():
            _gather_wait(h2_hbm, xbuf, slot, sem, tm)

        o_ref[...] = jnp.zeros(o_ref.shape, F32)


def _experts(h2_all, tile_expert, tile_valid, tok_sorted, w1, w3, w2, tm):
    n_tiles = tile_expert.shape[0]
    _, d, de = w1.shape
    tok3 = tok_sorted.reshape(n_tiles, 1, tm)
    assert n_tiles >= RING_SLOTS
    idx_spec = lambda tile: pl.BlockSpec((1, 1, tm), lambda i, te, tv: (tile(i), 0, 0), memory_space=pltpu.SMEM)
    grid_spec = pltpu.PrefetchScalarGridSpec(
        num_scalar_prefetch=2,
        grid=(n_tiles,),
        in_specs=[
            idx_spec(lambda i: 0),
            idx_spec(lambda i: 1),
            idx_spec(lambda i: jnp.minimum(i + 2, n_tiles - 1)),
            pl.BlockSpec(memory_space=pl.ANY),
            pl.BlockSpec((1, d, de), lambda i, te, tv: (te[i], 0, 0)),
            pl.BlockSpec((1, d, de), lambda i, te, tv: (te[i], 0, 0)),
            pl.BlockSpec((1, de, d), lambda i, te, tv: (te[i], 0, 0)),
        ],
        out_specs=pl.BlockSpec((tm, d), lambda i, te, tv: (i, 0)),
        scratch_shapes=[
            pltpu.VMEM((RING_SLOTS, tm, d), F32),
            pltpu.VMEM((d, de), BF16),
            pltpu.VMEM((d, de), BF16),
            pltpu.VMEM((de, d), BF16),
            pltpu.SemaphoreType.DMA((RING_SLOTS,)),
        ],
    )
    return pl.pallas_call(
        _expert_kernel,
        grid_spec=grid_spec,
        out_shape=jax.ShapeDtypeStruct((n_tiles * tm, d), F32),
        compiler_params=_params("arbitrary"),
        name="experts",
    )(tile_expert, tile_valid, tok3, tok3, tok3, h2_all, w1, w3, w2)


COMBINE_CHUNKS = 4


def _combine_kernel(d_ref, dn_ref, ys_hbm, x1_ref, rt_ref, g2a_ref, g2b_ref, ng_ref, ya_ref, yb_ref, ybuf, sem, *, n_tiles_a):
    i = pl.program_id(0)
    tm = x1_ref.shape[0]
    slot = lax.rem(i, 2)

    @pl.when(i == 0)
    def _():
        _gather_start(d_ref, ys_hbm, ybuf, 0, sem, 2 * tm)

    _gather_wait(ys_hbm, ybuf, slot, sem, 2 * tm)
    ng = ng_ref[...]
    ck = tm // COMBINE_CHUNKS

    def tile(g2_ref, y_ref):
        g2 = _bc(g2_ref, tm)
        for c in range(COMBINE_CHUNKS):
            _gather_start(dn_ref, ys_hbm, ybuf, 1 - slot, sem, 2 * ck, first=2 * ck * c)
            rows = slice(c * ck, (c + 1) * ck)
            rt = rt_ref[rows, :]
            y0 = ybuf[slot, c * ck:(c + 1) * ck]
            y1 = ybuf[slot, tm + c * ck:tm + (c + 1) * ck]
            moe = rt[:, ROUTE_W0:ROUTE_W0 + 1] * y0 + rt[:, ROUTE_W1:ROUTE_W1 + 1] * y1
            g2c = g2 if g2.shape[0] == 1 else g2[rows]
            y_ref[rows, :] = _rms(x1_ref[rows, :] + g2c * moe, ng)

    @pl.when(i < n_tiles_a)
    def _():
        tile(g2a_ref, ya_ref)

    @pl.when(i >= n_tiles_a)
    def _():
        tile(g2b_ref, yb_ref)

    @pl.when(i == pl.num_programs(0) - 1)
    def _():
        _gather_wait(ys_hbm, ybuf, 1 - slot, sem, 2 * tm)


def _combine(ys, dest, x1_all, route_all, path_a, path_b, normf_g):
    d = x1_all.shape[1]
    (g2a, seq_a, ma), (g2b, seq_b, mb) = path_a, path_b
    tm = _tile(min(ma, mb, seq_a), 256, 8)
    assert ma % tm == 0 and mb % tm == 0 and tm % COMBINE_CHUNKS == 0 and (tm <= seq_b or tm % seq_b == 0)
    na, nb = ma // tm, mb // tm
    d3 = dest.reshape(2, na + nb, tm).transpose(1, 0, 2).reshape(na + nb, 1, 2 * tm)
    d3 = jnp.concatenate([d3, jnp.zeros((1, 1, 2 * tm), I32)], axis=0)
    tile_a, tile_b = (lambda i: jnp.minimum(i, na - 1)), (lambda i: jnp.maximum(i - na, 0))
    g2a, g2a_spec = _rowvec(g2a, seq_a, tm, tile_a)
    g2b, g2b_spec = _rowvec(g2b, seq_b, tm, tile_b)
    return pl.pallas_call(
        functools.partial(_combine_kernel, n_tiles_a=na),
        grid=(na + nb,),
        in_specs=[
            pl.BlockSpec((1, 1, 2 * tm), lambda i: (i, 0, 0), memory_space=pltpu.SMEM),
            pl.BlockSpec((1, 1, 2 * tm), lambda i: (i + 1, 0, 0), memory_space=pltpu.SMEM),
            pl.BlockSpec(memory_space=pl.ANY),
            pl.BlockSpec((tm, d), lambda i: (i, 0)),
            pl.BlockSpec((tm, LANE), lambda i: (i, 0)),
            g2a_spec,
            g2b_spec,
            pl.BlockSpec((1, d), lambda i: (0, 0)),
        ],
        out_specs=[pl.BlockSpec((tm, d), lambda i: (tile_a(i), 0)), pl.BlockSpec((tm, d), lambda i: (tile_b(i), 0))],
        out_shape=[jax.ShapeDtypeStruct((ma, d), F32), jax.ShapeDtypeStruct((mb, d), F32)],
        scratch_shapes=[pltpu.VMEM((2, 2 * tm, d), F32), pltpu.SemaphoreType.DMA((2,))],
        compiler_params=_params("arbitrary"),
        name="combine_final_norm",
    )(d3, d3, ys, x1_all, route_all, g2a, g2b, normf_g.reshape(1, d))


def _route_plan(route, n_experts, tm):
    t = route.shape[0]
    s = TOP_K * t
    e_flat = jnp.concatenate([route[:, ROUTE_ID0], route[:, ROUTE_ID1]]).astype(I32)
    onehot = (e_flat[:, None] == jnp.arange(n_experts, dtype=I32)[None, :]).astype(I32)
    csum = jnp.cumsum(onehot, axis=0)
    counts = csum[-1]
    padded = (counts + tm - 1) // tm * tm
    ends = jnp.cumsum(padded)
    dest = jnp.sum(onehot * (csum - 1 + (ends - padded)[None, :]), axis=1)
    n_tiles = pl.cdiv(s, tm) + n_experts + 2
    tile_start = jnp.arange(n_tiles, dtype=I32) * tm
    tile_valid = (tile_start < ends[-1]).astype(I32)
    te = jnp.minimum(jnp.sum((ends[None, :] <= tile_start[:, None]).astype(I32), axis=1), n_experts - 1)
    last_used = jnp.maximum(ends[-1] // tm - 1, 0)
    tile_expert = jnp.where(tile_valid == 1, te, te[last_used])
    tok_sorted = jnp.zeros((n_tiles * tm,), I32).at[dest].set(
        jnp.arange(s, dtype=I32) % t, unique_indices=True, mode="promise_in_bounds")
    return dest.reshape(TOP_K, t), tok_sorted, tile_expert, tile_valid


def kernel(x_prompt, x_sample, cache_k, cache_v, cache_logf, state_conv, c_prompt, c_sample, w_ada, b_ada, norm1_g, norm2_g, normf_g, w_in, conv_w, conv_b, b_forget, w_br_conv, w_br_attn, w_gate, b_gate, w_o, w_rg, b_rg, w_re, b_re, w1, w3, w2):
    depth = w_ada.shape[0]
    assert depth == 1, "single-layer step"
    l = 0
    bp, sp, d = x_prompt.shape
    bs, ss, _ = x_sample.shape
    _, _, past, n_heads, dh = cache_k.shape
    d_conv = state_conv.shape[-1]
    da = n_heads * dh
    n_groups = w_rg.shape[-1]
    n_experts = w_re.shape[-1]
    per_group = n_experts // n_groups
    assert n_groups + n_experts <= LANE
    scale = LOG2_E * dh ** -0.5

    w_in_bf = w_in[l].astype(BF16)
    q_off = 3 * d_conv
    w_q, w_k, w_v = (w_in_bf[:, q_off + j * da:q_off + (j + 1) * da] for j in range(3))
    hp = _round_up(n_heads, BF16_ROWS)
    wf_t = jnp.zeros((hp, d), BF16).at[:n_heads].set(w_in[l][:, 3 * d_conv + 3 * da:].T.astype(BF16))
    b_f = jnp.zeros((hp, 1), F32).at[:n_heads, 0].set(b_forget[l])
    w_brc_bf = w_br_conv[l].astype(BF16)
    w_bra_bf = w_br_attn[l].astype(BF16)
    w_gate_bf = w_gate[l].astype(BF16)
    b_gate2 = b_gate[l].reshape(1, 2 * d)
    w_o_bf = w_o[l].astype(BF16)
    w_r32 = jnp.pad(jnp.concatenate([w_rg[l], w_re[l]], axis=1), ((0, 0), (0, LANE - n_groups - n_experts)))
    w_r_hi = w_r32.astype(BF16)
    w_r = jnp.concatenate([w_r_hi, (w_r32 - w_r_hi.astype(F32)).astype(BF16)], axis=1)
    b_r = jnp.zeros((1, LANE), F32).at[0, :n_groups].set(b_rg[l]).at[0, n_groups:n_groups + n_experts].set(b_re[l])

    mod = _ada(jnp.concatenate([c_prompt, c_sample], axis=0), w_ada[l], b_ada[l])
    mods = {"p": jnp.split(mod[:bp], 6, axis=-1), "s": jnp.split(mod[bp:], 6, axis=-1)}

    def mixer_inputs(x, mod6, seq_len, prev):
        shift1, scale1 = mod6[0], mod6[1]
        h, lf_t = _norm(x, norm1_g[l], shift1, scale1, wf_t, b_f, seq_len)
        y_conv, tail = _conv_branch(h, w_in_bf, conv_w[l], conv_b[l], prev, seq_len, d_conv)
        return h, y_conv, tail, lf_t[:n_heads]

    xp = x_prompt.reshape(bp * sp, d)
    xs = x_sample.reshape(bs * ss, d)
    hp_, ycp, tailp, lftp = mixer_inputs(xp, mods["p"], sp, None)
    hs_, ycs, tails, lfts = mixer_inputs(xs, mods["s"], ss, state_conv[l])
    qp, qs = _proj(hp_, hs_, w_q, BF16, scale)
    kp, ks = _proj(hp_, hs_, w_k, F32)
    vp, vs = _proj(hp_, hs_, w_v, F32)

    lf_p = lftp.reshape(n_heads, bp, sp).transpose(1, 0, 2).reshape(bp * n_heads, sp)
    f_p = _cumsum_lanes(lf_p).reshape(bp * n_heads, 1, sp)
    yap = _attn_prompt(qp, kp, vp, f_p, bp, sp, n_heads, dh)

    lf_new = lfts.reshape(n_heads, bs, ss).transpose(1, 0, 2)
    lf_past = cache_logf[l].astype(F32).transpose(0, 2, 1)
    tot = _round_up(past + LANE, CUMSUM_CHUNK)
    lf_all = jnp.concatenate([lf_past, lf_new, jnp.zeros((bs, n_heads, tot - past - ss), F32)], axis=-1)
    f_s = _cumsum_lanes(lf_all.reshape(bs * n_heads, tot)).reshape(bs, n_heads, tot)
    yas = _attn_sample(qs, cache_k, cache_v, l, ks, vs, f_s, bs, ss, past, n_heads, dh)

    def wo_path(x, mod6, seq_len, h, yc, ya):
        gate1, shift2, scale2 = mod6[2], mod6[3], mod6[4]
        merged = _merge(yc, ya, h, w_brc_bf, w_bra_bf, w_gate_bf, b_gate2)
        return merged, x, gate1, shift2, scale2, seq_len

    x1_all, h2_all, route_all = _wo(
        wo_path(xp, mods["p"], sp, hp_, ycp, yap), wo_path(xs, mods["s"], ss, hs_, ycs, yas),
        w_o_bf, norm2_g[l], w_r, b_r, n_groups, per_group)
    tm_e = 128 if TOP_K * h2_all.shape[0] >= 8192 else 64
    dest, tok_sorted, tile_expert, tile_valid = _route_plan(route_all, n_experts, tm_e)
    ys = _experts(h2_all, tile_expert, tile_valid, tok_sorted, w1[l], w3[l], w2[l], tm_e)

    mp, ms = bp * sp, bs * ss
    y_p, y_s = _combine(ys, dest, x1_all, route_all, (mods["p"][5], sp, mp), (mods["s"][5], ss, ms), normf_g)

    def cache_outs(k, v, lf_t, tail, n_seq, seq_len):
        return (
            k.reshape(1, n_seq, seq_len, n_heads, dh),
            v.reshape(1, n_seq, seq_len, n_heads, dh),
            lf_t.T.reshape(1, n_seq, seq_len, n_heads),
            tail[:, CONV_TAIL_ROWS - 2:, :].reshape(1, n_seq, 2, d_conv),
        )

    return (y_p.reshape(bp, sp, d), y_s.reshape(bs, ss, d)) + cache_outs(kp, vp, lftp, tailp, bp, sp) + cache_outs(ks, vs, lfts, tails, bs, ss)
```
